```python
import math
import jax, jax.numpy as jnp
from jax import lax
import numpy as np

D_MODEL = 1024
BATCH = 8
SEQ = 8192
DEPTH = 2

HEAD_DIM = 64
FOX_HEADS = 4
FOX_WIDTH = FOX_HEADS * HEAD_DIM
S5_WIDTH = 256
S5_GROUP = 16
S5_GROUPS = S5_WIDTH // S5_GROUP
S5_STATE = 64
MOBA_HEADS = 4
MOBA_WIDTH = MOBA_HEADS * HEAD_DIM
MOBA_BLOCK = 256
MOBA_TOPK = 3
MOBA_Q_CHUNK = 64
MLA_HEADS = 4
MLA_NOPE = 64
MLA_ROPE = 32
MLA_V = 64
MLA_Q_RANK = 384
MLA_KV_RANK = 128
MLA_WIDTH = MLA_HEADS * MLA_V
D_MIX = FOX_WIDTH + S5_WIDTH + MOBA_WIDTH + MLA_WIDTH
Q_BLOCK = 128
ROPE_THETA = 10000.0
EPS = 1e-6
NEG = -1e30
SPLITS = (FOX_WIDTH, FOX_WIDTH, FOX_WIDTH, FOX_WIDTH, FOX_HEADS,
          S5_WIDTH, S5_WIDTH,
          MOBA_WIDTH, MOBA_WIDTH, MOBA_WIDTH, MOBA_WIDTH,
          MLA_Q_RANK, MLA_KV_RANK, MLA_ROPE, MLA_WIDTH)
D_IN = sum(SPLITS)
SPLIT_POINTS = [int(v) for v in np.cumsum(SPLITS)[:-1]]

kernel_name = 'hybrid_fox_s5_moba_mla_trunk'


def rmsnorm(x, g):
    xf = x.astype(jnp.float32)
    y = xf * lax.rsqrt(jnp.mean(xf * xf, axis=-1, keepdims=True) + EPS) * g.astype(jnp.float32)
    return y.astype(x.dtype)


def rope(x, pos):
    d = x.shape[-1]
    half = d // 2
    inv = jnp.power(ROPE_THETA, -jnp.arange(half, dtype=jnp.float32) / half)
    ang = pos.astype(jnp.float32)[:, None] * inv[None, :]
    cos = jnp.cos(ang)[None, :, None, :]
    sin = jnp.sin(ang)[None, :, None, :]
    x1 = x[..., :half].astype(jnp.float32)
    x2 = x[..., half:].astype(jnp.float32)
    return jnp.concatenate([x1 * cos - x2 * sin, x1 * sin + x2 * cos], axis=-1).astype(x.dtype)


def dense_causal_attention(q, k, v, log_f_cum=None):
    Bsz, S, H, Dk = q.shape
    nq = S // Q_BLOCK
    scale = Dk ** -0.5
    kpos = jnp.arange(S)
    qb = q.reshape(Bsz, nq, Q_BLOCK, H, Dk).swapaxes(0, 1)
    xs = (jnp.arange(nq), qb)
    if log_f_cum is not None:
        cum_k = log_f_cum.astype(jnp.float32).transpose(0, 2, 1)
        xs = xs + (cum_k.reshape(Bsz, H, nq, Q_BLOCK).transpose(2, 0, 1, 3),)

    def one_block(args):
        i, q_blk = args[0], args[1]
        s = jnp.einsum('bqhd,bkhd->bhqk', q_blk, k).astype(jnp.float32) * scale
        if log_f_cum is not None:
            s = s + args[2][..., None] - cum_k[:, :, None, :]
        qpos = i * Q_BLOCK + jnp.arange(Q_BLOCK)
        s = jnp.where(kpos[None, :] <= qpos[:, None], s, NEG)
        p = jax.nn.softmax(s, axis=-1)
        return jnp.einsum('bhqk,bkhd->bqhd', p.astype(v.dtype), v)

    out = lax.map(one_block, xs)
    return out.swapaxes(0, 1).reshape(Bsz, S, H, v.shape[-1])


def moba_attention(q, k, v):
    Bsz, S, H, D = q.shape
    nblk = -(-S // MOBA_BLOCK)
    pad = nblk * MOBA_BLOCK - S
    padw = ((0, 0), (0, pad), (0, 0), (0, 0))
    kb = jnp.pad(k, padw).reshape(Bsz, nblk, MOBA_BLOCK, H, D).transpose(0, 3, 1, 2, 4)
    vb = jnp.pad(v, padw).reshape(Bsz, nblk, MOBA_BLOCK, H, D).transpose(0, 3, 1, 2, 4)
    kmean = kb.astype(jnp.float32).mean(axis=3)
    topk = min(MOBA_TOPK, nblk)
    nchunk = S // MOBA_Q_CHUNK
    scale = D ** -0.5
    qc = q.reshape(Bsz, nchunk, MOBA_Q_CHUNK, H, D).transpose(1, 0, 3, 2, 4)
    blk_ids = jnp.arange(nblk)
    gather = jax.vmap(jax.vmap(lambda t, idx: t[idx]))

    def one_chunk(args):
        c, q_c = args
        start = c * MOBA_Q_CHUNK
        qpos = start + jnp.arange(MOBA_Q_CHUNK)
        blk = start // MOBA_BLOCK
        gate = jnp.einsum('bhqd,bhnd->bhqn', q_c.astype(jnp.float32), kmean)
        gate = jnp.where(blk_ids < blk, gate, -jnp.inf)
        _, sel = lax.top_k(gate, topk)
        valid = sel < blk
        k_sel = gather(kb, sel)
        v_sel = gather(vb, sel)
        s_sel = jnp.einsum('bhqd,bhqnkd->bhqnk', q_c, k_sel).astype(jnp.float32) * scale
        s_sel = jnp.where(valid[..., None], s_sel, NEG).reshape(Bsz, H, MOBA_Q_CHUNK, topk * MOBA_BLOCK)
        k_own = lax.dynamic_slice_in_dim(kb, blk, 1, axis=2)[:, :, 0]
        v_own = lax.dynamic_slice_in_dim(vb, blk, 1, axis=2)[:, :, 0]
        s_own = jnp.einsum('bhqd,bhkd->bhqk', q_c, k_own).astype(jnp.float32) * scale
        kpos_own = blk * MOBA_BLOCK + jnp.arange(MOBA_BLOCK)
        s_own = jnp.where(kpos_own[None, :] <= qpos[:, None], s_own, NEG)
        p = jax.nn.softmax(jnp.concatenate([s_sel, s_own], axis=-1), axis=-1).astype(v.dtype)
        p_sel = p[..., :topk * MOBA_BLOCK].reshape(Bsz, H, MOBA_Q_CHUNK, topk, MOBA_BLOCK)
        p_own = p[..., topk * MOBA_BLOCK:]
        return (jnp.einsum('bhqnk,bhqnkd->bhqd', p_sel, v_sel)
                + jnp.einsum('bhqk,bhkd->bhqd', p_own, v_own))

    out = lax.map(one_chunk, (jnp.arange(nchunk), qc))
    return out.transpose(1, 0, 3, 2, 4).reshape(Bsz, S, H, D)


def s5_mixer(u, a_re, a_im, log_dt, b_re, b_im, c_re, c_im, d, glu_w, glu_b):
    f32 = jnp.float32
    Bsz, S, _ = u.shape
    ug = u.astype(f32).reshape(Bsz, S, S5_GROUPS, S5_GROUP)
    lam = lax.complex(a_re.astype(f32), a_im.astype(f32))
    dt = jnp.exp(log_dt.astype(f32))[:, None]
    lam_bar = jnp.exp(lam * dt)
    b = lax.complex(b_re.astype(f32), b_im.astype(f32))
    b_bar = ((lam_bar - 1.0) / lam)[..., None] * b
    bu = jnp.einsum('bsgc,gpc->bsgp', ug.astype(jnp.complex64), b_bar)
    lam_seq = jnp.broadcast_to(lam_bar, bu.shape)

    def combine(left, right):
        a_l, x_l = left
        a_r, x_r = right
        return a_r * a_l, a_r * x_l + x_r

    _, states = lax.associative_scan(combine, (lam_seq, bu), axis=1)
    c = lax.complex(c_re.astype(f32), c_im.astype(f32))
    y = jnp.einsum('bsgp,gcp->bsgc', states, c).real.reshape(Bsz, S, S5_WIDTH)
    y = y + d.astype(f32) * u.astype(f32)
    y = jax.nn.gelu(y)
    y = y * jax.nn.sigmoid(y @ glu_w.astype(f32) + glu_b.astype(f32))
    return y.astype(u.dtype)


def hybrid_layer(x, pos, norm_g, w_in, fox_fb, s5_a_re, s5_a_im, s5_log_dt, s5_b_re, s5_b_im,
                 s5_c_re, s5_c_im, s5_d, s5_glu_w, s5_glu_b, mla_q_norm, mla_w_uq, mla_kv_norm,
                 mla_w_ukv, w_out):
    Bsz, S, _ = x.shape
    h = rmsnorm(x, norm_g)
    z = h @ w_in
    (fq, fk, fv, fg, ff, su, sg, mq, mk, mv, mg, cq, ckv, kr, lg) = jnp.split(z, SPLIT_POINTS, axis=-1)
    heads = lambda t, n: t.reshape(Bsz, S, n, -1)

    log_f = jax.nn.log_sigmoid(ff.astype(jnp.float32) + fox_fb.astype(jnp.float32))
    cum = jnp.cumsum(log_f, axis=1)
    a_out = dense_causal_attention(heads(fq, FOX_HEADS), heads(fk, FOX_HEADS), heads(fv, FOX_HEADS), cum)
    a_out = a_out.reshape(Bsz, S, FOX_WIDTH) * jax.nn.silu(fg)

    b_out = s5_mixer(su, s5_a_re, s5_a_im, s5_log_dt, s5_b_re, s5_b_im, s5_c_re, s5_c_im,
                     s5_d, s5_glu_w, s5_glu_b) * jax.nn.silu(sg)

    c_out = moba_attention(rope(heads(mq, MOBA_HEADS), pos), rope(heads(mk, MOBA_HEADS), pos),
                           heads(mv, MOBA_HEADS))
    c_out = c_out.reshape(Bsz, S, MOBA_WIDTH) * jax.nn.silu(mg)

    qf = (rmsnorm(cq, mla_q_norm) @ mla_w_uq).reshape(Bsz, S, MLA_HEADS, MLA_NOPE + MLA_ROPE)
    q_nope, q_r = qf[..., :MLA_NOPE], rope(qf[..., MLA_NOPE:], pos)
    kv = (rmsnorm(ckv, mla_kv_norm) @ mla_w_ukv).reshape(Bsz, S, MLA_HEADS, MLA_NOPE + MLA_V)
    k_nope, v_d = kv[..., :MLA_NOPE], kv[..., MLA_NOPE:]
    k_r = jnp.broadcast_to(rope(kr.reshape(Bsz, S, 1, MLA_ROPE), pos), (Bsz, S, MLA_HEADS, MLA_ROPE))
    d_out = dense_causal_attention(jnp.concatenate([q_nope, q_r], axis=-1),
                                   jnp.concatenate([k_nope, k_r], axis=-1), v_d)
    d_out = d_out.reshape(Bsz, S, MLA_WIDTH) * jax.nn.silu(lg)

    mix = jnp.concatenate([a_out, b_out, c_out, d_out], axis=-1)
    return x + (mix @ w_out).astype(x.dtype)


def setup_inputs(seed: int = 0) -> dict:
    key = jax.random.key(seed)
    ks = jax.random.split(key, 20)
    f32 = jnp.float32
    nrm = lambda k, shape, s: s * jax.random.normal(k, shape, f32)
    L, G, P, C = DEPTH, S5_GROUPS, S5_STATE, S5_GROUP
    x = jax.random.normal(ks[0], (BATCH, SEQ, D_MODEL), f32)
    norm_g = 1.0 + nrm(ks[1], (L, D_MODEL), 0.02)
    w_in = nrm(ks[2], (L, D_MODEL, D_IN), D_MODEL ** -0.5)
    fox_fb = 3.0 + nrm(ks[3], (L, FOX_HEADS), 0.1)
    s5_a_re = -0.5 + nrm(ks[4], (L, G, P), 0.01)
    s5_a_im = jnp.pi * jnp.arange(P, dtype=f32) + nrm(ks[5], (L, G, P), 0.01)
    s5_log_dt = jax.random.uniform(ks[6], (L, G), f32, math.log(1e-3), math.log(1e-1))
    s5_b_re = nrm(ks[7], (L, G, P, C), (2 * C) ** -0.5)
    s5_b_im = nrm(ks[8], (L, G, P, C), (2 * C) ** -0.5)
    s5_c_re = nrm(ks[9], (L, G, C, P), (2 * P) ** -0.5)
    s5_c_im = nrm(ks[10], (L, G, C, P), (2 * P) ** -0.5)
    s5_d = nrm(ks[11], (L, S5_WIDTH), 1.0)
    s5_glu_w = nrm(ks[12], (L, S5_WIDTH, S5_WIDTH), S5_WIDTH ** -0.5)
    s5_glu_b = nrm(ks[13], (L, S5_WIDTH), 0.02)
    mla_q_norm = 1.0 + nrm(ks[14], (L, MLA_Q_RANK), 0.02)
    mla_w_uq = nrm(ks[15], (L, MLA_Q_RANK, MLA_HEADS * (MLA_NOPE + MLA_ROPE)), MLA_Q_RANK ** -0.5)
    mla_kv_norm = 1.0 + nrm(ks[16], (L, MLA_KV_RANK), 0.02)
    mla_w_ukv = nrm(ks[17], (L, MLA_KV_RANK, MLA_HEADS * (MLA_NOPE + MLA_V)), MLA_KV_RANK ** -0.5)
    w_out = nrm(ks[18], (L, D_MIX, D_MODEL), D_MIX ** -0.5)
    final_g = 1.0 + nrm(ks[19], (D_MODEL,), 0.02)
    return {'x': x, 'norm_g': norm_g, 'w_in': w_in, 'fox_fb': fox_fb,
            's5_a_re': s5_a_re, 's5_a_im': s5_a_im, 's5_log_dt': s5_log_dt,
            's5_b_re': s5_b_re, 's5_b_im': s5_b_im, 's5_c_re': s5_c_re, 's5_c_im': s5_c_im,
            's5_d': s5_d, 's5_glu_w': s5_glu_w, 's5_glu_b': s5_glu_b,
            'mla_q_norm': mla_q_norm, 'mla_w_uq': mla_w_uq, 'mla_kv_norm': mla_kv_norm,
            'mla_w_ukv': mla_w_ukv, 'w_out': w_out, 'final_g': final_g}


def reference(x, norm_g, w_in, fox_fb, s5_a_re, s5_a_im, s5_log_dt, s5_b_re, s5_b_im, s5_c_re,
              s5_c_im, s5_d, s5_glu_w, s5_glu_b, mla_q_norm, mla_w_uq, mla_kv_norm, mla_w_ukv,
              w_out, final_g):
    pos = jnp.arange(x.shape[1])
    for l in range(DEPTH):
        x = hybrid_layer(x, pos, norm_g[l], w_in[l], fox_fb[l], s5_a_re[l], s5_a_im[l], s5_log_dt[l],
                         s5_b_re[l], s5_b_im[l], s5_c_re[l], s5_c_im[l], s5_d[l], s5_glu_w[l],
                         s5_glu_b[l], mla_q_norm[l], mla_w_uq[l], mla_kv_norm[l], mla_w_ukv[l], w_out[l])
    return rmsnorm(x, final_g)
```

```python
import functools

import jax
import jax.numpy as jnp
from jax import lax
from jax.experimental import pallas as pl
from jax.experimental.pallas import tpu as pltpu

F32 = jnp.float32
BF16 = jnp.bfloat16

D_MODEL = 1024
HEAD_DIM = 64
N_HEADS = 4
WIDTH = 256
S5_GROUPS = 16
S5_GROUP = 16
S5_STATE = 64
S5_NSTATE = S5_GROUPS * S5_STATE
MOBA_BLOCK = 256
MOBA_TOPK = 3
MLA_NOPE = 64
MLA_ROPE = 32
MLA_V = 64
MLA_Q_RANK = 384
MLA_KV_RANK = 128
ROPE_THETA = 10000.0
EPS = 1e-6
NEG = -1e30
LANES = 128
VMEM_LIMIT = 56 * 1024 * 1024

C_FQ, C_FK, C_FV, C_FG = 0, 256, 512, 768
C_FF = 1024
C_SU, C_SG = 1152, 1408
C_MQ, C_MK, C_MV, C_MG = 1664, 1920, 2176, 2432
C_CQ = 2688
C_CKV = 3072
C_KR = 3200
C_LG = 3328
N_COLS = 3584


def _split3(x):
    x1 = x.astype(BF16)
    r1 = x - x1.astype(F32)
    x2 = r1.astype(BF16)
    r2 = r1 - x2.astype(F32)
    return x1, x2, r2.astype(BF16)


def _rope(x, cos, sin_lo, sin_hi, half):
    return (x * cos + pltpu.roll(x, LANES - half, 1) * sin_lo + pltpu.roll(x, half, 1) * sin_hi)


def _rms(x, g):
    return x * lax.rsqrt(jnp.mean(x * x, axis=-1, keepdims=True) + EPS) * g


def _silu(g):
    return g * (1.0 / (1.0 + jnp.exp(-g)))


def _in_kernel(x_ref, g_ref, w_ref, fb_ref, tri_ref, route_ref, bconst_ref,
               mc_ref, msl_ref, msh_ref, lc_ref, lsl_ref, lsh_ref,
               gq_ref, gkv_ref, wuq_ref, wukv_ref,
               foxq_ref, foxk_ref, foxv_ref, fg_ref, su_ref, sg_ref,
               mobq_ref, mobk_ref, mobv_ref, mg_ref,
               mlaq_ref, mlak_ref, mlav_ref, lg_ref,
               carry_ref, kme_ref, kmo_ref, *, tm):
    sblk = pl.program_id(1)
    nblk = tm // MOBA_BLOCK

    @pl.when(sblk == 0)
    def _():
        carry_ref[...] = jnp.zeros_like(carry_ref)
        kme_ref[...] = jnp.zeros_like(kme_ref)
        kmo_ref[...] = jnp.zeros_like(kmo_ref)

    h = _rms(x_ref[0], g_ref[...]).astype(BF16)

    def proj(c0, width):
        return jnp.dot(h, w_ref[:, c0:c0 + width], preferred_element_type=F32)

    lane = lax.broadcasted_iota(jnp.int32, (tm, LANES), 1)
    lanef = lane.astype(F32)
    row = lax.broadcasted_iota(jnp.int32, (tm, LANES), 0)
    low_half = lane < HEAD_DIM

    foxv_ref[0] = proj(C_FV, WIDTH).astype(BF16)
    mobv_ref[0] = proj(C_MV, WIDTH).astype(BF16)
    fg_ref[0] = proj(C_FG, WIDTH)
    sg_ref[0] = proj(C_SG, WIDTH)
    mg_ref[0] = proj(C_MG, WIDTH)
    lg_ref[0] = proj(C_LG, WIDTH)
    su_ref[...] = proj(C_SU, WIDTH)

    ff = proj(C_FF, LANES) + fb_ref[...]
    logf = -(jnp.maximum(-ff, 0.0) + jnp.log1p(jnp.exp(-jnp.abs(ff))))
    l1, l2, l3 = _split3(logf)
    tri = tri_ref[...]
    within = (jnp.dot(tri, l1, preferred_element_type=F32)
              + jnp.dot(tri, l2, preferred_element_type=F32)
              + jnp.dot(tri, l3, preferred_element_type=F32))
    cum = within + carry_ref[...]
    carry_ref[...] = cum[tm - 1:tm, :]
    c1, c2, c3 = _split3(cum)
    routed = jnp.dot(jnp.concatenate([c1, c2, c3], axis=1), route_ref[...],
                     preferred_element_type=F32) + bconst_ref[...]
    fq = proj(C_FQ, WIDTH) * (HEAD_DIM ** -0.5)
    fk = proj(C_FK, WIDTH)
    for hd in range(N_HEADS):
        pair = hd // 2
        own = low_half if hd % 2 == 0 else jnp.logical_not(low_half)
        sl = slice(pair * LANES, (pair + 1) * LANES)
        qa = jnp.where(own, fq[:, sl], routed[:, hd * LANES:(hd + 1) * LANES])
        ka = jnp.where(own, fk[:, sl], routed[:, (N_HEADS + hd) * LANES:(N_HEADS + hd + 1) * LANES])
        foxq_ref[0, :, hd * LANES:(hd + 1) * LANES] = qa.astype(BF16)
        foxk_ref[0, :, hd * LANES:(hd + 1) * LANES] = ka.astype(BF16)

    mc, msl, msh = mc_ref[...], msl_ref[...], msh_ref[...]
    mq = proj(C_MQ, WIDTH)
    mk = proj(C_MK, WIDTH)
    q_r = [_rope(mq[:, p * LANES:(p + 1) * LANES], mc, msl, msh, HEAD_DIM // 2) for p in range(2)]
    k_r = [_rope(mk[:, p * LANES:(p + 1) * LANES], mc, msl, msh, HEAD_DIM // 2) for p in range(2)]
    km_row = lax.broadcasted_iota(jnp.int32, (LANES, WIDTH), 0)
    kme, kmo = kme_ref[...], kmo_ref[...]
    for nb in range(nblk):
        blk = sblk * nblk + nb
        km = jnp.concatenate(
            [jnp.mean(k_r[p][nb * MOBA_BLOCK:(nb + 1) * MOBA_BLOCK, :], axis=0, keepdims=True) for p in range(2)],
            axis=1)
        kme = jnp.where(km_row == HEAD_DIM + blk, km, kme)
        kmo = jnp.where(km_row == blk, km, kmo)
    kme_ref[...] = kme
    kmo_ref[...] = kmo
    blk_row = (sblk * nblk + lax.shift_right_logical(row, 8)).astype(F32)
    for hd in range(N_HEADS):
        pair = hd // 2
        even = hd % 2 == 0
        own = low_half if even else jnp.logical_not(low_half)
        km_ref = kme_ref if even else kmo_ref
        qh = jnp.where(own, q_r[pair], 0.0)
        gate = lax.dot_general(qh, km_ref[:, pair * LANES:(pair + 1) * LANES],
                               (((1,), (1,)), ((), ())), precision=lax.Precision.HIGHEST,
                               preferred_element_type=F32)
        rel = lanef - (float(HEAD_DIM) if even else 0.0)
        g = jnp.where(rel >= 0.0, jnp.where(rel < blk_row, gate, -jnp.inf), -jnp.inf)
        chosen = jnp.zeros((tm, LANES), F32)
        for _ in range(MOBA_TOPK):
            m = jnp.max(g, axis=1, keepdims=True)
            first = jnp.min(jnp.where(g == m, lanef, 1e9), axis=1, keepdims=True)
            first = jnp.where(m > -jnp.inf, first, -1.0)
            pick = lanef == first
            chosen = jnp.where(pick, 1.0, chosen)
            g = jnp.where(pick, -jnp.inf, g)
        is_own_blk = rel == blk_row
        keep = jnp.where(is_own_blk, 1.0, chosen)
        sel_bias = jnp.where(keep > 0.0, 0.0, NEG)
        onehot = jnp.where(is_own_blk, 1.0, 0.0)
        qa = jnp.where(own, q_r[pair] * (HEAD_DIM ** -0.5), sel_bias)
        ka = jnp.where(own, k_r[pair], onehot)
        mobq_ref[0, :, hd * LANES:(hd + 1) * LANES] = qa.astype(BF16)
        mobk_ref[0, :, hd * LANES:(hd + 1) * LANES] = ka.astype(BF16)

    lc, lsl, lsh = lc_ref[...], lsl_ref[...], lsh_ref[...]
    cqn = _rms(proj(C_CQ, MLA_Q_RANK), gq_ref[...]).astype(BF16)
    qf = jnp.dot(cqn, wuq_ref[...], preferred_element_type=F32)
    ckvn = _rms(proj(C_CKV, MLA_KV_RANK), gkv_ref[...]).astype(BF16)
    kv = jnp.dot(ckvn, wukv_ref[...], preferred_element_type=F32)
    kr = _rope(proj(C_KR, LANES), lc, lsl, lsh, MLA_ROPE // 2)
    scale = (MLA_NOPE + MLA_ROPE) ** -0.5
    for hd in range(N_HEADS):
        sl = slice(hd * LANES, (hd + 1) * LANES)
        mlaq_ref[0, :, sl] = (_rope(qf[:, sl], lc, lsl, lsh, MLA_ROPE // 2) * scale).astype(BF16)
        mlak_ref[0, :, sl] = (kv[:, sl] + kr).astype(BF16)
    mlav_ref[0] = kv[:, N_HEADS * LANES:].astype(BF16)


def _attn_kernel(q_ref, k_ref, v_ref, g_ref, o_ref, *, tq, tk):
    i = pl.program_id(2)
    n_diag = tq // tk
    n_full = i * n_diag
    lane = lax.broadcasted_iota(jnp.int32, (1, LANES), 1)
    qpos = i * tq + lax.broadcasted_iota(jnp.int32, (tq, tk), 0)
    kloc = lax.broadcasted_iota(jnp.int32, (tq, tk), 1)
    out = jnp.zeros((tq, LANES), F32)
    for hh in range(2):
        q = q_ref[0, :, hh * LANES:(hh + 1) * LANES]
        vsel = jnp.where((lane < HEAD_DIM) if hh == 0 else (lane >= HEAD_DIM), 1.0, 0.0).astype(BF16)

        def step(j, carry, masked, hh=hh, q=q, vsel=vsel):
            m, l, acc = carry
            off = pl.multiple_of(j * tk, tk)
            k = k_ref[0, pl.ds(off, tk), hh * LANES:(hh + 1) * LANES]
            v = v_ref[0, pl.ds(off, tk), :] * vsel
            s = lax.dot_general(q, k, (((1,), (1,)), ((), ())), preferred_element_type=F32)
            if masked:
                s = jnp.where(kloc + j * tk <= qpos, s, NEG)
            m_new = jnp.maximum(m, jnp.max(s, axis=1, keepdims=True))
            alpha = jnp.exp(m - m_new)
            p = jnp.exp(s - m_new)
            l = alpha * l + jnp.sum(p, axis=1, keepdims=True)
            acc = alpha * acc + jnp.dot(p.astype(BF16), v, preferred_element_type=F32)
            return m_new, l, acc

        carry = (jnp.full((tq, 1), -jnp.inf, F32), jnp.zeros((tq, 1), F32), jnp.zeros((tq, LANES), F32))
        carry = lax.fori_loop(0, n_full, functools.partial(step, masked=False), carry)
        carry = lax.fori_loop(n_full, n_full + n_diag, functools.partial(step, masked=True), carry)
        _, l, acc = carry
        out = out + acc / l
    o_ref[0] = (out * _silu(g_ref[0])).astype(BF16)


def _s5_kernel(u_ref, bm_ref, lam_ref, cm_ref, d_ref, gw_ref, gb_ref, o_ref, x_ref, st_ref, *, steps, nbatch):
    @pl.when(pl.program_id(0) == 0)
    def _():
        st_ref[...] = jnp.zeros_like(st_ref)

    u = u_ref[...]
    x_ref[...] = jnp.dot(u.astype(BF16), bm_ref[...], preferred_element_type=F32)
    lam_re = jnp.broadcast_to(lam_ref[0:1, :], (nbatch, S5_NSTATE))
    lam_im = jnp.broadcast_to(lam_ref[1:2, :], (nbatch, S5_NSTATE))

    def body(t, carry):
        xr, xi = carry
        off = pl.multiple_of(t * nbatch, nbatch)
        nr = lam_re * xr - lam_im * xi + x_ref[pl.ds(off, nbatch), 0:S5_NSTATE]
        ni = lam_re * xi + lam_im * xr + x_ref[pl.ds(off, nbatch), S5_NSTATE:2 * S5_NSTATE]
        x_ref[pl.ds(off, nbatch), 0:S5_NSTATE] = nr
        x_ref[pl.ds(off, nbatch), S5_NSTATE:2 * S5_NSTATE] = ni
        return nr, ni

    xr, xi = lax.fori_loop(0, steps, body, (st_ref[:, 0:S5_NSTATE], st_ref[:, S5_NSTATE:2 * S5_NSTATE]))
    st_ref[:, 0:S5_NSTATE] = xr
    st_ref[:, S5_NSTATE:2 * S5_NSTATE] = xi

    y = jnp.dot(x_ref[...].astype(BF16), cm_ref[...], preferred_element_type=F32) + d_ref[...] * u
    y = 0.5 * y * (1.0 + jnp.tanh(0.7978845608028654 * (y + 0.044715 * (y * y * y))))
    z = jnp.dot(y.astype(BF16), gw_ref[...], preferred_element_type=F32) + gb_ref[...]
    o_ref[...] = y * (1.0 / (1.0 + jnp.exp(-z)))


def _out_kernel(x_ref, a_ref, b_ref, sg_ref, c_ref, d_ref, w_ref, fg_ref, o_ref, *, final):
    b = (b_ref[...] * _silu(sg_ref[0])).astype(BF16)
    y = (jnp.dot(a_ref[0], w_ref[0:WIDTH, :], preferred_element_type=F32)
         + jnp.dot(b, w_ref[WIDTH:2 * WIDTH, :], preferred_element_type=F32)
         + jnp.dot(c_ref[0], w_ref[2 * WIDTH:3 * WIDTH, :], preferred_element_type=F32)
         + jnp.dot(d_ref[0], w_ref[3 * WIDTH:4 * WIDTH, :], preferred_element_type=F32))
    xn = x_ref[0] + y
    if final:
        xn = _rms(xn, fg_ref[...])
    o_ref[0] = xn


def _tiles(seq):
    tm = 512 if seq % 512 == 0 else MOBA_BLOCK
    tq = 512 if seq % 512 == 0 else MOBA_BLOCK
    tk = 512 if seq % 512 == 0 else MOBA_BLOCK
    steps = 128
    return tm, tq, tk, steps


def _arrange_w_in(w):
    z = lambda n: jnp.zeros((D_MODEL, n), w.dtype)
    o = 0
    parts = {}
    for name, n in (("fq", 256), ("fk", 256), ("fv", 256), ("fg", 256), ("ff", 4), ("su", 256), ("sg", 256),
                    ("mq", 256), ("mk", 256), ("mv", 256), ("mg", 256), ("cq", 384), ("ckv", 128), ("kr", 32),
                    ("lg", 256)):
        parts[name] = w[:, o:o + n]
        o += n
    cols = [parts["fq"], parts["fk"], parts["fv"], parts["fg"], parts["ff"], z(LANES - 4), parts["su"], parts["sg"],
            parts["mq"], parts["mk"], parts["mv"], parts["mg"], parts["cq"], parts["ckv"],
            z(MLA_NOPE), parts["kr"], z(LANES - MLA_NOPE - MLA_ROPE), parts["lg"]]
    return jnp.concatenate(cols, axis=1).astype(BF16)


def _fox_routing():
    import numpy as np
    route = np.zeros((3 * LANES, 2 * N_HEADS * LANES), np.float32)
    const = np.zeros((1, 2 * N_HEADS * LANES), np.float32)
    for hd in range(N_HEADS):
        base = HEAD_DIM if hd % 2 == 0 else 0
        qs, ks = hd * LANES, (N_HEADS + hd) * LANES
        for part in range(3):
            route[part * LANES + hd, qs + base + part] = 1.0
            const[0, qs + base + 3 + part] = 1.0
            const[0, ks + base + part] = 1.0
            route[part * LANES + hd, ks + base + 3 + part] = -1.0
    return jnp.asarray(route, BF16), jnp.asarray(const, F32)


def _rope_tables(seq):
    pos = jnp.arange(seq).astype(F32)[:, None]
    lane = jnp.arange(LANES)
    half = HEAD_DIM // 2
    inv = jnp.power(ROPE_THETA, -jnp.arange(half, dtype=F32) / half)
    ang = pos * inv[None, :]
    cos, sin = jnp.cos(ang)[:, lane % half], jnp.sin(ang)[:, lane % half]
    lo = (lane % HEAD_DIM) < half
    moba = (cos, jnp.where(lo, -sin, 0.0), jnp.where(lo, 0.0, sin))
    half = MLA_ROPE // 2
    inv = jnp.power(ROPE_THETA, -jnp.arange(half, dtype=F32) / half)
    ang = pos * inv[None, :]
    cos, sin = jnp.cos(ang)[:, lane % half], jnp.sin(ang)[:, lane % half]
    in_lo = (lane >= MLA_NOPE) & (lane < MLA_NOPE + half)
    in_hi = (lane >= MLA_NOPE + half) & (lane < MLA_NOPE + MLA_ROPE)
    mla = (jnp.where(in_lo | in_hi, cos, 1.0), jnp.where(in_lo, -sin, 0.0), jnp.where(in_hi, sin, 0.0))
    return moba, mla


def _arrange_mla(w_uq, w_ukv):
    z = lambda r, n: jnp.zeros((r, n), F32)
    dq = MLA_NOPE + MLA_ROPE
    q_cols, k_cols, v_cols = [], [], []
    for hd in range(N_HEADS):
        q_cols += [w_uq[:, hd * dq:(hd + 1) * dq], z(MLA_Q_RANK, LANES - dq)]
        base = hd * (MLA_NOPE + MLA_V)
        k_cols += [w_ukv[:, base:base + MLA_NOPE], z(MLA_KV_RANK, LANES - MLA_NOPE)]
        v_cols += [w_ukv[:, base + MLA_NOPE:base + MLA_NOPE + MLA_V]]
    return (jnp.concatenate(q_cols, axis=1).astype(BF16),
            jnp.concatenate(k_cols + v_cols, axis=1).astype(BF16))


def _s5_matrices(a_re, a_im, log_dt, b_re, b_im, c_re, c_im):
    lam = lax.complex(a_re.astype(F32), a_im.astype(F32))
    dt = jnp.exp(log_dt.astype(F32))[:, None]
    lam_bar = jnp.exp(lam * dt)
    b_bar = ((lam_bar - 1.0) / lam)[..., None] * lax.complex(b_re.astype(F32), b_im.astype(F32))
    eye = jnp.eye(S5_GROUPS, dtype=F32)
    blockdiag_in = lambda t: jnp.einsum('gpc,gh->gchp', t, eye).reshape(WIDTH, S5_NSTATE)
    blockdiag_out = lambda t: jnp.einsum('gcp,gh->gphc', t, eye).reshape(S5_NSTATE, WIDTH)
    bm = jnp.concatenate([blockdiag_in(b_bar.real), blockdiag_in(b_bar.imag)], axis=1).astype(BF16)
    cm = jnp.concatenate([blockdiag_out(c_re.astype(F32)), -blockdiag_out(c_im.astype(F32))], axis=0).astype(BF16)
    lam2 = jnp.stack([lam_bar.real.reshape(S5_NSTATE), lam_bar.imag.reshape(S5_NSTATE)], axis=0)
    return bm, lam2, cm


def _full(shape):
    return pl.BlockSpec(shape, lambda *_: (0,) * len(shape))


def _in_call(x, g, w, fb, tri, route, bconst, moba_t, mla_t, gq, gkv, wuq, wukv, tm):
    nb, seq, _ = x.shape
    tok = lambda width: pl.BlockSpec((1, tm, width), lambda b, s: (b, s, 0))
    tab = pl.BlockSpec((tm, LANES), lambda b, s: (s, 0))
    bf = lambda width: jax.ShapeDtypeStruct((nb, seq, width), BF16)
    f32 = lambda width: jax.ShapeDtypeStruct((nb, seq, width), F32)
    out_shape = (bf(512), bf(512), bf(256), f32(256),
                 jax.ShapeDtypeStruct((seq, nb * WIDTH), F32), f32(256),
                 bf(512), bf(512), bf(256), f32(256),
                 bf(512), bf(512), bf(256), f32(256))
    out_specs = (tok(512), tok(512), tok(256), tok(256),
                 pl.BlockSpec((tm, WIDTH), lambda b, s: (s, b)), tok(256),
                 tok(512), tok(512), tok(256), tok(256),
                 tok(512), tok(512), tok(256), tok(256))
    in_specs = [tok(D_MODEL), _full((1, D_MODEL)), _full((D_MODEL, N_COLS)), _full((1, LANES)),
                _full((tm, tm)), _full((3 * LANES, 2 * N_HEADS * LANES)), _full((1, 2 * N_HEADS * LANES)),
                tab, tab, tab, tab, tab, tab,
                _full((1, MLA_Q_RANK)), _full((1, MLA_KV_RANK)),
                _full((MLA_Q_RANK, N_HEADS * LANES)), _full((MLA_KV_RANK, N_HEADS * LANES + WIDTH))]
    return pl.pallas_call(
        functools.partial(_in_kernel, tm=tm),
        grid=(nb, seq // tm),
        in_specs=in_specs, out_specs=out_specs, out_shape=out_shape,
        scratch_shapes=[pltpu.VMEM((1, LANES), F32), pltpu.VMEM((LANES, WIDTH), F32),
                        pltpu.VMEM((LANES, WIDTH), F32)],
        compiler_params=pltpu.CompilerParams(dimension_semantics=("arbitrary", "arbitrary"),
                                             vmem_limit_bytes=VMEM_LIMIT),
        name="in_proj",
    )(x, g, w, fb, tri, route, bconst, *moba_t, *mla_t, gq, gkv, wuq, wukv)


def _attn_call(q, k, v, gate, tq, tk, name):
    nb, seq, _ = q.shape
    return pl.pallas_call(
        functools.partial(_attn_kernel, tq=tq, tk=tk),
        grid=(nb, 2, seq // tq),
        in_specs=[pl.BlockSpec((1, tq, 2 * LANES), lambda b, p, i: (b, i, p)),
                  pl.BlockSpec((1, seq, 2 * LANES), lambda b, p, i: (b, 0, p)),
                  pl.BlockSpec((1, seq, LANES), lambda b, p, i: (b, 0, p)),
                  pl.BlockSpec((1, tq, LANES), lambda b, p, i: (b, i, p))],
        out_specs=pl.BlockSpec((1, tq, LANES), lambda b, p, i: (b, i, p)),
        out_shape=jax.ShapeDtypeStruct((nb, seq, WIDTH), BF16),
        compiler_params=pltpu.CompilerParams(dimension_semantics=("arbitrary", "arbitrary", "arbitrary"),
                                             vmem_limit_bytes=VMEM_LIMIT),
        name=name,
    )(q, k, v, gate)


def _s5_call(u, bm, lam2, cm, d, gw, gb, steps, nbatch):
    rows = u.shape[0]
    blk = steps * nbatch
    return pl.pallas_call(
        functools.partial(_s5_kernel, steps=steps, nbatch=nbatch),
        grid=(rows // blk,),
        in_specs=[pl.BlockSpec((blk, WIDTH), lambda c: (c, 0)),
                  _full((WIDTH, 2 * S5_NSTATE)), _full((2, S5_NSTATE)), _full((2 * S5_NSTATE, WIDTH)),
                  _full((1, WIDTH)), _full((WIDTH, WIDTH)), _full((1, WIDTH))],
        out_specs=pl.BlockSpec((blk, WIDTH), lambda c: (c, 0)),
        out_shape=jax.ShapeDtypeStruct((rows, WIDTH), F32),
        scratch_shapes=[pltpu.VMEM((blk, 2 * S5_NSTATE), F32), pltpu.VMEM((nbatch, 2 * S5_NSTATE), F32)],
        compiler_params=pltpu.CompilerParams(dimension_semantics=("arbitrary",),
                                             vmem_limit_bytes=VMEM_LIMIT),
        name="s5",
    )(u, bm, lam2, cm, d, gw, gb)


def _out_call(x, a, b2d, sg, c, d, w, fg, tm, final):
    nb, seq, _ = x.shape
    tok = lambda width: pl.BlockSpec((1, tm, width), lambda b, s: (b, s, 0))
    return pl.pallas_call(
        functools.partial(_out_kernel, final=final),
        grid=(nb, seq // tm),
        in_specs=[tok(D_MODEL), tok(WIDTH), pl.BlockSpec((tm, WIDTH), lambda b, s: (s, b)), tok(WIDTH),
                  tok(WIDTH), tok(WIDTH), _full((4 * WIDTH, D_MODEL)), _full((1, D_MODEL))],
        out_specs=tok(D_MODEL),
        out_shape=jax.ShapeDtypeStruct(x.shape, F32),
        compiler_params=pltpu.CompilerParams(dimension_semantics=("arbitrary", "arbitrary"),
                                             vmem_limit_bytes=VMEM_LIMIT),
        name="out_proj",
    )(x, a, b2d, sg, c, d, w, fg)


def kernel(x, norm_g, w_in, fox_fb, s5_a_re, s5_a_im, s5_log_dt, s5_b_re, s5_b_im, s5_c_re, s5_c_im, s5_d,
           s5_glu_w, s5_glu_b, mla_q_norm, mla_w_uq, mla_kv_norm, mla_w_ukv, w_out, final_g):
    nb, seq, _ = x.shape
    depth = norm_g.shape[0]
    assert nb == 8, "the S5 recurrence keeps the batch on the 8 sublanes of a vreg"
    tm, tq, tk, steps = _tiles(seq)
    assert seq % tm == 0 and seq % steps == 0 and seq // MOBA_BLOCK <= 32

    tri = jnp.tril(jnp.ones((tm, tm), F32)).astype(BF16)
    route, bconst = _fox_routing()
    moba_t, mla_t = _rope_tables(seq)
    fgain = final_g.astype(F32).reshape(1, D_MODEL)

    for l in range(depth):
        w = _arrange_w_in(w_in[l])
        fb = jnp.zeros((1, LANES), F32).at[0, :N_HEADS].set(fox_fb[l].astype(F32))
        wuq, wukv = _arrange_mla(mla_w_uq[l].astype(F32), mla_w_ukv[l].astype(F32))
        (foxq, foxk, foxv, fg, su, sg, mobq, mobk, mobv, mg, mlaq, mlak, mlav, lg) = _in_call(
            x, norm_g[l].astype(F32).reshape(1, D_MODEL), w, fb, tri, route, bconst, moba_t, mla_t,
            mla_q_norm[l].astype(F32).reshape(1, MLA_Q_RANK), mla_kv_norm[l].astype(F32).reshape(1, MLA_KV_RANK),
            wuq, wukv, tm)

        a_out = _attn_call(foxq, foxk, foxv, fg, tq, tk, "fox_attn")
        c_out = _attn_call(mobq, mobk, mobv, mg, tq, tk, "moba_attn")
        d_out = _attn_call(mlaq, mlak, mlav, lg, tq, tk, "mla_attn")

        bm, lam2, cm = _s5_matrices(s5_a_re[l], s5_a_im[l], s5_log_dt[l], s5_b_re[l], s5_b_im[l],
                                    s5_c_re[l], s5_c_im[l])
        b_rows = _s5_call(su.reshape(seq * nb, WIDTH), bm, lam2, cm,
                          s5_d[l].astype(F32).reshape(1, WIDTH), s5_glu_w[l].astype(BF16),
                          s5_glu_b[l].astype(F32).reshape(1, WIDTH), steps, nb)
        x = _out_call(x, a_out, b_rows.reshape(seq, nb * WIDTH), sg, c_out, d_out,
                      w_out[l].astype(BF16), fgain, tm, final=(l == depth - 1))
    return x
```

```python
import functools

import jax
import jax.numpy as jnp
from jax import lax
from jax.experimental import pallas as pl
from jax.experimental.pallas import tpu as pltpu

F32 = jnp.float32
BF16 = jnp.bfloat16

D_MODEL = 1024
HEAD_DIM = 64
N_HEADS = 4
WIDTH = 256
S5_GROUPS = 16
S5_GROUP = 16
S5_STATE = 64
S5_NSTATE = S5_GROUPS * S5_STATE
MOBA_BLOCK = 256
MOBA_TOPK = 3
MLA_NOPE = 64
MLA_ROPE = 32
MLA_V = 64
MLA_Q_RANK = 384
MLA_KV_RANK = 128
ROPE_THETA = 10000.0
EPS = 1e-6
NEG = -1e30
LANES = 128
VMEM_LIMIT = 56 * 1024 * 1024

C_FQ, C_FK, C_FG = 0, 256, 512
C_FF = 768
C_SU, C_SG = 896, 1152
C_MQ, C_MK, C_MG = 1408, 1664, 1920
C_CQ = 2176
C_CKV = 2560
C_KR = 2688
C_LG = 2816
N_COLS = 3072


def _split3(x):
    x1 = x.astype(BF16)
    r1 = x - x1.astype(F32)
    x2 = r1.astype(BF16)
    r2 = r1 - x2.astype(F32)
    return x1, x2, r2.astype(BF16)


def _rope(x, cos, sin_lo, sin_hi, half):
    return (x * cos + pltpu.roll(x, LANES - half, 1) * sin_lo + pltpu.roll(x, half, 1) * sin_hi)


def _rms(x, g):
    return x * lax.rsqrt(jnp.mean(x * x, axis=-1, keepdims=True) + EPS) * g


def _silu(g):
    return g * (1.0 / (1.0 + jnp.exp(-g)))


def _in_kernel(x_ref, g_ref, w_ref, fb_ref, tri_ref, route_ref, bconst_ref,
               mc_ref, msl_ref, msh_ref, lc_ref, lsl_ref, lsh_ref,
               gq_ref, gkv_ref, wuq_ref, wukv_ref, wvt_ref, wuvt_ref,
               foxq_ref, foxk_ref, foxv_ref, fg_ref, su_ref, sg_ref,
               mobq_ref, mobk_ref, mobv_ref, mg_ref,
               mlaq_ref, mlak_ref, mlav_ref, lg_ref,
               carry_ref, kme_ref, kmo_ref, *, tm):
    sblk = pl.program_id(1)
    nblk = tm // MOBA_BLOCK

    @pl.when(sblk == 0)
    def _():
        carry_ref[...] = jnp.zeros_like(carry_ref)
        kme_ref[...] = jnp.zeros_like(kme_ref)
        kmo_ref[...] = jnp.zeros_like(kmo_ref)

    h = _rms(x_ref[0], g_ref[...]).astype(BF16)

    def proj(c0, width):
        return jnp.dot(h, w_ref[:, c0:c0 + width], preferred_element_type=F32)

    lane = lax.broadcasted_iota(jnp.int32, (tm, LANES), 1)
    lanef = lane.astype(F32)
    row = lax.broadcasted_iota(jnp.int32, (tm, LANES), 0)
    low_half = lane < HEAD_DIM

    def proj_t(wt):
        return lax.dot_general(wt, h, (((1,), (1,)), ((), ())), preferred_element_type=F32)

    foxv_ref[0, 0] = proj_t(wvt_ref[0:WIDTH, :]).astype(BF16)
    mobv_ref[0, 0] = proj_t(wvt_ref[WIDTH:2 * WIDTH, :]).astype(BF16)
    fg_ref[0] = proj(C_FG, WIDTH)
    sg_ref[0] = proj(C_SG, WIDTH)
    mg_ref[0] = proj(C_MG, WIDTH)
    lg_ref[0] = proj(C_LG, WIDTH)
    su_ref[...] = proj(C_SU, WIDTH)

    ff = proj(C_FF, LANES) + fb_ref[...]
    logf = -(jnp.maximum(-ff, 0.0) + jnp.log1p(jnp.exp(-jnp.abs(ff))))
    l1, l2, l3 = _split3(logf)
    tri = tri_ref[...]
    within = (jnp.dot(tri, l1, preferred_element_type=F32)
              + jnp.dot(tri, l2, preferred_element_type=F32)
              + jnp.dot(tri, l3, preferred_element_type=F32))
    cum = within + carry_ref[...]
    carry_ref[...] = cum[tm - 1:tm, :]
    c1, c2, c3 = _split3(cum)
    routed = jnp.dot(jnp.concatenate([c1, c2, c3], axis=1), route_ref[...],
                     preferred_element_type=F32) + bconst_ref[...]
    fq = proj(C_FQ, WIDTH) * (HEAD_DIM ** -0.5)
    fk = proj(C_FK, WIDTH)
    for hd in range(N_HEADS):
        pair = hd // 2
        own = low_half if hd % 2 == 0 else jnp.logical_not(low_half)
        sl = slice(pair * LANES, (pair + 1) * LANES)
        qa = jnp.where(own, fq[:, sl], routed[:, hd * LANES:(hd + 1) * LANES])
        ka = jnp.where(own, fk[:, sl], routed[:, (N_HEADS + hd) * LANES:(N_HEADS + hd + 1) * LANES])
        foxq_ref[0, :, hd * LANES:(hd + 1) * LANES] = qa.astype(BF16)
        foxk_ref[0, :, hd * LANES:(hd + 1) * LANES] = ka.astype(BF16)

    mc, msl, msh = mc_ref[...], msl_ref[...], msh_ref[...]
    mq = proj(C_MQ, WIDTH)
    mk = proj(C_MK, WIDTH)
    q_r = [_rope(mq[:, p * LANES:(p + 1) * LANES], mc, msl, msh, HEAD_DIM // 2) for p in range(2)]
    k_r = [_rope(mk[:, p * LANES:(p + 1) * LANES], mc, msl, msh, HEAD_DIM // 2) for p in range(2)]
    km_row = lax.broadcasted_iota(jnp.int32, (LANES, WIDTH), 0)
    kme, kmo = kme_ref[...], kmo_ref[...]
    for nb in range(nblk):
        blk = sblk * nblk + nb
        km = jnp.concatenate(
            [jnp.mean(k_r[p][nb * MOBA_BLOCK:(nb + 1) * MOBA_BLOCK, :], axis=0, keepdims=True) for p in range(2)],
            axis=1)
        kme = jnp.where(km_row == HEAD_DIM + blk, km, kme)
        kmo = jnp.where(km_row == blk, km, kmo)
    kme_ref[...] = kme
    kmo_ref[...] = kmo
    blk_row = (sblk * nblk + lax.shift_right_logical(row, 8)).astype(F32)
    for hd in range(N_HEADS):
        pair = hd // 2
        even = hd % 2 == 0
        own = low_half if even else jnp.logical_not(low_half)
        km_ref = kme_ref if even else kmo_ref
        qh = jnp.where(own, q_r[pair], 0.0)
        gate = lax.dot_general(qh, km_ref[:, pair * LANES:(pair + 1) * LANES],
                               (((1,), (1,)), ((), ())), precision=lax.Precision.HIGHEST,
                               preferred_element_type=F32)
        rel = lanef - (float(HEAD_DIM) if even else 0.0)
        g = jnp.where(rel >= 0.0, jnp.where(rel < blk_row, gate, -jnp.inf), -jnp.inf)
        chosen = jnp.zeros((tm, LANES), F32)
        for _ in range(MOBA_TOPK):
            m = jnp.max(g, axis=1, keepdims=True)
            first = jnp.min(jnp.where(g == m, lanef, 1e9), axis=1, keepdims=True)
            first = jnp.where(m > -jnp.inf, first, -1.0)
            pick = lanef == first
            chosen = jnp.where(pick, 1.0, chosen)
            g = jnp.where(pick, -jnp.inf, g)
        is_own_blk = rel == blk_row
        keep = jnp.where(is_own_blk, 1.0, chosen)
        sel_bias = jnp.where(keep > 0.0, 0.0, NEG)
        onehot = jnp.where(is_own_blk, 1.0, 0.0)
        qa = jnp.where(own, q_r[pair] * (HEAD_DIM ** -0.5), sel_bias)
        ka = jnp.where(own, k_r[pair], onehot)
        mobq_ref[0, :, hd * LANES:(hd + 1) * LANES] = qa.astype(BF16)
        mobk_ref[0, :, hd * LANES:(hd + 1) * LANES] = ka.astype(BF16)

    lc, lsl, lsh = lc_ref[...], lsl_ref[...], lsh_ref[...]
    cqn = _rms(proj(C_CQ, MLA_Q_RANK), gq_ref[...]).astype(BF16)
    qf = jnp.dot(cqn, wuq_ref[...], preferred_element_type=F32)
    ckvn = _rms(proj(C_CKV, MLA_KV_RANK), gkv_ref[...]).astype(BF16)
    kv = jnp.dot(ckvn, wukv_ref[...], preferred_element_type=F32)
    kr = _rope(proj(C_KR, LANES), lc, lsl, lsh, MLA_ROPE // 2)
    scale = (MLA_NOPE + MLA_ROPE) ** -0.5
    for hd in range(N_HEADS):
        sl = slice(hd * LANES, (hd + 1) * LANES)
        mlaq_ref[0, :, sl] = (_rope(qf[:, sl], lc, lsl, lsh, MLA_ROPE // 2) * scale).astype(BF16)
        mlak_ref[0, :, sl] = (kv[:, sl] + kr).astype(BF16)
    mlav_ref[0, 0] = lax.dot_general(wuvt_ref[...], ckvn, (((1,), (1,)), ((), ())),
                                     preferred_element_type=F32).astype(BF16)


def _attn_kernel(q_ref, k_ref, vt_ref, g_ref, o_ref, sa_ref, sb_ref, m_ref, acc_ref, *, tq, tk):
    assert tq == tk
    i = pl.program_id(2)
    kpos = lax.broadcasted_iota(jnp.int32, (tk, tq), 0)
    qpos = lax.broadcasted_iota(jnp.int32, (tk, tq), 1)
    vrow = lax.broadcasted_iota(jnp.int32, (LANES, tk), 0)
    qs = [q_ref[0, :, hh * LANES:(hh + 1) * LANES] for hh in range(2)]

    def scores(j, s_ref):
        off = pl.multiple_of(j * tk, tk)
        for hh in range(2):
            s_ref[hh] = lax.dot_general(k_ref[0, pl.ds(off, tk), hh * LANES:(hh + 1) * LANES], qs[hh],
                                        (((1,), (1,)), ((), ())), preferred_element_type=F32)

    def update(j, s_ref, masked):
        vt = vt_ref[0, j]
        for hh in range(2):
            st = s_ref[hh]
            if masked:
                st = jnp.where(kpos <= qpos, st, NEG)
            m = m_ref[hh]
            m_new = jnp.maximum(m, jnp.max(st, axis=0, keepdims=True))
            alpha = jnp.exp(m - m_new)
            p = jnp.exp(st - m_new).astype(BF16)
            own = (vrow < HEAD_DIM) if hh == 0 else (vrow >= HEAD_DIM)
            va = jnp.where(own, vt, jnp.ones_like(vt))
            acc_ref[hh] = alpha * acc_ref[hh] + jnp.dot(va, p, preferred_element_type=F32)
            m_ref[hh] = m_new

    m_ref[...] = jnp.full(m_ref.shape, -jnp.inf, F32)
    acc_ref[...] = jnp.zeros(acc_ref.shape, F32)
    scores(0, sa_ref)

    def pair(t, carry):
        scores(2 * t + 1, sb_ref)
        update(2 * t, sa_ref, masked=False)
        scores(2 * t + 2, sa_ref)
        update(2 * t + 1, sb_ref, masked=False)
        return carry

    lax.fori_loop(0, i // 2, pair, 0)

    @pl.when(i % 2 == 0)
    def _():
        update(i, sa_ref, masked=True)

    @pl.when(i % 2 == 1)
    def _():
        scores(i, sb_ref)
        update(i - 1, sa_ref, masked=False)
        update(i, sb_ref, masked=True)

    out_t = jnp.concatenate([acc_ref[0, 0:HEAD_DIM] / acc_ref[0, HEAD_DIM:HEAD_DIM + 1],
                             acc_ref[1, HEAD_DIM:] / acc_ref[1, 0:1]], axis=0)
    o_ref[0] = (out_t.T * _silu(g_ref[0])).astype(BF16)


def _s5_kernel(u_ref, bm_ref, lam_ref, cm_ref, d_ref, gw_ref, gb_ref, o_ref, x_ref, st_ref, *, steps, nbatch):
    @pl.when(pl.program_id(0) == 0)
    def _():
        st_ref[...] = jnp.zeros_like(st_ref)

    u = u_ref[...]
    x_ref[...] = jnp.dot(u.astype(BF16), bm_ref[...], preferred_element_type=F32)
    lam_re = jnp.broadcast_to(lam_ref[0:1, :], (nbatch, S5_NSTATE))
    lam_im = jnp.broadcast_to(lam_ref[1:2, :], (nbatch, S5_NSTATE))

    def body(t, carry):
        xr, xi = carry
        off = pl.multiple_of(t * nbatch, nbatch)
        nr = lam_re * xr - lam_im * xi + x_ref[pl.ds(off, nbatch), 0:S5_NSTATE]
        ni = lam_re * xi + lam_im * xr + x_ref[pl.ds(off, nbatch), S5_NSTATE:2 * S5_NSTATE]
        x_ref[pl.ds(off, nbatch), 0:S5_NSTATE] = nr
        x_ref[pl.ds(off, nbatch), S5_NSTATE:2 * S5_NSTATE] = ni
        return nr, ni

    xr, xi = lax.fori_loop(0, steps, body, (st_ref[:, 0:S5_NSTATE], st_ref[:, S5_NSTATE:2 * S5_NSTATE]))
    st_ref[:, 0:S5_NSTATE] = xr
    st_ref[:, S5_NSTATE:2 * S5_NSTATE] = xi

    y = jnp.dot(x_ref[...].astype(BF16), cm_ref[...], preferred_element_type=F32) + d_ref[...] * u
    y = 0.5 * y * (1.0 + jnp.tanh(0.7978845608028654 * (y + 0.044715 * (y * y * y))))
    z = jnp.dot(y.astype(BF16), gw_ref[...], preferred_element_type=F32) + gb_ref[...]
    o_ref[...] = y * (1.0 / (1.0 + jnp.exp(-z)))


def _out_kernel(x_ref, a_ref, b_ref, sg_ref, c_ref, d_ref, w_ref, fg_ref, o_ref, *, final):
    b = (b_ref[...] * _silu(sg_ref[0])).astype(BF16)
    y = (jnp.dot(a_ref[0], w_ref[0:WIDTH, :], preferred_element_type=F32)
         + jnp.dot(b, w_ref[WIDTH:2 * WIDTH, :], preferred_element_type=F32)
         + jnp.dot(c_ref[0], w_ref[2 * WIDTH:3 * WIDTH, :], preferred_element_type=F32)
         + jnp.dot(d_ref[0], w_ref[3 * WIDTH:4 * WIDTH, :], preferred_element_type=F32))
    xn = x_ref[0] + y
    if final:
        xn = _rms(xn, fg_ref[...])
    o_ref[0] = xn


def _tiles(seq):
    tm = 512 if seq % 512 == 0 else MOBA_BLOCK
    tk = tm
    tq = tk
    steps = 128
    return tm, tq, tk, steps


def _arrange_w_in(w):
    z = lambda n: jnp.zeros((D_MODEL, n), w.dtype)
    o = 0
    parts = {}
    for name, n in (("fq", 256), ("fk", 256), ("fv", 256), ("fg", 256), ("ff", 4), ("su", 256), ("sg", 256),
                    ("mq", 256), ("mk", 256), ("mv", 256), ("mg", 256), ("cq", 384), ("ckv", 128), ("kr", 32),
                    ("lg", 256)):
        parts[name] = w[:, o:o + n]
        o += n
    cols = [parts["fq"], parts["fk"], parts["fg"], parts["ff"], z(LANES - 4), parts["su"], parts["sg"],
            parts["mq"], parts["mk"], parts["mg"], parts["cq"], parts["ckv"],
            z(MLA_NOPE), parts["kr"], z(LANES - MLA_NOPE - MLA_ROPE), parts["lg"]]
    w_vt = jnp.concatenate([parts["fv"], parts["mv"]], axis=1).T
    return jnp.concatenate(cols, axis=1).astype(BF16), w_vt.astype(BF16)


def _fox_routing():
    import numpy as np
    route = np.zeros((3 * LANES, 2 * N_HEADS * LANES), np.float32)
    const = np.zeros((1, 2 * N_HEADS * LANES), np.float32)
    for hd in range(N_HEADS):
        base = HEAD_DIM if hd % 2 == 0 else 0
        qs, ks = hd * LANES, (N_HEADS + hd) * LANES
        for part in range(3):
            route[part * LANES + hd, qs + base + part] = 1.0
            const[0, qs + base + 3 + part] = 1.0
            const[0, ks + base + part] = 1.0
            route[part * LANES + hd, ks + base + 3 + part] = -1.0
    return jnp.asarray(route, BF16), jnp.asarray(const, F32)


def _rope_tables(seq):
    pos = jnp.arange(seq).astype(F32)[:, None]
    lane = jnp.arange(LANES)
    half = HEAD_DIM // 2
    inv = jnp.power(ROPE_THETA, -jnp.arange(half, dtype=F32) / half)
    ang = pos * inv[None, :]
    cos, sin = jnp.cos(ang)[:, lane % half], jnp.sin(ang)[:, lane % half]
    lo = (lane % HEAD_DIM) < half
    moba = (cos, jnp.where(lo, -sin, 0.0), jnp.where(lo, 0.0, sin))
    half = MLA_ROPE // 2
    inv = jnp.power(ROPE_THETA, -jnp.arange(half, dtype=F32) / half)
    ang = pos * inv[None, :]
    cos, sin = jnp.cos(ang)[:, lane % half], jnp.sin(ang)[:, lane % half]
    in_lo = (lane >= MLA_NOPE) & (lane < MLA_NOPE + half)
    in_hi = (lane >= MLA_NOPE + half) & (lane < MLA_NOPE + MLA_ROPE)
    mla = (jnp.where(in_lo | in_hi, cos, 1.0), jnp.where(in_lo, -sin, 0.0), jnp.where(in_hi, sin, 0.0))
    return moba, mla


def _arrange_mla(w_uq, w_ukv):
    z = lambda r, n: jnp.zeros((r, n), F32)
    dq = MLA_NOPE + MLA_ROPE
    q_cols, k_cols, v_cols = [], [], []
    for hd in range(N_HEADS):
        q_cols += [w_uq[:, hd * dq:(hd + 1) * dq], z(MLA_Q_RANK, LANES - dq)]
        base = hd * (MLA_NOPE + MLA_V)
        k_cols += [w_ukv[:, base:base + MLA_NOPE], z(MLA_KV_RANK, LANES - MLA_NOPE)]
        v_cols += [w_ukv[:, base + MLA_NOPE:base + MLA_NOPE + MLA_V]]
    return (jnp.concatenate(q_cols, axis=1).astype(BF16), jnp.concatenate(k_cols, axis=1).astype(BF16),
            jnp.concatenate(v_cols, axis=1).T.astype(BF16))


def _s5_matrices(a_re, a_im, log_dt, b_re, b_im, c_re, c_im):
    lam = lax.complex(a_re.astype(F32), a_im.astype(F32))
    dt = jnp.exp(log_dt.astype(F32))[:, None]
    lam_bar = jnp.exp(lam * dt)
    b_bar = ((lam_bar - 1.0) / lam)[..., None] * lax.complex(b_re.astype(F32), b_im.astype(F32))
    eye = jnp.eye(S5_GROUPS, dtype=F32)
    blockdiag_in = lambda t: jnp.einsum('gpc,gh->gchp', t, eye).reshape(WIDTH, S5_NSTATE)
    blockdiag_out = lambda t: jnp.einsum('gcp,gh->gphc', t, eye).reshape(S5_NSTATE, WIDTH)
    bm = jnp.concatenate([blockdiag_in(b_bar.real), blockdiag_in(b_bar.imag)], axis=1).astype(BF16)
    cm = jnp.concatenate([blockdiag_out(c_re.astype(F32)), -blockdiag_out(c_im.astype(F32))], axis=0).astype(BF16)
    lam2 = jnp.stack([lam_bar.real.reshape(S5_NSTATE), lam_bar.imag.reshape(S5_NSTATE)], axis=0)
    return bm, lam2, cm


def _full(shape):
    return pl.BlockSpec(shape, lambda *_: (0,) * len(shape))


def _in_call(x, g, w, fb, tri, route, bconst, moba_t, mla_t, gq, gkv, wuq, wukv, wvt, wuvt, tm):
    nb, seq, _ = x.shape
    tok = lambda width: pl.BlockSpec((1, tm, width), lambda b, s: (b, s, 0))
    tab = pl.BlockSpec((tm, LANES), lambda b, s: (s, 0))
    bf = lambda width: jax.ShapeDtypeStruct((nb, seq, width), BF16)
    f32 = lambda width: jax.ShapeDtypeStruct((nb, seq, width), F32)
    val_t = jax.ShapeDtypeStruct((nb, seq // tm, WIDTH, tm), BF16)
    val_t_spec = pl.BlockSpec((1, 1, WIDTH, tm), lambda b, s: (b, s, 0, 0))
    out_shape = (bf(512), bf(512), val_t, f32(256),
                 jax.ShapeDtypeStruct((seq, nb * WIDTH), F32), f32(256),
                 bf(512), bf(512), val_t, f32(256),
                 bf(512), bf(512), val_t, f32(256))
    out_specs = (tok(512), tok(512), val_t_spec, tok(256),
                 pl.BlockSpec((tm, WIDTH), lambda b, s: (s, b)), tok(256),
                 tok(512), tok(512), val_t_spec, tok(256),
                 tok(512), tok(512), val_t_spec, tok(256))
    in_specs = [tok(D_MODEL), _full((1, D_MODEL)), _full((D_MODEL, N_COLS)), _full((1, LANES)),
                _full((tm, tm)), _full((3 * LANES, 2 * N_HEADS * LANES)), _full((1, 2 * N_HEADS * LANES)),
                tab, tab, tab, tab, tab, tab,
                _full((1, MLA_Q_RANK)), _full((1, MLA_KV_RANK)),
                _full((MLA_Q_RANK, N_HEADS * LANES)), _full((MLA_KV_RANK, N_HEADS * LANES)),
                _full((2 * WIDTH, D_MODEL)), _full((WIDTH, MLA_KV_RANK))]
    return pl.pallas_call(
        functools.partial(_in_kernel, tm=tm),
        grid=(nb, seq // tm),
        in_specs=in_specs, out_specs=out_specs, out_shape=out_shape,
        scratch_shapes=[pltpu.VMEM((1, LANES), F32), pltpu.VMEM((LANES, WIDTH), F32),
                        pltpu.VMEM((LANES, WIDTH), F32)],
        compiler_params=pltpu.CompilerParams(dimension_semantics=("arbitrary", "arbitrary"),
                                             vmem_limit_bytes=VMEM_LIMIT),
        name="in_proj",
    )(x, g, w, fb, tri, route, bconst, *moba_t, *mla_t, gq, gkv, wuq, wukv, wvt, wuvt)


def _attn_call(q, k, vt, gate, tq, tk, name):
    nb, seq, _ = q.shape
    return pl.pallas_call(
        functools.partial(_attn_kernel, tq=tq, tk=tk),
        grid=(nb, 2, seq // tq),
        in_specs=[pl.BlockSpec((1, tq, 2 * LANES), lambda b, p, i: (b, i, p)),
                  pl.BlockSpec((1, seq, 2 * LANES), lambda b, p, i: (b, 0, p)),
                  pl.BlockSpec((1, seq // tk, LANES, tk), lambda b, p, i: (b, 0, p, 0)),
                  pl.BlockSpec((1, tq, LANES), lambda b, p, i: (b, i, p))],
        out_specs=pl.BlockSpec((1, tq, LANES), lambda b, p, i: (b, i, p)),
        out_shape=jax.ShapeDtypeStruct((nb, seq, WIDTH), BF16),
        scratch_shapes=[pltpu.VMEM((2, tk, tq), F32), pltpu.VMEM((2, tk, tq), F32),
                        pltpu.VMEM((2, 1, tq), F32), pltpu.VMEM((2, LANES, tq), F32)],
        compiler_params=pltpu.CompilerParams(dimension_semantics=("arbitrary", "arbitrary", "arbitrary"),
                                             vmem_limit_bytes=VMEM_LIMIT),
        name=name,
    )(q, k, vt, gate)


def _s5_call(u, bm, lam2, cm, d, gw, gb, steps, nbatch):
    rows = u.shape[0]
    blk = steps * nbatch
    return pl.pallas_call(
        functools.partial(_s5_kernel, steps=steps, nbatch=nbatch),
        grid=(rows // blk,),
        in_specs=[pl.BlockSpec((blk, WIDTH), lambda c: (c, 0)),
                  _full((WIDTH, 2 * S5_NSTATE)), _full((2, S5_NSTATE)), _full((2 * S5_NSTATE, WIDTH)),
                  _full((1, WIDTH)), _full((WIDTH, WIDTH)), _full((1, WIDTH))],
        out_specs=pl.BlockSpec((blk, WIDTH), lambda c: (c, 0)),
        out_shape=jax.ShapeDtypeStruct((rows, WIDTH), F32),
        scratch_shapes=[pltpu.VMEM((blk, 2 * S5_NSTATE), F32), pltpu.VMEM((nbatch, 2 * S5_NSTATE), F32)],
        compiler_params=pltpu.CompilerParams(dimension_semantics=("arbitrary",),
                                             vmem_limit_bytes=VMEM_LIMIT),
        name="s5",
    )(u, bm, lam2, cm, d, gw, gb)


def _out_call(x, a, b2d, sg, c, d, w, fg, tm, final):
    nb, seq, _ = x.shape
    tok = lambda width: pl.BlockSpec((1, tm, width), lambda b, s: (b, s, 0))
    return pl.pallas_call(
        functools.partial(_out_kernel, final=final),
        grid=(nb, seq // tm),
        in_specs=[tok(D_MODEL), tok(WIDTH), pl.BlockSpec((tm, WIDTH), lambda b, s: (s, b)), tok(WIDTH),
                  tok(WIDTH), tok(WIDTH), _full((4 * WIDTH, D_MODEL)), _full((1, D_MODEL))],
        out_specs=tok(D_MODEL),
        out_shape=jax.ShapeDtypeStruct(x.shape, F32),
        compiler_params=pltpu.CompilerParams(dimension_semantics=("arbitrary", "arbitrary"),
                                             vmem_limit_bytes=VMEM_LIMIT),
        name="out_proj",
    )(x, a, b2d, sg, c, d, w, fg)


def kernel(x, norm_g, w_in, fox_fb, s5_a_re, s5_a_im, s5_log_dt, s5_b_re, s5_b_im, s5_c_re, s5_c_im, s5_d,
           s5_glu_w, s5_glu_b, mla_q_norm, mla_w_uq, mla_kv_norm, mla_w_ukv, w_out, final_g):
    nb, seq, _ = x.shape
    depth = norm_g.shape[0]
    assert nb == 8, "the S5 recurrence keeps the batch on the 8 sublanes of a vreg"
    tm, tq, tk, steps = _tiles(seq)
    assert seq % tm == 0 and seq % steps == 0 and seq // MOBA_BLOCK <= 32

    tri = jnp.tril(jnp.ones((tm, tm), F32)).astype(BF16)
    route, bconst = _fox_routing()
    moba_t, mla_t = _rope_tables(seq)
    fgain = final_g.astype(F32).reshape(1, D_MODEL)

    for l in range(depth):
        w, wvt = _arrange_w_in(w_in[l])
        fb = jnp.zeros((1, LANES), F32).at[0, :N_HEADS].set(fox_fb[l].astype(F32))
        wuq, wukv, wuvt = _arrange_mla(mla_w_uq[l].astype(F32), mla_w_ukv[l].astype(F32))
        (foxq, foxk, foxv, fg, su, sg, mobq, mobk, mobv, mg, mlaq, mlak, mlav, lg) = _in_call(
            x, norm_g[l].astype(F32).reshape(1, D_MODEL), w, fb, tri, route, bconst, moba_t, mla_t,
            mla_q_norm[l].astype(F32).reshape(1, MLA_Q_RANK), mla_kv_norm[l].astype(F32).reshape(1, MLA_KV_RANK),
            wuq, wukv, wvt, wuvt, tm)

        a_out = _attn_call(foxq, foxk, foxv, fg, tq, tk, "fox_attn")
        c_out = _attn_call(mobq, mobk, mobv, mg, tq, tk, "moba_attn")
        d_out = _attn_call(mlaq, mlak, mlav, lg, tq, tk, "mla_attn")

        bm, lam2, cm = _s5_matrices(s5_a_re[l], s5_a_im[l], s5_log_dt[l], s5_b_re[l], s5_b_im[l],
                                    s5_c_re[l], s5_c_im[l])
        b_rows = _s5_call(su.reshape(seq * nb, WIDTH), bm, lam2, cm,
                          s5_d[l].astype(F32).reshape(1, WIDTH), s5_glu_w[l].astype(BF16),
                          s5_glu_b[l].astype(F32).reshape(1, WIDTH), steps, nb)
        x = _out_call(x, a_out, b_rows.reshape(seq, nb * WIDTH), sg, c_out, d_out,
                      w_out[l].astype(BF16), fgain, tm, final=(l == depth - 1))
    return x
```

```python
import functools

import jax
import jax.numpy as jnp
from jax import lax
from jax.experimental import pallas as pl
from jax.experimental.pallas import tpu as pltpu

F32 = jnp.float32
BF16 = jnp.bfloat16

D_MODEL = 1024
HEAD_DIM = 64
DEN_ROWS = 16
N_HEADS = 4
WIDTH = 256
S5_GROUPS = 16
S5_GROUP = 16
S5_STATE = 64
S5_NSTATE = S5_GROUPS * S5_STATE
MOBA_BLOCK = 256
MOBA_TOPK = 3
MAX_BLOCKS = 32
GROUP_HEAD = (1, 3, 0, 2)
LOG2E = 1.4426950408889634
MLA_NOPE = 64
MLA_ROPE = 32
MLA_V = 64
MLA_Q_RANK = 384
MLA_KV_RANK = 128
ROPE_THETA = 10000.0
EPS = 1e-6
NEG = -1e30
LANES = 128
VMEM_LIMIT = 56 * 1024 * 1024

C_FQ, C_FK, C_FG = 0, 256, 512
C_SU, C_SG = 768, 1024
C_MQ, C_MK, C_MG = 1280, 1536, 1792
C_LG = 2048
C_CQ = 2304
C_CKV = 2688
C_FFKR = 2816
N_COLS = 2944
PROJ_GROUPS = ((0, 768), (768, 512), (1280, 768), (2048, 896))
FOX_BIAS_LANES = 6


def _split3(x):
    x1 = x.astype(BF16)
    r1 = x - x1.astype(F32)
    x2 = r1.astype(BF16)
    r2 = r1 - x2.astype(F32)
    return x1, x2, r2.astype(BF16)


def _rope(x, cos, sin_lo, sin_hi, half):
    return (x * cos + pltpu.roll(x, LANES - half, 1) * sin_lo + pltpu.roll(x, half, 1) * sin_hi)


def _rms(x, g):
    return x * lax.rsqrt(jnp.mean(x * x, axis=-1, keepdims=True) + EPS) * g


def _silu(g):
    return g * (1.0 / (1.0 + jnp.exp(-g)))


def _in_kernel(x_ref, g_ref, w_ref, fb_ref, tri_ref, route_ref, bconst_ref,
               mc_ref, msl_ref, msh_ref, lc_ref, lsl_ref, lsh_ref,
               gq_ref, gkv_ref, wuq_ref, wukv_ref, wvt_ref, wuvt_ref,
               foxq_ref, foxk_ref, foxv_ref, fg_ref, su_ref, sg_ref,
               mobq_ref, mobk_ref, mobv_ref, mg_ref,
               mlaq_ref, mlak_ref, mlav_ref, lg_ref,
               carry_ref, km_ref, *, tm):
    sblk = pl.program_id(1)
    nblk = tm // MOBA_BLOCK

    @pl.when(sblk == 0)
    def _():
        carry_ref[...] = jnp.zeros_like(carry_ref)
        km_ref[...] = jnp.zeros_like(km_ref)

    h = _rms(x_ref[0], g_ref[...]).astype(BF16)

    z = [jnp.dot(h, w_ref[:, c0:c0 + width], preferred_element_type=F32) for c0, width in PROJ_GROUPS]

    def proj(c0, width):
        for zg, (g0, gw) in zip(z, PROJ_GROUPS):
            if g0 <= c0 and c0 + width <= g0 + gw:
                return zg[:, c0 - g0:c0 - g0 + width]
        raise ValueError("column range crosses a projection group")

    lane =lax.broadcasted_iota(jnp.int32, (tm, LANES), 1)
    row =lax.broadcasted_iota(jnp.int32, (tm, LANES), 0)
    low_half = lane < HEAD_DIM

    def proj_t(wt):
        return lax.dot_general(wt, h, (((1,), (1,)), ((), ())), preferred_element_type=F32)

    foxv_ref[0, 0] = proj_t(wvt_ref[0:WIDTH, :]).astype(BF16)
    mobv_ref[0, 0] = proj_t(wvt_ref[WIDTH:2 * WIDTH, :]).astype(BF16)
    fg_ref[0] = proj(C_FG, WIDTH)
    sg_ref[0] = proj(C_SG, WIDTH)
    mg_ref[0] = proj(C_MG, WIDTH)
    lg_ref[0] = proj(C_LG, WIDTH)
    su_ref[...] = proj(C_SU, WIDTH)

    ffkr = proj(C_FFKR, LANES)
    ff = ffkr + fb_ref[...]
    logf = -(jnp.maximum(-ff, 0.0) + jnp.log1p(jnp.exp(-jnp.abs(ff))))
    within3 = jnp.dot(tri_ref[...], jnp.concatenate(_split3(logf), axis=1), preferred_element_type=F32)
    within = within3[:, 0:LANES] + within3[:, LANES:2 * LANES] + within3[:, 2 * LANES:]
    cum = within + carry_ref[...]
    carry_ref[...] = cum[tm - 1:tm, :]
    routed = jnp.dot(jnp.concatenate(_split3(cum * LOG2E), axis=1), route_ref[...],
                     preferred_element_type=F32) + bconst_ref[...]
    fq = proj(C_FQ, WIDTH) * (HEAD_DIM ** -0.5 * LOG2E)
    fk = proj(C_FK, WIDTH)
    for hd in range(N_HEADS):
        pair = hd // 2
        own = low_half if hd % 2 == 0 else jnp.logical_not(low_half)
        base = _fox_bias_base(hd)
        mine = (lane >= base) & (lane < base + FOX_BIAS_LANES)
        sl = slice(pair * LANES, (pair + 1) * LANES)
        qa = jnp.where(own, fq[:, sl], jnp.where(mine, routed[:, 0:LANES], 0.0))
        ka = jnp.where(own, fk[:, sl], jnp.where(mine, routed[:, LANES:], 0.0))
        foxq_ref[0, :, hd * LANES:(hd + 1) * LANES] = qa.astype(BF16)
        foxk_ref[0, :, hd * LANES:(hd + 1) * LANES] = ka.astype(BF16)

    mc, msl, msh = mc_ref[...], msl_ref[...], msh_ref[...]
    mq = proj(C_MQ, WIDTH)
    mk = proj(C_MK, WIDTH)
    q_r = [_rope(mq[:, p * LANES:(p + 1) * LANES], mc, msl, msh, HEAD_DIM // 2) for p in range(2)]
    k_r = [_rope(mk[:, p * LANES:(p + 1) * LANES], mc, msl, msh, HEAD_DIM // 2) for p in range(2)]
    km_row = lax.broadcasted_iota(jnp.int32, (LANES, WIDTH), 0)
    km_lane = lax.broadcasted_iota(jnp.int32, (LANES, WIDTH), 1)
    km_grp = lax.shift_right_logical(km_row, 5)
    km_head = jnp.where(km_grp == 0, GROUP_HEAD[0], jnp.where(km_grp == 1, GROUP_HEAD[1],
                        jnp.where(km_grp == 2, GROUP_HEAD[2], GROUP_HEAD[3])))
    km_own = lax.shift_right_logical(km_lane, 6) == km_head
    kmt = km_ref[...]
    for nb in range(nblk):
        blk = sblk * nblk + nb
        km = jnp.concatenate(
            [jnp.mean(k_r[p][nb * MOBA_BLOCK:(nb + 1) * MOBA_BLOCK, :], axis=0, keepdims=True) for p in range(2)],
            axis=1)
        kmt = jnp.where((km_row & (MAX_BLOCKS - 1)) == blk, jnp.where(km_own, km, 0.0), kmt)
    km_ref[...] = kmt
    gate_t = lax.dot_general(kmt, jnp.concatenate(q_r, axis=1), (((1,), (1,)), ((), ())),
                             precision=lax.Precision.HIGHEST, preferred_element_type=F32)
    cand = lax.broadcasted_iota(jnp.int32, (MAX_BLOCKS, tm), 0).astype(F32)
    tok = lax.broadcasted_iota(jnp.int32, (MAX_BLOCKS, tm), 1)
    blk_tok = (sblk * nblk + lax.shift_right_logical(tok, 8)).astype(F32)
    bias_rows = []
    for grp in range(N_HEADS):
        g = jnp.where(cand < blk_tok, gate_t[grp * MAX_BLOCKS:(grp + 1) * MAX_BLOCKS, :], -jnp.inf)
        chosen = jnp.zeros((MAX_BLOCKS, tm), F32)
        for _ in range(MOBA_TOPK):
            m = jnp.max(g, axis=0, keepdims=True)
            first = jnp.min(jnp.where(g == m, cand, 1e9), axis=0, keepdims=True)
            first = jnp.where(m > -jnp.inf, first, -1.0)
            pick = cand == first
            chosen = jnp.where(pick, 1.0, chosen)
            g = jnp.where(pick, -jnp.inf, g)
        keep = jnp.where(cand == blk_tok, 1.0, chosen)
        bias_rows.append(jnp.where(keep > 0.0, 0.0, NEG))
    sel_bias = jnp.concatenate(bias_rows, axis=0).T
    blk_row = sblk * nblk + lax.shift_right_logical(row, 8)
    onehot = jnp.where((lane & (MAX_BLOCKS - 1)) == blk_row, 1.0, 0.0)
    lane_grp = lax.shift_right_logical(lane, 5)
    for hd in range(N_HEADS):
        pair = hd // 2
        own = low_half if hd % 2 == 0 else jnp.logical_not(low_half)
        mine = lane_grp == GROUP_HEAD.index(hd)
        qa = jnp.where(own, q_r[pair] * (HEAD_DIM ** -0.5 * LOG2E), jnp.where(mine, sel_bias, 0.0))
        ka = jnp.where(own, k_r[pair], jnp.where(mine, onehot, 0.0))
        mobq_ref[0, :, hd * LANES:(hd + 1) * LANES] = qa.astype(BF16)
        mobk_ref[0, :, hd * LANES:(hd + 1) * LANES] = ka.astype(BF16)

    lc, lsl, lsh = lc_ref[...], lsl_ref[...], lsh_ref[...]
    cqn = _rms(proj(C_CQ, MLA_Q_RANK), gq_ref[...]).astype(BF16)
    qf = jnp.dot(cqn, wuq_ref[...], preferred_element_type=F32)
    ckvn = _rms(proj(C_CKV, MLA_KV_RANK), gkv_ref[...]).astype(BF16)
    kv = jnp.dot(ckvn, wukv_ref[...], preferred_element_type=F32)
    kr = jnp.where(low_half, 0.0, _rope(ffkr, lc, lsl, lsh, MLA_ROPE // 2))
    scale = (MLA_NOPE + MLA_ROPE) ** -0.5 * LOG2E
    for hd in range(N_HEADS):
        sl = slice(hd * LANES, (hd + 1) * LANES)
        mlaq_ref[0, :, sl] = (_rope(qf[:, sl], lc, lsl, lsh, MLA_ROPE // 2) * scale).astype(BF16)
        mlak_ref[0, :, sl] = (kv[:, sl] + kr).astype(BF16)
    mlav_ref[0, 0] = lax.dot_general(wuvt_ref[...], ckvn, (((1,), (1,)), ((), ())),
                                     preferred_element_type=F32).astype(BF16)


def _attn_kernel(q_ref, k_ref, vt_ref, g_ref, o_ref, sa_ref, sb_ref, mxa_ref, mxb_ref, m_ref, acc_ref, *, tq, tk):
    assert tq == tk
    i = pl.program_id(2)
    kpos = lax.broadcasted_iota(jnp.int32, (tk, tq), 0)
    qpos = lax.broadcasted_iota(jnp.int32, (tk, tq), 1)
    ones = jnp.ones((DEN_ROWS, tk), BF16)
    qs =[q_ref[0, :, hh * LANES:(hh + 1) * LANES] for hh in range(2)]

    def scores(hh, j, s_ref, mx_ref):
        off = pl.multiple_of(j * tk, tk)
        st = lax.dot_general(k_ref[0, pl.ds(off, tk), hh * LANES:(hh + 1) * LANES], qs[hh],
                             (((1,), (1,)), ((), ())), preferred_element_type=F32)
        s_ref[hh] = st
        mx_ref[hh] = jnp.max(st, axis=0, keepdims=True)

    def update(hh, j, s_ref, mx_ref, masked):
        st = s_ref[hh]
        if masked:
            st = jnp.where(kpos <= qpos, st, NEG)
            mx = jnp.max(st, axis=0, keepdims=True)
        else:
            mx = mx_ref[hh]
        m = m_ref[hh]
        m_new = jnp.maximum(m, mx)
        alpha = jnp.exp2(m - m_new)
        p = jnp.exp2(st - m_new).astype(BF16)
        va = jnp.concatenate([vt_ref[0, j, hh * HEAD_DIM:(hh + 1) * HEAD_DIM, :], ones], axis=0)
        acc_ref[hh] = alpha * acc_ref[hh] + jnp.dot(va, p, preferred_element_type=F32)
        m_ref[hh] = m_new

    def stage(nxt, cur):
        for hh in range(2):
            if nxt is not None:
                scores(hh, *nxt)
            if cur is not None:
                update(hh, *cur)

    m_ref[...] = jnp.full(m_ref.shape, -jnp.inf, F32)
    acc_ref[...] = jnp.zeros(acc_ref.shape, F32)
    stage((0, sa_ref, mxa_ref), None)

    def pair(t, carry):
        stage((2 * t + 1, sb_ref, mxb_ref), (2 * t, sa_ref, mxa_ref, False))
        stage((2 * t + 2, sa_ref, mxa_ref), (2 * t + 1, sb_ref, mxb_ref, False))
        return carry

    lax.fori_loop(0, i // 2, pair, 0)

    @pl.when(i % 2 == 0)
    def _():
        stage(None, (i, sa_ref, mxa_ref, True))

    @pl.when(i % 2 == 1)
    def _():
        stage((i, sb_ref, mxb_ref), (i - 1, sa_ref, mxa_ref, False))
        stage(None, (i, sb_ref, mxb_ref, True))

    out_t = jnp.concatenate([acc_ref[hh, 0:HEAD_DIM] / acc_ref[hh, HEAD_DIM:HEAD_DIM + 1] for hh in range(2)],
                            axis=0)
    o_ref[0] = (out_t.T * _silu(g_ref[0])).astype(BF16)


def _s5_kernel(u_ref, bm_ref, lam_ref, cm_ref, d_ref, gw_ref, gb_ref, o_ref, x_ref, st_ref, *, steps, nbatch):
    @pl.when(pl.program_id(0) == 0)
    def _():
        st_ref[...] = jnp.zeros_like(st_ref)

    u = u_ref[...]
    x_ref[...] = jnp.dot(u.astype(BF16), bm_ref[...], preferred_element_type=F32)
    lam_re = jnp.broadcast_to(lam_ref[0:1, :], (nbatch, S5_NSTATE))
    lam_im = jnp.broadcast_to(lam_ref[1:2, :], (nbatch, S5_NSTATE))

    def body(t, carry):
        xr, xi = carry
        off = pl.multiple_of(t * nbatch, nbatch)
        nr = lam_re * xr - lam_im * xi + x_ref[pl.ds(off, nbatch), 0:S5_NSTATE]
        ni = lam_re * xi + lam_im * xr + x_ref[pl.ds(off, nbatch), S5_NSTATE:2 * S5_NSTATE]
        x_ref[pl.ds(off, nbatch), 0:S5_NSTATE] = nr
        x_ref[pl.ds(off, nbatch), S5_NSTATE:2 * S5_NSTATE] = ni
        return nr, ni

    xr, xi = lax.fori_loop(0, steps, body, (st_ref[:, 0:S5_NSTATE], st_ref[:, S5_NSTATE:2 * S5_NSTATE]))
    st_ref[:, 0:S5_NSTATE] = xr
    st_ref[:, S5_NSTATE:2 * S5_NSTATE] = xi

    y = jnp.dot(x_ref[...].astype(BF16), cm_ref[...], preferred_element_type=F32) + d_ref[...] * u
    y = 0.5 * y * (1.0 + jnp.tanh(0.7978845608028654 * (y + 0.044715 * (y * y * y))))
    z = jnp.dot(y.astype(BF16), gw_ref[...], preferred_element_type=F32) + gb_ref[...]
    o_ref[...] = y * (1.0 / (1.0 + jnp.exp(-z)))


def _out_kernel(x_ref, a_ref, b_ref, sg_ref, c_ref, d_ref, w_ref, fg_ref, o_ref, *, final):
    b = (b_ref[...] * _silu(sg_ref[0])).astype(BF16)
    y = (jnp.dot(a_ref[0], w_ref[0:WIDTH, :], preferred_element_type=F32)
         + jnp.dot(b, w_ref[WIDTH:2 * WIDTH, :], preferred_element_type=F32)
         + jnp.dot(c_ref[0], w_ref[2 * WIDTH:3 * WIDTH, :], preferred_element_type=F32)
         + jnp.dot(d_ref[0], w_ref[3 * WIDTH:4 * WIDTH, :], preferred_element_type=F32))
    xn = x_ref[0] + y
    if final:
        xn = _rms(xn, fg_ref[...])
    o_ref[0] = xn


def _tiles(seq):
    tm = 512 if seq % 512 == 0 else MOBA_BLOCK
    tk = tm
    tq = tk
    steps = 128
    return tm, tq, tk, steps


def _arrange_w_in(w):
    z = lambda n: jnp.zeros((D_MODEL, n), w.dtype)
    o = 0
    parts = {}
    for name, n in (("fq", 256), ("fk", 256), ("fv", 256), ("fg", 256), ("ff", 4), ("su", 256), ("sg", 256),
                    ("mq", 256), ("mk", 256), ("mv", 256), ("mg", 256), ("cq", 384), ("ckv", 128), ("kr", 32),
                    ("lg", 256)):
        parts[name] = w[:, o:o + n]
        o += n
    cols = [parts["fq"], parts["fk"], parts["fg"], parts["su"], parts["sg"],
            parts["mq"], parts["mk"], parts["mg"], parts["lg"], parts["cq"], parts["ckv"],
            parts["ff"], z(MLA_NOPE - 4), parts["kr"], z(LANES - MLA_NOPE - MLA_ROPE)]
    w_vt = jnp.concatenate([parts["fv"], parts["mv"]], axis=1).T
    return jnp.concatenate(cols, axis=1).astype(BF16), w_vt.astype(BF16)


def _fox_bias_base(hd):
    return (HEAD_DIM if hd % 2 == 0 else 0) + FOX_BIAS_LANES * (hd // 2)


def _fox_routing():
    import numpy as np
    route = np.zeros((3 * LANES, 2 * LANES), np.float32)
    const = np.zeros((1, 2 * LANES), np.float32)
    for hd in range(N_HEADS):
        base = _fox_bias_base(hd)
        for part in range(3):
            route[part * LANES + hd, base + part] = 1.0
            const[0, base + 3 + part] = 1.0
            const[0, LANES + base + part] = 1.0
            route[part * LANES + hd, LANES + base + 3 + part] = -1.0
    return jnp.asarray(route, BF16), jnp.asarray(const, F32)


def _rope_tables(seq):
    pos = jnp.arange(seq).astype(F32)[:, None]
    lane = jnp.arange(LANES)
    half = HEAD_DIM // 2
    inv = jnp.power(ROPE_THETA, -jnp.arange(half, dtype=F32) / half)
    ang = pos * inv[None, :]
    cos, sin = jnp.cos(ang)[:, lane % half], jnp.sin(ang)[:, lane % half]
    lo = (lane % HEAD_DIM) < half
    moba = (cos, jnp.where(lo, -sin, 0.0), jnp.where(lo, 0.0, sin))
    half = MLA_ROPE // 2
    inv = jnp.power(ROPE_THETA, -jnp.arange(half, dtype=F32) / half)
    ang = pos * inv[None, :]
    cos, sin = jnp.cos(ang)[:, lane % half], jnp.sin(ang)[:, lane % half]
    in_lo = (lane >= MLA_NOPE) & (lane < MLA_NOPE + half)
    in_hi = (lane >= MLA_NOPE + half) & (lane < MLA_NOPE + MLA_ROPE)
    mla = (jnp.where(in_lo | in_hi, cos, 1.0), jnp.where(in_lo, -sin, 0.0), jnp.where(in_hi, sin, 0.0))
    return moba, mla


def _arrange_mla(w_uq, w_ukv):
    z = lambda r, n: jnp.zeros((r, n), F32)
    dq = MLA_NOPE + MLA_ROPE
    q_cols, k_cols, v_cols = [], [], []
    for hd in range(N_HEADS):
        q_cols += [w_uq[:, hd * dq:(hd + 1) * dq], z(MLA_Q_RANK, LANES - dq)]
        base = hd * (MLA_NOPE + MLA_V)
        k_cols += [w_ukv[:, base:base + MLA_NOPE], z(MLA_KV_RANK, LANES - MLA_NOPE)]
        v_cols += [w_ukv[:, base + MLA_NOPE:base + MLA_NOPE + MLA_V]]
    return (jnp.concatenate(q_cols, axis=1).astype(BF16), jnp.concatenate(k_cols, axis=1).astype(BF16),
            jnp.concatenate(v_cols, axis=1).T.astype(BF16))


def _s5_matrices(a_re, a_im, log_dt, b_re, b_im, c_re, c_im):
    lam = lax.complex(a_re.astype(F32), a_im.astype(F32))
    dt = jnp.exp(log_dt.astype(F32))[:, None]
    lam_bar = jnp.exp(lam * dt)
    b_bar = ((lam_bar - 1.0) / lam)[..., None] * lax.complex(b_re.astype(F32), b_im.astype(F32))
    eye = jnp.eye(S5_GROUPS, dtype=F32)
    blockdiag_in = lambda t: jnp.einsum('gpc,gh->gchp', t, eye).reshape(WIDTH, S5_NSTATE)
    blockdiag_out = lambda t: jnp.einsum('gcp,gh->gphc', t, eye).reshape(S5_NSTATE, WIDTH)
    bm = jnp.concatenate([blockdiag_in(b_bar.real), blockdiag_in(b_bar.imag)], axis=1).astype(BF16)
    cm = jnp.concatenate([blockdiag_out(c_re.astype(F32)), -blockdiag_out(c_im.astype(F32))], axis=0).astype(BF16)
    lam2 = jnp.stack([lam_bar.real.reshape(S5_NSTATE), lam_bar.imag.reshape(S5_NSTATE)], axis=0)
    return bm, lam2, cm


def _full(shape):
    return pl.BlockSpec(shape, lambda *_: (0,) * len(shape))


def _in_call(x, g, w, fb, tri, route, bconst, moba_t, mla_t, gq, gkv, wuq, wukv, wvt, wuvt, tm):
    nb, seq, _ = x.shape
    tok = lambda width: pl.BlockSpec((1, tm, width), lambda b, s: (b, s, 0))
    tab = pl.BlockSpec((tm, LANES), lambda b, s: (s, 0))
    bf = lambda width: jax.ShapeDtypeStruct((nb, seq, width), BF16)
    f32 = lambda width: jax.ShapeDtypeStruct((nb, seq, width), F32)
    val_t = jax.ShapeDtypeStruct((nb, seq // tm, WIDTH, tm), BF16)
    val_t_spec = pl.BlockSpec((1, 1, WIDTH, tm), lambda b, s: (b, s, 0, 0))
    out_shape = (bf(512), bf(512), val_t, f32(256),
                 jax.ShapeDtypeStruct((seq, nb * WIDTH), F32), f32(256),
                 bf(512), bf(512), val_t, f32(256),
                 bf(512), bf(512), val_t, f32(256))
    out_specs = (tok(512), tok(512), val_t_spec, tok(256),
                 pl.BlockSpec((tm, WIDTH), lambda b, s: (s, b)), tok(256),
                 tok(512), tok(512), val_t_spec, tok(256),
                 tok(512), tok(512), val_t_spec, tok(256))
    in_specs = [tok(D_MODEL), _full((1, D_MODEL)), _full((D_MODEL, N_COLS)), _full((1, LANES)),
                _full((tm, tm)), _full((3 * LANES, 2 * LANES)), _full((1, 2 * LANES)),
                tab, tab, tab, tab, tab, tab,
                _full((1, MLA_Q_RANK)), _full((1, MLA_KV_RANK)),
                _full((MLA_Q_RANK, N_HEADS * LANES)), _full((MLA_KV_RANK, N_HEADS * LANES)),
                _full((2 * WIDTH, D_MODEL)), _full((WIDTH, MLA_KV_RANK))]
    return pl.pallas_call(
        functools.partial(_in_kernel, tm=tm),
        grid=(nb, seq // tm),
        in_specs=in_specs, out_specs=out_specs, out_shape=out_shape,
        scratch_shapes=[pltpu.VMEM((1, LANES), F32), pltpu.VMEM((LANES, WIDTH), F32)],
        compiler_params=pltpu.CompilerParams(dimension_semantics=("arbitrary", "arbitrary"),
                                             vmem_limit_bytes=VMEM_LIMIT),
        name="in_proj",
    )(x, g, w, fb, tri, route, bconst, *moba_t, *mla_t, gq, gkv, wuq, wukv, wvt, wuvt)


def _attn_call(q, k, vt, gate, tq, tk, name):
    nb, seq, _ = q.shape
    return pl.pallas_call(
        functools.partial(_attn_kernel, tq=tq, tk=tk),
        grid=(nb, 2, seq // tq),
        in_specs=[pl.BlockSpec((1, tq, 2 * LANES), lambda b, p, i: (b, i, p)),
                  pl.BlockSpec((1, seq, 2 * LANES), lambda b, p, i: (b, 0, p)),
                  pl.BlockSpec((1, seq // tk, LANES, tk), lambda b, p, i: (b, 0, p, 0)),
                  pl.BlockSpec((1, tq, LANES), lambda b, p, i: (b, i, p))],
        out_specs=pl.BlockSpec((1, tq, LANES), lambda b, p, i: (b, i, p)),
        out_shape=jax.ShapeDtypeStruct((nb, seq, WIDTH), BF16),
        scratch_shapes=[pltpu.VMEM((2, tk, tq), F32), pltpu.VMEM((2, tk, tq), F32),
                        pltpu.VMEM((2, 1, tq), F32), pltpu.VMEM((2, 1, tq), F32),
                        pltpu.VMEM((2, 1, tq), F32), pltpu.VMEM((2, HEAD_DIM + DEN_ROWS, tq), F32)],
        compiler_params=pltpu.CompilerParams(dimension_semantics=("arbitrary", "arbitrary", "arbitrary"),
                                             vmem_limit_bytes=VMEM_LIMIT),
        name=name,
    )(q, k, vt, gate)


def _s5_call(u, bm, lam2, cm, d, gw, gb, steps, nbatch):
    rows = u.shape[0]
    blk = steps * nbatch
    return pl.pallas_call(
        functools.partial(_s5_kernel, steps=steps, nbatch=nbatch),
        grid=(rows // blk,),
        in_specs=[pl.BlockSpec((blk, WIDTH), lambda c: (c, 0)),
                  _full((WIDTH, 2 * S5_NSTATE)), _full((2, S5_NSTATE)), _full((2 * S5_NSTATE, WIDTH)),
                  _full((1, WIDTH)), _full((WIDTH, WIDTH)), _full((1, WIDTH))],
        out_specs=pl.BlockSpec((blk, WIDTH), lambda c: (c, 0)),
        out_shape=jax.ShapeDtypeStruct((rows, WIDTH), F32),
        scratch_shapes=[pltpu.VMEM((blk, 2 * S5_NSTATE), F32), pltpu.VMEM((nbatch, 2 * S5_NSTATE), F32)],
        compiler_params=pltpu.CompilerParams(dimension_semantics=("arbitrary",),
                                             vmem_limit_bytes=VMEM_LIMIT),
        name="s5",
    )(u, bm, lam2, cm, d, gw, gb)


def _out_call(x, a, b2d, sg, c, d, w, fg, tm, final):
    nb, seq, _ = x.shape
    tok = lambda width: pl.BlockSpec((1, tm, width), lambda b, s: (b, s, 0))
    return pl.pallas_call(
        functools.partial(_out_kernel, final=final),
        grid=(nb, seq // tm),
        in_specs=[tok(D_MODEL), tok(WIDTH), pl.BlockSpec((tm, WIDTH), lambda b, s: (s, b)), tok(WIDTH),
                  tok(WIDTH), tok(WIDTH), _full((4 * WIDTH, D_MODEL)), _full((1, D_MODEL))],
        out_specs=tok(D_MODEL),
        out_shape=jax.ShapeDtypeStruct(x.shape, F32),
        compiler_params=pltpu.CompilerParams(dimension_semantics=("arbitrary", "arbitrary"),
                                             vmem_limit_bytes=VMEM_LIMIT),
        name="out_proj",
    )(x, a, b2d, sg, c, d, w, fg)


def kernel(x, norm_g, w_in, fox_fb, s5_a_re, s5_a_im, s5_log_dt, s5_b_re, s5_b_im, s5_c_re, s5_c_im, s5_d,
           s5_glu_w, s5_glu_b, mla_q_norm, mla_w_uq, mla_kv_norm, mla_w_ukv, w_out, final_g):
    nb, seq, _ = x.shape
    depth = norm_g.shape[0]
    assert nb == 8, "the S5 recurrence keeps the batch on the 8 sublanes of a vreg"
    tm, tq, tk, steps = _tiles(seq)
    assert seq % tm == 0 and seq % steps == 0 and seq // MOBA_BLOCK <= 32

    tri = jnp.tril(jnp.ones((tm, tm), F32)).astype(BF16)
    route, bconst = _fox_routing()
    moba_t, mla_t = _rope_tables(seq)
    fgain = final_g.astype(F32).reshape(1, D_MODEL)

    for l in range(depth):
        w, wvt = _arrange_w_in(w_in[l])
        fb = jnp.zeros((1, LANES), F32).at[0, :N_HEADS].set(fox_fb[l].astype(F32))
        wuq, wukv, wuvt = _arrange_mla(mla_w_uq[l].astype(F32), mla_w_ukv[l].astype(F32))
        (foxq, foxk, foxv, fg, su, sg, mobq, mobk, mobv, mg, mlaq, mlak, mlav, lg) = _in_call(
            x, norm_g[l].astype(F32).reshape(1, D_MODEL), w, fb, tri, route, bconst, moba_t, mla_t,
            mla_q_norm[l].astype(F32).reshape(1, MLA_Q_RANK), mla_kv_norm[l].astype(F32).reshape(1, MLA_KV_RANK),
            wuq, wukv, wvt, wuvt, tm)

        a_out = _attn_call(foxq, foxk, foxv, fg, tq, tk, "fox_attn")
        c_out = _attn_call(mobq, mobk, mobv, mg, tq, tk, "moba_attn")
        d_out = _attn_call(mlaq, mlak, mlav, lg, tq, tk, "mla_attn")

        bm, lam2, cm = _s5_matrices(s5_a_re[l], s5_a_im[l], s5_log_dt[l], s5_b_re[l], s5_b_im[l],
                                    s5_c_re[l], s5_c_im[l])
        b_rows = _s5_call(su.reshape(seq * nb, WIDTH), bm, lam2, cm,
                          s5_d[l].astype(F32).reshape(1, WIDTH), s5_glu_w[l].astype(BF16),
                          s5_glu_b[l].astype(F32).reshape(1, WIDTH), steps, nb)
        x = _out_call(x, a_out, b_rows.reshape(seq, nb * WIDTH), sg, c_out, d_out,
                      w_out[l].astype(BF16), fgain, tm, final=(l == depth - 1))
    return x
```

```python
import functools

import jax
import jax.numpy as jnp
from jax import lax
from jax.experimental import pallas as pl
from jax.experimental.pallas import tpu as pltpu

F32 = jnp.float32
BF16 = jnp.bfloat16

D_MODEL = 1024
HEAD_DIM = 64
DEN_ROWS = 16
PAST_UNROLL = 4
N_HEADS = 4
WIDTH = 256
S5_GROUPS = 16
S5_GROUP = 16
S5_STATE = 64
S5_NSTATE = S5_GROUPS * S5_STATE
MOBA_BLOCK = 256
MOBA_TOPK = 3
MAX_BLOCKS = 32
GROUP_HEAD = (1, 3, 0, 2)
LOG2E = 1.4426950408889634
MLA_NOPE = 64
MLA_ROPE = 32
MLA_V = 64
MLA_Q_RANK = 384
MLA_KV_RANK = 128
ROPE_THETA = 10000.0
EPS = 1e-6
NEG = -1e30
LANES = 128
VMEM_LIMIT = 56 * 1024 * 1024

C_FQ, C_FK, C_FG = 0, 256, 512
C_SU, C_SG = 768, 1024
C_MQ, C_MK, C_MG = 1280, 1536, 1792
C_LG = 2048
C_CQ = 2304
C_CKV = 2688
C_FFKR = 2816
N_COLS = 2944
PROJ_GROUPS = ((0, 768), (768, 512), (1280, 768), (2048, 896))
FOX_BIAS_LANES = 6


def _split3(x):
    x1 = x.astype(BF16)
    r1 = x - x1.astype(F32)
    x2 = r1.astype(BF16)
    r2 = r1 - x2.astype(F32)
    return x1, x2, r2.astype(BF16)


def _rope(x, cos, sin_lo, sin_hi, half):
    return (x * cos + pltpu.roll(x, LANES - half, 1) * sin_lo + pltpu.roll(x, half, 1) * sin_hi)


def _rms(x, g):
    return x * lax.rsqrt(jnp.mean(x * x, axis=-1, keepdims=True) + EPS) * g


def _silu(g):
    return g * (1.0 / (1.0 + jnp.exp(-g)))


def _in_kernel(x_ref, g_ref, w_ref, fb_ref, tri_ref, route_ref, bconst_ref,
               mc_ref, msl_ref, msh_ref, lc_ref, lsl_ref, lsh_ref,
               gq_ref, gkv_ref, wuq_ref, wukv_ref, wvt_ref, wuvt_ref,
               foxq_ref, foxk_ref, foxv_ref, fg_ref, su_ref, sg_ref,
               mobq_ref, mobk_ref, mobv_ref, mg_ref,
               mlaq_ref, mlak_ref, mlav_ref, lg_ref,
               carry_ref, km_ref, *, tm):
    sblk = pl.program_id(1)
    nblk = tm // MOBA_BLOCK

    @pl.when(sblk == 0)
    def _():
        carry_ref[...] = jnp.zeros_like(carry_ref)
        km_ref[...] = jnp.zeros_like(km_ref)

    h = _rms(x_ref[0], g_ref[...]).astype(BF16)

    z = [jnp.dot(h, w_ref[:, c0:c0 + width], preferred_element_type=F32) for c0, width in PROJ_GROUPS]

    def proj(c0, width):
        for zg, (g0, gw) in zip(z, PROJ_GROUPS):
            if g0 <= c0 and c0 + width <= g0 + gw:
                return zg[:, c0 - g0:c0 - g0 + width]
        raise ValueError("column range crosses a projection group")

    lane =lax.broadcasted_iota(jnp.int32, (tm, LANES), 1)
    row =lax.broadcasted_iota(jnp.int32, (tm, LANES), 0)
    low_half = lane < HEAD_DIM

    def proj_t(wt):
        return lax.dot_general(wt, h, (((1,), (1,)), ((), ())), preferred_element_type=F32)

    foxv_ref[0, 0] = proj_t(wvt_ref[0:WIDTH, :]).astype(BF16)
    mobv_ref[0, 0] = proj_t(wvt_ref[WIDTH:2 * WIDTH, :]).astype(BF16)
    fg_ref[0] = proj(C_FG, WIDTH)
    sg_ref[0] = proj(C_SG, WIDTH)
    mg_ref[0] = proj(C_MG, WIDTH)
    lg_ref[0] = proj(C_LG, WIDTH)
    su_ref[...] = proj(C_SU, WIDTH)

    ffkr = proj(C_FFKR, LANES)
    ff = ffkr + fb_ref[...]
    logf = -(jnp.maximum(-ff, 0.0) + jnp.log1p(jnp.exp(-jnp.abs(ff))))
    within3 = jnp.dot(tri_ref[...], jnp.concatenate(_split3(logf), axis=1), preferred_element_type=F32)
    within = within3[:, 0:LANES] + within3[:, LANES:2 * LANES] + within3[:, 2 * LANES:]
    cum = within + carry_ref[...]
    carry_ref[...] = cum[tm - 1:tm, :]
    routed = jnp.dot(jnp.concatenate(_split3(cum * LOG2E), axis=1), route_ref[...],
                     preferred_element_type=F32) + bconst_ref[...]
    fq = proj(C_FQ, WIDTH) * (HEAD_DIM ** -0.5 * LOG2E)
    fk = proj(C_FK, WIDTH)
    for hd in range(N_HEADS):
        pair = hd // 2
        own = low_half if hd % 2 == 0 else jnp.logical_not(low_half)
        base = _fox_bias_base(hd)
        mine = (lane >= base) & (lane < base + FOX_BIAS_LANES)
        sl = slice(pair * LANES, (pair + 1) * LANES)
        qa = jnp.where(own, fq[:, sl], jnp.where(mine, routed[:, 0:LANES], 0.0))
        ka = jnp.where(own, fk[:, sl], jnp.where(mine, routed[:, LANES:], 0.0))
        foxq_ref[0, :, hd * LANES:(hd + 1) * LANES] = qa.astype(BF16)
        foxk_ref[0, :, hd * LANES:(hd + 1) * LANES] = ka.astype(BF16)

    mc, msl, msh = mc_ref[...], msl_ref[...], msh_ref[...]
    mq = proj(C_MQ, WIDTH)
    mk = proj(C_MK, WIDTH)
    q_r = [_rope(mq[:, p * LANES:(p + 1) * LANES], mc, msl, msh, HEAD_DIM // 2) for p in range(2)]
    k_r = [_rope(mk[:, p * LANES:(p + 1) * LANES], mc, msl, msh, HEAD_DIM // 2) for p in range(2)]
    km_row = lax.broadcasted_iota(jnp.int32, (LANES, WIDTH), 0)
    km_lane = lax.broadcasted_iota(jnp.int32, (LANES, WIDTH), 1)
    km_grp = lax.shift_right_logical(km_row, 5)
    km_head = jnp.where(km_grp == 0, GROUP_HEAD[0], jnp.where(km_grp == 1, GROUP_HEAD[1],
                        jnp.where(km_grp == 2, GROUP_HEAD[2], GROUP_HEAD[3])))
    km_own = lax.shift_right_logical(km_lane, 6) == km_head
    kmt = km_ref[...]
    for nb in range(nblk):
        blk = sblk * nblk + nb
        km = jnp.concatenate(
            [jnp.mean(k_r[p][nb * MOBA_BLOCK:(nb + 1) * MOBA_BLOCK, :], axis=0, keepdims=True) for p in range(2)],
            axis=1)
        kmt = jnp.where((km_row & (MAX_BLOCKS - 1)) == blk, jnp.where(km_own, km, 0.0), kmt)
    km_ref[...] = kmt
    gate_t = lax.dot_general(kmt, jnp.concatenate(q_r, axis=1), (((1,), (1,)), ((), ())),
                             precision=lax.Precision.HIGHEST, preferred_element_type=F32)
    cand = lax.broadcasted_iota(jnp.int32, (MAX_BLOCKS, tm), 0).astype(F32)
    tok = lax.broadcasted_iota(jnp.int32, (MAX_BLOCKS, tm), 1)
    blk_tok = (sblk * nblk + lax.shift_right_logical(tok, 8)).astype(F32)
    bias_rows = []
    for grp in range(N_HEADS):
        g = jnp.where(cand < blk_tok, gate_t[grp * MAX_BLOCKS:(grp + 1) * MAX_BLOCKS, :], -jnp.inf)
        chosen = jnp.zeros((MAX_BLOCKS, tm), F32)
        for _ in range(MOBA_TOPK):
            m = jnp.max(g, axis=0, keepdims=True)
            first = jnp.min(jnp.where(g == m, cand, 1e9), axis=0, keepdims=True)
            first = jnp.where(m > -jnp.inf, first, -1.0)
            pick = cand == first
            chosen = jnp.where(pick, 1.0, chosen)
            g = jnp.where(pick, -jnp.inf, g)
        keep = jnp.where(cand == blk_tok, 1.0, chosen)
        bias_rows.append(jnp.where(keep > 0.0, 0.0, NEG))
    sel_bias = jnp.concatenate(bias_rows, axis=0).T
    blk_row = sblk * nblk + lax.shift_right_logical(row, 8)
    onehot = jnp.where((lane & (MAX_BLOCKS - 1)) == blk_row, 1.0, 0.0)
    lane_grp = lax.shift_right_logical(lane, 5)
    for hd in range(N_HEADS):
        pair = hd // 2
        own = low_half if hd % 2 == 0 else jnp.logical_not(low_half)
        mine = lane_grp == GROUP_HEAD.index(hd)
        qa = jnp.where(own, q_r[pair] * (HEAD_DIM ** -0.5 * LOG2E), jnp.where(mine, sel_bias, 0.0))
        ka = jnp.where(own, k_r[pair], jnp.where(mine, onehot, 0.0))
        mobq_ref[0, :, hd * LANES:(hd + 1) * LANES] = qa.astype(BF16)
        mobk_ref[0, :, hd * LANES:(hd + 1) * LANES] = ka.astype(BF16)

    lc, lsl, lsh = lc_ref[...], lsl_ref[...], lsh_ref[...]
    cqn = _rms(proj(C_CQ, MLA_Q_RANK), gq_ref[...]).astype(BF16)
    qf = jnp.dot(cqn, wuq_ref[...], preferred_element_type=F32)
    ckvn = _rms(proj(C_CKV, MLA_KV_RANK), gkv_ref[...]).astype(BF16)
    kv = jnp.dot(ckvn, wukv_ref[...], preferred_element_type=F32)
    kr = jnp.where(low_half, 0.0, _rope(ffkr, lc, lsl, lsh, MLA_ROPE // 2))
    scale = (MLA_NOPE + MLA_ROPE) ** -0.5 * LOG2E
    for hd in range(N_HEADS):
        sl = slice(hd * LANES, (hd + 1) * LANES)
        mlaq_ref[0, :, sl] = (_rope(qf[:, sl], lc, lsl, lsh, MLA_ROPE // 2) * scale).astype(BF16)
        mlak_ref[0, :, sl] = (kv[:, sl] + kr).astype(BF16)
    mlav_ref[0, 0] = lax.dot_general(wuvt_ref[...], ckvn, (((1,), (1,)), ((), ())),
                                     preferred_element_type=F32).astype(BF16)


def _attn_kernel(q_ref, k_ref, vt_ref, g_ref, o_ref, sa_ref, sb_ref, mxa_ref, mxb_ref, m_ref, acc_ref, *, t, nq):
    bufs = ((sa_ref, mxa_ref), (sb_ref, mxb_ref))
    kpos = lax.broadcasted_iota(jnp.int32, (t, t), 0)
    qpos = lax.broadcasted_iota(jnp.int32, (t, t), 1)
    ones = jnp.ones((DEN_ROWS, t), BF16)
    pair = lambda i, j: (jnp.int32(i), jnp.int32(j))

    def scores(hh, ij, buf):
        s_ref, mx_ref = buf
        qoff = pl.multiple_of(ij[0] * t, t)
        koff = pl.multiple_of(ij[1] * t, t)
        st = lax.dot_general(k_ref[0, pl.ds(koff, t), hh * LANES:(hh + 1) * LANES],
                             q_ref[0, pl.ds(qoff, t), hh * LANES:(hh + 1) * LANES],
                             (((1,), (1,)), ((), ())), preferred_element_type=F32)
        s_ref[hh] = st
        mx_ref[hh] = jnp.max(st, axis=0, keepdims=True)

    def values(hh, j):
        return jnp.concatenate([vt_ref[0, j, hh * HEAD_DIM:(hh + 1) * HEAD_DIM, :], ones], axis=0)

    def first_update(hh, ij, buf):
        i, j = ij
        st = jnp.where(kpos <= qpos, buf[0][hh], NEG)
        mx = jnp.max(st, axis=0, keepdims=True)
        p = jnp.exp2(st - mx).astype(BF16)
        acc_ref[i, hh] = jnp.dot(values(hh, j), p, preferred_element_type=F32)
        m_ref[i, hh] = mx

    def update(hh, ij, buf):
        i, j = ij
        m = m_ref[i, hh]
        m_new = jnp.maximum(m, buf[1][hh])
        alpha = jnp.exp2(m - m_new)
        p = jnp.exp2(buf[0][hh] - m_new).astype(BF16)
        acc_ref[i, hh] = alpha * acc_ref[i, hh] + jnp.dot(values(hh, j), p, preferred_element_type=F32)
        m_ref[i, hh] = m_new

    def stage(nxt, nxt_buf, cur, cur_buf, consume):
        for hh in range(2):
            if nxt is not None:
                scores(hh, nxt, nxt_buf)
            consume(hh, cur, cur_buf)

    n_past = nq * (nq - 1) // 2
    first_past = pair(min(1, nq - 1), 0)

    for hh in range(2):
        scores(hh, pair(0, 0), bufs[0])

    def diag_of(i):
        inside = i < nq
        return jnp.where(inside, i, first_past[0]), jnp.where(inside, i, first_past[1])

    def two_diagonals(n, carry):
        i = 2 * n
        stage(diag_of(i + 1), bufs[1], (i, i), bufs[0], first_update)
        stage(diag_of(i + 2), bufs[0], (i + 1, i + 1), bufs[1], first_update)
        return carry

    lax.fori_loop(0, nq // 2, two_diagonals, 0)
    if nq % 2 == 1:
        stage(first_past if n_past else None, bufs[1], pair(nq - 1, nq - 1), bufs[0], first_update)
    par = nq % 2

    def succ(ij):
        i, j = ij
        last = j == i - 1
        return jnp.where(last, jnp.minimum(i + 1, nq - 1), i), jnp.where(last, 0, j + 1)

    def past_steps(count, cur, skip_last_scores):
        for k in range(count):
            nxt = succ(cur)
            no_next = skip_last_scores and k == count - 1
            stage(None if no_next else nxt, bufs[(par + k + 1) % 2], cur, bufs[(par + k) % 2], update)
            cur = nxt
        return cur

    cur = lax.fori_loop(0, n_past // PAST_UNROLL, lambda _, c: past_steps(PAST_UNROLL, c, False), first_past)
    if n_past % PAST_UNROLL:
        past_steps(n_past % PAST_UNROLL, cur, True)

    def finish_tile(i, carry):
        out_t = jnp.concatenate([acc_ref[i, hh, 0:HEAD_DIM] / acc_ref[i, hh, HEAD_DIM:HEAD_DIM + 1]
                                 for hh in range(2)], axis=0)
        rows = pl.ds(pl.multiple_of(i * t, t), t)
        o_ref[0, rows, :] = (out_t.T * _silu(g_ref[0, rows, :])).astype(BF16)
        return carry

    lax.fori_loop(0, nq, finish_tile, 0)


def _s5_kernel(u_ref, bm_ref, lam_ref, cm_ref, d_ref, gw_ref, gb_ref, o_ref, x_ref, st_ref, *, steps, nbatch):
    @pl.when(pl.program_id(0) == 0)
    def _():
        st_ref[...] = jnp.zeros_like(st_ref)

    u = u_ref[...]
    x_ref[...] = jnp.dot(u.astype(BF16), bm_ref[...], preferred_element_type=F32)
    lam_re = jnp.broadcast_to(lam_ref[0:1, :], (nbatch, S5_NSTATE))
    lam_im = jnp.broadcast_to(lam_ref[1:2, :], (nbatch, S5_NSTATE))

    def body(t, carry):
        xr, xi = carry
        off = pl.multiple_of(t * nbatch, nbatch)
        nr = lam_re * xr - lam_im * xi + x_ref[pl.ds(off, nbatch), 0:S5_NSTATE]
        ni = lam_re * xi + lam_im * xr + x_ref[pl.ds(off, nbatch), S5_NSTATE:2 * S5_NSTATE]
        x_ref[pl.ds(off, nbatch), 0:S5_NSTATE] = nr
        x_ref[pl.ds(off, nbatch), S5_NSTATE:2 * S5_NSTATE] = ni
        return nr, ni

    xr, xi = lax.fori_loop(0, steps, body, (st_ref[:, 0:S5_NSTATE], st_ref[:, S5_NSTATE:2 * S5_NSTATE]))
    st_ref[:, 0:S5_NSTATE] = xr
    st_ref[:, S5_NSTATE:2 * S5_NSTATE] = xi

    y = jnp.dot(x_ref[...].astype(BF16), cm_ref[...], preferred_element_type=F32) + d_ref[...] * u
    y = 0.5 * y * (1.0 + jnp.tanh(0.7978845608028654 * (y + 0.044715 * (y * y * y))))
    z = jnp.dot(y.astype(BF16), gw_ref[...], preferred_element_type=F32) + gb_ref[...]
    o_ref[...] = y * (1.0 / (1.0 + jnp.exp(-z)))


def _out_kernel(x_ref, a_ref, b_ref, sg_ref, c_ref, d_ref, w_ref, fg_ref, o_ref, *, final):
    b = (b_ref[...] * _silu(sg_ref[0])).astype(BF16)
    y = (jnp.dot(a_ref[0], w_ref[0:WIDTH, :], preferred_element_type=F32)
         + jnp.dot(b, w_ref[WIDTH:2 * WIDTH, :], preferred_element_type=F32)
         + jnp.dot(c_ref[0], w_ref[2 * WIDTH:3 * WIDTH, :], preferred_element_type=F32)
         + jnp.dot(d_ref[0], w_ref[3 * WIDTH:4 * WIDTH, :], preferred_element_type=F32))
    xn = x_ref[0] + y
    if final:
        xn = _rms(xn, fg_ref[...])
    o_ref[0] = xn


def _tiles(seq):
    tm = 512 if seq % 512 == 0 else MOBA_BLOCK
    steps = 128
    return tm, steps


def _arrange_w_in(w):
    z = lambda n: jnp.zeros((D_MODEL, n), w.dtype)
    o = 0
    parts = {}
    for name, n in (("fq", 256), ("fk", 256), ("fv", 256), ("fg", 256), ("ff", 4), ("su", 256), ("sg", 256),
                    ("mq", 256), ("mk", 256), ("mv", 256), ("mg", 256), ("cq", 384), ("ckv", 128), ("kr", 32),
                    ("lg", 256)):
        parts[name] = w[:, o:o + n]
        o += n
    cols = [parts["fq"], parts["fk"], parts["fg"], parts["su"], parts["sg"],
            parts["mq"], parts["mk"], parts["mg"], parts["lg"], parts["cq"], parts["ckv"],
            parts["ff"], z(MLA_NOPE - 4), parts["kr"], z(LANES - MLA_NOPE - MLA_ROPE)]
    w_vt = jnp.concatenate([parts["fv"], parts["mv"]], axis=1).T
    return jnp.concatenate(cols, axis=1).astype(BF16), w_vt.astype(BF16)


def _fox_bias_base(hd):
    return (HEAD_DIM if hd % 2 == 0 else 0) + FOX_BIAS_LANES * (hd // 2)


def _fox_routing():
    import numpy as np
    route = np.zeros((3 * LANES, 2 * LANES), np.float32)
    const = np.zeros((1, 2 * LANES), np.float32)
    for hd in range(N_HEADS):
        base = _fox_bias_base(hd)
        for part in range(3):
            route[part * LANES + hd, base + part] = 1.0
            const[0, base + 3 + part] = 1.0
            const[0, LANES + base + part] = 1.0
            route[part * LANES + hd, LANES + base + 3 + part] = -1.0
    return jnp.asarray(route, BF16), jnp.asarray(const, F32)


def _rope_tables(seq):
    pos = jnp.arange(seq).astype(F32)[:, None]
    lane = jnp.arange(LANES)
    half = HEAD_DIM // 2
    inv = jnp.power(ROPE_THETA, -jnp.arange(half, dtype=F32) / half)
    ang = pos * inv[None, :]
    cos, sin = jnp.cos(ang)[:, lane % half], jnp.sin(ang)[:, lane % half]
    lo = (lane % HEAD_DIM) < half
    moba = (cos, jnp.where(lo, -sin, 0.0), jnp.where(lo, 0.0, sin))
    half = MLA_ROPE // 2
    inv = jnp.power(ROPE_THETA, -jnp.arange(half, dtype=F32) / half)
    ang = pos * inv[None, :]
    cos, sin = jnp.cos(ang)[:, lane % half], jnp.sin(ang)[:, lane % half]
    in_lo = (lane >= MLA_NOPE) & (lane < MLA_NOPE + half)
    in_hi = (lane >= MLA_NOPE + half) & (lane < MLA_NOPE + MLA_ROPE)
    mla = (jnp.where(in_lo | in_hi, cos, 1.0), jnp.where(in_lo, -sin, 0.0), jnp.where(in_hi, sin, 0.0))
    return moba, mla


def _arrange_mla(w_uq, w_ukv):
    z = lambda r, n: jnp.zeros((r, n), F32)
    dq = MLA_NOPE + MLA_ROPE
    q_cols, k_cols, v_cols = [], [], []
    for hd in range(N_HEADS):
        q_cols += [w_uq[:, hd * dq:(hd + 1) * dq], z(MLA_Q_RANK, LANES - dq)]
        base = hd * (MLA_NOPE + MLA_V)
        k_cols += [w_ukv[:, base:base + MLA_NOPE], z(MLA_KV_RANK, LANES - MLA_NOPE)]
        v_cols += [w_ukv[:, base + MLA_NOPE:base + MLA_NOPE + MLA_V]]
    return (jnp.concatenate(q_cols, axis=1).astype(BF16), jnp.concatenate(k_cols, axis=1).astype(BF16),
            jnp.concatenate(v_cols, axis=1).T.astype(BF16))


def _s5_matrices(a_re, a_im, log_dt, b_re, b_im, c_re, c_im):
    lam = lax.complex(a_re.astype(F32), a_im.astype(F32))
    dt = jnp.exp(log_dt.astype(F32))[:, None]
    lam_bar = jnp.exp(lam * dt)
    b_bar = ((lam_bar - 1.0) / lam)[..., None] * lax.complex(b_re.astype(F32), b_im.astype(F32))
    eye = jnp.eye(S5_GROUPS, dtype=F32)
    blockdiag_in = lambda t: jnp.einsum('gpc,gh->gchp', t, eye).reshape(WIDTH, S5_NSTATE)
    blockdiag_out = lambda t: jnp.einsum('gcp,gh->gphc', t, eye).reshape(S5_NSTATE, WIDTH)
    bm = jnp.concatenate([blockdiag_in(b_bar.real), blockdiag_in(b_bar.imag)], axis=1).astype(BF16)
    cm = jnp.concatenate([blockdiag_out(c_re.astype(F32)), -blockdiag_out(c_im.astype(F32))], axis=0).astype(BF16)
    lam2 = jnp.stack([lam_bar.real.reshape(S5_NSTATE), lam_bar.imag.reshape(S5_NSTATE)], axis=0)
    return bm, lam2, cm


def _full(shape):
    return pl.BlockSpec(shape, lambda *_: (0,) * len(shape))


def _in_call(x, g, w, fb, tri, route, bconst, moba_t, mla_t, gq, gkv, wuq, wukv, wvt, wuvt, tm):
    nb, seq, _ = x.shape
    tok = lambda width: pl.BlockSpec((1, tm, width), lambda b, s: (b, s, 0))
    tab = pl.BlockSpec((tm, LANES), lambda b, s: (s, 0))
    bf = lambda width: jax.ShapeDtypeStruct((nb, seq, width), BF16)
    f32 = lambda width: jax.ShapeDtypeStruct((nb, seq, width), F32)
    val_t = jax.ShapeDtypeStruct((nb, seq // tm, WIDTH, tm), BF16)
    val_t_spec = pl.BlockSpec((1, 1, WIDTH, tm), lambda b, s: (b, s, 0, 0))
    out_shape = (bf(512), bf(512), val_t, f32(256),
                 jax.ShapeDtypeStruct((seq, nb * WIDTH), F32), f32(256),
                 bf(512), bf(512), val_t, f32(256),
                 bf(512), bf(512), val_t, f32(256))
    out_specs = (tok(512), tok(512), val_t_spec, tok(256),
                 pl.BlockSpec((tm, WIDTH), lambda b, s: (s, b)), tok(256),
                 tok(512), tok(512), val_t_spec, tok(256),
                 tok(512), tok(512), val_t_spec, tok(256))
    in_specs = [tok(D_MODEL), _full((1, D_MODEL)), _full((D_MODEL, N_COLS)), _full((1, LANES)),
                _full((tm, tm)), _full((3 * LANES, 2 * LANES)), _full((1, 2 * LANES)),
                tab, tab, tab, tab, tab, tab,
                _full((1, MLA_Q_RANK)), _full((1, MLA_KV_RANK)),
                _full((MLA_Q_RANK, N_HEADS * LANES)), _full((MLA_KV_RANK, N_HEADS * LANES)),
                _full((2 * WIDTH, D_MODEL)), _full((WIDTH, MLA_KV_RANK))]
    return pl.pallas_call(
        functools.partial(_in_kernel, tm=tm),
        grid=(nb, seq // tm),
        in_specs=in_specs, out_specs=out_specs, out_shape=out_shape,
        scratch_shapes=[pltpu.VMEM((1, LANES), F32), pltpu.VMEM((LANES, WIDTH), F32)],
        compiler_params=pltpu.CompilerParams(dimension_semantics=("arbitrary", "arbitrary"),
                                             vmem_limit_bytes=VMEM_LIMIT),
        name="in_proj",
    )(x, g, w, fb, tri, route, bconst, *moba_t, *mla_t, gq, gkv, wuq, wukv, wvt, wuvt)


def _attn_call(q, k, vt, gate, t, name):
    nb, seq, _ = q.shape
    whole = lambda width: pl.BlockSpec((1, seq, width), lambda b, p: (b, 0, p))
    return pl.pallas_call(
        functools.partial(_attn_kernel, t=t, nq=seq // t),
        grid=(nb, 2),
        in_specs=[whole(2 * LANES), whole(2 * LANES),
                  pl.BlockSpec((1, seq // t, LANES, t), lambda b, p: (b, 0, p, 0)), whole(LANES)],
        out_specs=whole(LANES),
        out_shape=jax.ShapeDtypeStruct((nb, seq, WIDTH), BF16),
        scratch_shapes=[pltpu.VMEM((2, t, t), F32), pltpu.VMEM((2, t, t), F32),
                        pltpu.VMEM((2, 1, t), F32), pltpu.VMEM((2, 1, t), F32),
                        pltpu.VMEM((seq // t, 2, 1, t), F32),
                        pltpu.VMEM((seq // t, 2, HEAD_DIM + DEN_ROWS, t), F32)],
        compiler_params=pltpu.CompilerParams(dimension_semantics=("arbitrary", "arbitrary"),
                                             vmem_limit_bytes=VMEM_LIMIT),
        name=name,
    )(q, k, vt, gate)


def _s5_call(u, bm, lam2, cm, d, gw, gb, steps, nbatch):
    rows = u.shape[0]
    blk = steps * nbatch
    return pl.pallas_call(
        functools.partial(_s5_kernel, steps=steps, nbatch=nbatch),
        grid=(rows // blk,),
        in_specs=[pl.BlockSpec((blk, WIDTH), lambda c: (c, 0)),
                  _full((WIDTH, 2 * S5_NSTATE)), _full((2, S5_NSTATE)), _full((2 * S5_NSTATE, WIDTH)),
                  _full((1, WIDTH)), _full((WIDTH, WIDTH)), _full((1, WIDTH))],
        out_specs=pl.BlockSpec((blk, WIDTH), lambda c: (c, 0)),
        out_shape=jax.ShapeDtypeStruct((rows, WIDTH), F32),
        scratch_shapes=[pltpu.VMEM((blk, 2 * S5_NSTATE), F32), pltpu.VMEM((nbatch, 2 * S5_NSTATE), F32)],
        compiler_params=pltpu.CompilerParams(dimension_semantics=("arbitrary",),
                                             vmem_limit_bytes=VMEM_LIMIT),
        name="s5",
    )(u, bm, lam2, cm, d, gw, gb)


def _out_call(x, a, b2d, sg, c, d, w, fg, tm, final):
    nb, seq, _ = x.shape
    tok = lambda width: pl.BlockSpec((1, tm, width), lambda b, s: (b, s, 0))
    return pl.pallas_call(
        functools.partial(_out_kernel, final=final),
        grid=(nb, seq // tm),
        in_specs=[tok(D_MODEL), tok(WIDTH), pl.BlockSpec((tm, WIDTH), lambda b, s: (s, b)), tok(WIDTH),
                  tok(WIDTH), tok(WIDTH), _full((4 * WIDTH, D_MODEL)), _full((1, D_MODEL))],
        out_specs=tok(D_MODEL),
        out_shape=jax.ShapeDtypeStruct(x.shape, F32),
        compiler_params=pltpu.CompilerParams(dimension_semantics=("arbitrary", "arbitrary"),
                                             vmem_limit_bytes=VMEM_LIMIT),
        name="out_proj",
    )(x, a, b2d, sg, c, d, w, fg)


def kernel(x, norm_g, w_in, fox_fb, s5_a_re, s5_a_im, s5_log_dt, s5_b_re, s5_b_im, s5_c_re, s5_c_im, s5_d,
           s5_glu_w, s5_glu_b, mla_q_norm, mla_w_uq, mla_kv_norm, mla_w_ukv, w_out, final_g):
    nb, seq, _ = x.shape
    depth = norm_g.shape[0]
    assert nb == 8, "the S5 recurrence keeps the batch on the 8 sublanes of a vreg"
    tm, steps = _tiles(seq)
    assert seq % tm == 0 and seq % steps == 0 and seq // MOBA_BLOCK <= 32

    tri = jnp.tril(jnp.ones((tm, tm), F32)).astype(BF16)
    route, bconst = _fox_routing()
    moba_t, mla_t = _rope_tables(seq)
    fgain = final_g.astype(F32).reshape(1, D_MODEL)

    for l in range(depth):
        w, wvt = _arrange_w_in(w_in[l])
        fb = jnp.zeros((1, LANES), F32).at[0, :N_HEADS].set(fox_fb[l].astype(F32))
        wuq, wukv, wuvt = _arrange_mla(mla_w_uq[l].astype(F32), mla_w_ukv[l].astype(F32))
        (foxq, foxk, foxv, fg, su, sg, mobq, mobk, mobv, mg, mlaq, mlak, mlav, lg) = _in_call(
            x, norm_g[l].astype(F32).reshape(1, D_MODEL), w, fb, tri, route, bconst, moba_t, mla_t,
            mla_q_norm[l].astype(F32).reshape(1, MLA_Q_RANK), mla_kv_norm[l].astype(F32).reshape(1, MLA_KV_RANK),
            wuq, wukv, wvt, wuvt, tm)

        a_out = _attn_call(foxq, foxk, foxv, fg, tm, "fox_attn")
        c_out = _attn_call(mobq, mobk, mobv, mg, tm, "moba_attn")
        d_out = _attn_call(mlaq, mlak, mlav, lg, tm, "mla_attn")

        bm, lam2, cm = _s5_matrices(s5_a_re[l], s5_a_im[l], s5_log_dt[l], s5_b_re[l], s5_b_im[l],
                                    s5_c_re[l], s5_c_im[l])
        b_rows = _s5_call(su.reshape(seq * nb, WIDTH), bm, lam2, cm,
                          s5_d[l].astype(F32).reshape(1, WIDTH), s5_glu_w[l].astype(BF16),
                          s5_glu_b[l].astype(F32).reshape(1, WIDTH), steps, nb)
        x = _out_call(x, a_out, b_rows.reshape(seq, nb * WIDTH), sg, c_out, d_out,
                      w_out[l].astype(BF16), fgain, tm, final=(l == depth - 1))
    return x
```

```python
import functools

import jax
import jax.numpy as jnp
from jax import lax
from jax.experimental import pallas as pl
from jax.experimental.pallas import tpu as pltpu

F32 = jnp.float32
BF16 = jnp.bfloat16

D_MODEL = 1024
HEAD_DIM = 64
DEN_ROWS = 16
PAST_UNROLL = 8
N_HEADS = 4
WIDTH = 256
S5_GROUPS = 16
S5_GROUP = 16
S5_STATE = 64
S5_NSTATE = S5_GROUPS * S5_STATE
MOBA_BLOCK = 256
MOBA_TOPK = 3
MAX_BLOCKS = 32
GROUP_HEAD = (1, 3, 0, 2)
LOG2E = 1.4426950408889634
MLA_NOPE = 64
MLA_ROPE = 32
MLA_V = 64
MLA_Q_RANK = 384
MLA_KV_RANK = 128
ROPE_THETA = 10000.0
EPS = 1e-6
NEG = -1e30
LANES = 128
VMEM_LIMIT = 56 * 1024 * 1024

C_FQ, C_FK, C_FG = 0, 256, 512
C_SU, C_SG = 768, 1024
C_MQ, C_MK, C_MG = 1280, 1536, 1792
C_LG = 2048
C_CQ = 2304
C_CKV = 2688
C_FFKR = 2816
N_COLS = 2944
PROJ_GROUPS = ((0, 768), (768, 512), (1280, 768), (2048, 896))
FOX_BIAS_LANES = 6


def _split3(x):
    x1 = x.astype(BF16)
    r1 = x - x1.astype(F32)
    x2 = r1.astype(BF16)
    r2 = r1 - x2.astype(F32)
    return x1, x2, r2.astype(BF16)


def _rope(x, cos, sin_lo, sin_hi, half):
    return (x * cos + pltpu.roll(x, LANES - half, 1) * sin_lo + pltpu.roll(x, half, 1) * sin_hi)


def _rms(x, g):
    return x * lax.rsqrt(jnp.mean(x * x, axis=-1, keepdims=True) + EPS) * g


def _silu(g):
    return g * (1.0 / (1.0 + jnp.exp(-g)))


def _in_kernel(x_ref, g_ref, w_ref, fb_ref, tri_ref, route_ref, bconst_ref,
               mc_ref, msl_ref, msh_ref, lc_ref, lsl_ref, lsh_ref,
               gq_ref, gkv_ref, wuq_ref, wukv_ref, wvt_ref, wuvt_ref,
               foxq_ref, foxk_ref, foxv_ref, fg_ref, su_ref, sg_ref,
               mobq_ref, mobk_ref, mobv_ref, mg_ref,
               mlaq_ref, mlak_ref, mlav_ref, lg_ref,
               carry_ref, km_ref, *, tm):
    sblk = pl.program_id(1)
    nblk = tm // MOBA_BLOCK

    @pl.when(sblk == 0)
    def _():
        carry_ref[...] = jnp.zeros_like(carry_ref)
        km_ref[...] = jnp.zeros_like(km_ref)

    h = _rms(x_ref[0], g_ref[...]).astype(BF16)

    z = [jnp.dot(h, w_ref[:, c0:c0 + width], preferred_element_type=F32) for c0, width in PROJ_GROUPS]

    def proj(c0, width):
        for zg, (g0, gw) in zip(z, PROJ_GROUPS):
            if g0 <= c0 and c0 + width <= g0 + gw:
                return zg[:, c0 - g0:c0 - g0 + width]
        raise ValueError("column range crosses a projection group")

    lane =lax.broadcasted_iota(jnp.int32, (tm, LANES), 1)
    row =lax.broadcasted_iota(jnp.int32, (tm, LANES), 0)
    low_half = lane < HEAD_DIM

    def proj_t(wt):
        return lax.dot_general(wt, h, (((1,), (1,)), ((), ())), preferred_element_type=F32)

    foxv_ref[0, 0] = proj_t(wvt_ref[0:WIDTH, :]).astype(BF16)
    mobv_ref[0, 0] = proj_t(wvt_ref[WIDTH:2 * WIDTH, :]).astype(BF16)
    fg_ref[0] = proj(C_FG, WIDTH)
    sg_ref[0] = proj(C_SG, WIDTH)
    mg_ref[0] = proj(C_MG, WIDTH)
    lg_ref[0] = proj(C_LG, WIDTH)
    su_ref[...] = proj(C_SU, WIDTH)

    ffkr = proj(C_FFKR, LANES)
    ff = ffkr + fb_ref[...]
    logf = -(jnp.maximum(-ff, 0.0) + jnp.log1p(jnp.exp(-jnp.abs(ff))))
    within3 = jnp.dot(tri_ref[...], jnp.concatenate(_split3(logf), axis=1), preferred_element_type=F32)
    within = within3[:, 0:LANES] + within3[:, LANES:2 * LANES] + within3[:, 2 * LANES:]
    cum = within + carry_ref[...]
    carry_ref[...] = cum[tm - 1:tm, :]
    routed = jnp.dot(jnp.concatenate(_split3(cum * LOG2E), axis=1), route_ref[...],
                     preferred_element_type=F32) + bconst_ref[...]
    fq = proj(C_FQ, WIDTH) * (HEAD_DIM ** -0.5 * LOG2E)
    fk = proj(C_FK, WIDTH)
    for hd in range(N_HEADS):
        pair = hd // 2
        own = low_half if hd % 2 == 0 else jnp.logical_not(low_half)
        base = _fox_bias_base(hd)
        mine = (lane >= base) & (lane < base + FOX_BIAS_LANES)
        sl = slice(pair * LANES, (pair + 1) * LANES)
        qa = jnp.where(own, fq[:, sl], jnp.where(mine, routed[:, 0:LANES], 0.0))
        ka = jnp.where(own, fk[:, sl], jnp.where(mine, routed[:, LANES:], 0.0))
        foxq_ref[0, :, hd * LANES:(hd + 1) * LANES] = qa.astype(BF16)
        foxk_ref[0, :, hd * LANES:(hd + 1) * LANES] = ka.astype(BF16)

    mc, msl, msh = mc_ref[...], msl_ref[...], msh_ref[...]
    mq = proj(C_MQ, WIDTH)
    mk = proj(C_MK, WIDTH)
    q_r = [_rope(mq[:, p * LANES:(p + 1) * LANES], mc, msl, msh, HEAD_DIM // 2) for p in range(2)]
    k_r = [_rope(mk[:, p * LANES:(p + 1) * LANES], mc, msl, msh, HEAD_DIM // 2) for p in range(2)]
    km_row = lax.broadcasted_iota(jnp.int32, (LANES, WIDTH), 0)
    km_lane = lax.broadcasted_iota(jnp.int32, (LANES, WIDTH), 1)
    km_grp = lax.shift_right_logical(km_row, 5)
    km_head = jnp.where(km_grp == 0, GROUP_HEAD[0], jnp.where(km_grp == 1, GROUP_HEAD[1],
                        jnp.where(km_grp == 2, GROUP_HEAD[2], GROUP_HEAD[3])))
    km_own = lax.shift_right_logical(km_lane, 6) == km_head
    kmt = km_ref[...]
    for nb in range(nblk):
        blk = sblk * nblk + nb
        km = jnp.concatenate(
            [jnp.mean(k_r[p][nb * MOBA_BLOCK:(nb + 1) * MOBA_BLOCK, :], axis=0, keepdims=True) for p in range(2)],
            axis=1)
        kmt = jnp.where((km_row & (MAX_BLOCKS - 1)) == blk, jnp.where(km_own, km, 0.0), kmt)
    km_ref[...] = kmt
    gate_t = lax.dot_general(kmt, jnp.concatenate(q_r, axis=1), (((1,), (1,)), ((), ())),
                             precision=lax.Precision.HIGHEST, preferred_element_type=F32)
    cand = lax.broadcasted_iota(jnp.int32, (MAX_BLOCKS, tm), 0).astype(F32)
    tok = lax.broadcasted_iota(jnp.int32, (MAX_BLOCKS, tm), 1)
    blk_tok = (sblk * nblk + lax.shift_right_logical(tok, 8)).astype(F32)
    bias_rows = []
    for grp in range(N_HEADS):
        g = jnp.where(cand < blk_tok, gate_t[grp * MAX_BLOCKS:(grp + 1) * MAX_BLOCKS, :], -jnp.inf)
        chosen = jnp.zeros((MAX_BLOCKS, tm), F32)
        for _ in range(MOBA_TOPK):
            m = jnp.max(g, axis=0, keepdims=True)
            first = jnp.min(jnp.where(g == m, cand, 1e9), axis=0, keepdims=True)
            first = jnp.where(m > -jnp.inf, first, -1.0)
            pick = cand == first
            chosen = jnp.where(pick, 1.0, chosen)
            g = jnp.where(pick, -jnp.inf, g)
        keep = jnp.where(cand == blk_tok, 1.0, chosen)
        bias_rows.append(jnp.where(keep > 0.0, 0.0, NEG))
    sel_bias = jnp.concatenate(bias_rows, axis=0).T
    blk_row = sblk * nblk + lax.shift_right_logical(row, 8)
    onehot = jnp.where((lane & (MAX_BLOCKS - 1)) == blk_row, 1.0, 0.0)
    lane_grp = lax.shift_right_logical(lane, 5)
    for hd in range(N_HEADS):
        pair = hd // 2
        own = low_half if hd % 2 == 0 else jnp.logical_not(low_half)
        mine = lane_grp == GROUP_HEAD.index(hd)
        qa = jnp.where(own, q_r[pair] * (HEAD_DIM ** -0.5 * LOG2E), jnp.where(mine, sel_bias, 0.0))
        ka = jnp.where(own, k_r[pair], jnp.where(mine, onehot, 0.0))
        mobq_ref[0, :, hd * LANES:(hd + 1) * LANES] = qa.astype(BF16)
        mobk_ref[0, :, hd * LANES:(hd + 1) * LANES] = ka.astype(BF16)

    lc, lsl, lsh = lc_ref[...], lsl_ref[...], lsh_ref[...]
    cqn = _rms(proj(C_CQ, MLA_Q_RANK), gq_ref[...]).astype(BF16)
    qf = jnp.dot(cqn, wuq_ref[...], preferred_element_type=F32)
    ckvn = _rms(proj(C_CKV, MLA_KV_RANK), gkv_ref[...]).astype(BF16)
    kv = jnp.dot(ckvn, wukv_ref[...], preferred_element_type=F32)
    kr = jnp.where(low_half, 0.0, _rope(ffkr, lc, lsl, lsh, MLA_ROPE // 2))
    scale = (MLA_NOPE + MLA_ROPE) ** -0.5 * LOG2E
    for hd in range(N_HEADS):
        sl = slice(hd * LANES, (hd + 1) * LANES)
        mlaq_ref[0, :, sl] = (_rope(qf[:, sl], lc, lsl, lsh, MLA_ROPE // 2) * scale).astype(BF16)
        mlak_ref[0, :, sl] = (kv[:, sl] + kr).astype(BF16)
    mlav_ref[0, 0] = lax.dot_general(wuvt_ref[...], ckvn, (((1,), (1,)), ((), ())),
                                     preferred_element_type=F32).astype(BF16)


def _attn_kernel(q_ref, k_ref, vt_ref, g_ref, o_ref, sa_ref, sb_ref, mxa_ref, mxb_ref, m_ref, acc_ref, *, t, nq):
    bufs = ((sa_ref, mxa_ref), (sb_ref, mxb_ref))
    kpos = lax.broadcasted_iota(jnp.int32, (t, t), 0)
    qpos = lax.broadcasted_iota(jnp.int32, (t, t), 1)
    ones = jnp.ones((DEN_ROWS, t), BF16)
    pair = lambda i, j: (jnp.int32(i), jnp.int32(j))

    def scores(hh, ij, buf):
        s_ref, mx_ref = buf
        qoff = pl.multiple_of(ij[0] * t, t)
        koff = pl.multiple_of(ij[1] * t, t)
        st = lax.dot_general(k_ref[0, pl.ds(koff, t), hh * LANES:(hh + 1) * LANES],
                             q_ref[0, pl.ds(qoff, t), hh * LANES:(hh + 1) * LANES],
                             (((1,), (1,)), ((), ())), preferred_element_type=F32)
        s_ref[hh] = st
        mx_ref[hh] = jnp.max(st, axis=0, keepdims=True)

    def values(hh, j):
        return jnp.concatenate([vt_ref[0, j, hh * HEAD_DIM:(hh + 1) * HEAD_DIM, :], ones], axis=0)

    def first_update(hh, ij, buf):
        i, j = ij
        st = jnp.where(kpos <= qpos, buf[0][hh], NEG)
        mx = jnp.max(st, axis=0, keepdims=True)
        p = jnp.exp2(st - mx).astype(BF16)
        acc_ref[i, hh] = jnp.dot(values(hh, j), p, preferred_element_type=F32)
        m_ref[i, hh] = mx

    def update(hh, ij, buf):
        i, j = ij
        m = m_ref[i, hh]
        m_new = jnp.maximum(m, buf[1][hh])
        alpha = jnp.exp2(m - m_new)
        p = jnp.exp2(buf[0][hh] - m_new).astype(BF16)
        acc_ref[i, hh] = alpha * acc_ref[i, hh] + jnp.dot(values(hh, j), p, preferred_element_type=F32)
        m_ref[i, hh] = m_new

    def stage(nxt, nxt_buf, cur, cur_buf, consume):
        for hh in range(2):
            if nxt is not None:
                scores(hh, nxt, nxt_buf)
            consume(hh, cur, cur_buf)

    n_past = nq * (nq - 1) // 2
    first_past = pair(min(1, nq - 1), 0)

    for hh in range(2):
        scores(hh, pair(0, 0), bufs[0])

    def diag_of(i):
        inside = i < nq
        return jnp.where(inside, i, first_past[0]), jnp.where(inside, i, first_past[1])

    def two_diagonals(n, carry):
        i = 2 * n
        stage(diag_of(i + 1), bufs[1], (i, i), bufs[0], first_update)
        stage(diag_of(i + 2), bufs[0], (i + 1, i + 1), bufs[1], first_update)
        return carry

    lax.fori_loop(0, nq // 2, two_diagonals, 0)
    if nq % 2 == 1:
        stage(first_past if n_past else None, bufs[1], pair(nq - 1, nq - 1), bufs[0], first_update)
    par = nq % 2

    def succ(ij):
        i, j = ij
        last = j == i - 1
        return jnp.where(last, jnp.minimum(i + 1, nq - 1), i), jnp.where(last, 0, j + 1)

    def past_steps(count, cur, skip_last_scores):
        for k in range(count):
            nxt = succ(cur)
            no_next = skip_last_scores and k == count - 1
            stage(None if no_next else nxt, bufs[(par + k + 1) % 2], cur, bufs[(par + k) % 2], update)
            cur = nxt
        return cur

    cur = lax.fori_loop(0, n_past // PAST_UNROLL, lambda _, c: past_steps(PAST_UNROLL, c, False), first_past)
    if n_past % PAST_UNROLL:
        past_steps(n_past % PAST_UNROLL, cur, True)

    def finish_tile(i, carry):
        out_t = jnp.concatenate([acc_ref[i, hh, 0:HEAD_DIM] / acc_ref[i, hh, HEAD_DIM:HEAD_DIM + 1]
                                 for hh in range(2)], axis=0)
        rows = pl.ds(pl.multiple_of(i * t, t), t)
        o_ref[0, rows, :] = (out_t.T * _silu(g_ref[0, rows, :])).astype(BF16)
        return carry

    lax.fori_loop(0, nq, finish_tile, 0)


def _s5_kernel(u_ref, up_ref, bm_ref, lam_ref, cm_ref, d_ref, gw_ref, gb_ref, o_ref, xa_ref, xb_ref, st_ref,
               *, steps, nbatch, nchunks):
    c = pl.program_id(0)

    @pl.when(c == 0)
    def _():
        st_ref[...] = jnp.zeros_like(st_ref)
        xb_ref[...] = jnp.zeros_like(xb_ref)

    lam_re = jnp.broadcast_to(lam_ref[0:1, :], (nbatch, S5_NSTATE))
    lam_im = jnp.broadcast_to(lam_ref[1:2, :], (nbatch, S5_NSTATE))

    def chunk(x_cur, x_prev):
        x_cur[...] = jnp.dot(u_ref[...].astype(BF16), bm_ref[...], preferred_element_type=F32)
        y = jnp.dot(x_prev[...].astype(BF16), cm_ref[...], preferred_element_type=F32) + d_ref[...] * up_ref[...]
        y = 0.5 * y * (1.0 + jnp.tanh(0.7978845608028654 * (y + 0.044715 * (y * y * y))))
        z = jnp.dot(y.astype(BF16), gw_ref[...], preferred_element_type=F32) + gb_ref[...]
        o_ref[...] = y * (1.0 / (1.0 + jnp.exp(-z)))

        def body(t, carry):
            xr, xi = carry
            off = pl.multiple_of(t * nbatch, nbatch)
            nr = lam_re * xr - lam_im * xi + x_cur[pl.ds(off, nbatch), 0:S5_NSTATE]
            ni = lam_re * xi + lam_im * xr + x_cur[pl.ds(off, nbatch), S5_NSTATE:2 * S5_NSTATE]
            x_cur[pl.ds(off, nbatch), 0:S5_NSTATE] = nr
            x_cur[pl.ds(off, nbatch), S5_NSTATE:2 * S5_NSTATE] = ni
            return nr, ni

        @pl.when(c < nchunks)
        def _():
            xr, xi = lax.fori_loop(0, steps, body, (st_ref[:, 0:S5_NSTATE], st_ref[:, S5_NSTATE:2 * S5_NSTATE]))
            st_ref[:, 0:S5_NSTATE] = xr
            st_ref[:, S5_NSTATE:2 * S5_NSTATE] = xi

    @pl.when(c % 2 == 0)
    def _():
        chunk(xa_ref, xb_ref)

    @pl.when(c % 2 == 1)
    def _():
        chunk(xb_ref, xa_ref)


def _out_kernel(x_ref, a_ref, b_ref, sg_ref, c_ref, d_ref, w_ref, fg_ref, o_ref, *, final):
    b = (b_ref[...] * _silu(sg_ref[0])).astype(BF16)
    y = (jnp.dot(a_ref[0], w_ref[0:WIDTH, :], preferred_element_type=F32)
         + jnp.dot(b, w_ref[WIDTH:2 * WIDTH, :], preferred_element_type=F32)
         + jnp.dot(c_ref[0], w_ref[2 * WIDTH:3 * WIDTH, :], preferred_element_type=F32)
         + jnp.dot(d_ref[0], w_ref[3 * WIDTH:4 * WIDTH, :], preferred_element_type=F32))
    xn = x_ref[0] + y
    if final:
        xn = _rms(xn, fg_ref[...])
    o_ref[0] = xn


def _tiles(seq):
    tm = 512 if seq % 512 == 0 else MOBA_BLOCK
    steps = 128
    return tm, steps


def _arrange_w_in(w):
    z = lambda n: jnp.zeros((D_MODEL, n), w.dtype)
    o = 0
    parts = {}
    for name, n in (("fq", 256), ("fk", 256), ("fv", 256), ("fg", 256), ("ff", 4), ("su", 256), ("sg", 256),
                    ("mq", 256), ("mk", 256), ("mv", 256), ("mg", 256), ("cq", 384), ("ckv", 128), ("kr", 32),
                    ("lg", 256)):
        parts[name] = w[:, o:o + n]
        o += n
    cols = [parts["fq"], parts["fk"], parts["fg"], parts["su"], parts["sg"],
            parts["mq"], parts["mk"], parts["mg"], parts["lg"], parts["cq"], parts["ckv"],
            parts["ff"], z(MLA_NOPE - 4), parts["kr"], z(LANES - MLA_NOPE - MLA_ROPE)]
    w_vt = jnp.concatenate([parts["fv"], parts["mv"]], axis=1).T
    return jnp.concatenate(cols, axis=1).astype(BF16), w_vt.astype(BF16)


def _fox_bias_base(hd):
    return (HEAD_DIM if hd % 2 == 0 else 0) + FOX_BIAS_LANES * (hd // 2)


def _fox_routing():
    import numpy as np
    route = np.zeros((3 * LANES, 2 * LANES), np.float32)
    const = np.zeros((1, 2 * LANES), np.float32)
    for hd in range(N_HEADS):
        base = _fox_bias_base(hd)
        for part in range(3):
            route[part * LANES + hd, base + part] = 1.0
            const[0, base + 3 + part] = 1.0
            const[0, LANES + base + part] = 1.0
            route[part * LANES + hd, LANES + base + 3 + part] = -1.0
    return jnp.asarray(route, BF16), jnp.asarray(const, F32)


def _rope_tables(seq):
    pos = jnp.arange(seq).astype(F32)[:, None]
    lane = jnp.arange(LANES)
    half = HEAD_DIM // 2
    inv = jnp.power(ROPE_THETA, -jnp.arange(half, dtype=F32) / half)
    ang = pos * inv[None, :]
    cos, sin = jnp.cos(ang)[:, lane % half], jnp.sin(ang)[:, lane % half]
    lo = (lane % HEAD_DIM) < half
    moba = (cos, jnp.where(lo, -sin, 0.0), jnp.where(lo, 0.0, sin))
    half = MLA_ROPE // 2
    inv = jnp.power(ROPE_THETA, -jnp.arange(half, dtype=F32) / half)
    ang = pos * inv[None, :]
    cos, sin = jnp.cos(ang)[:, lane % half], jnp.sin(ang)[:, lane % half]
    in_lo = (lane >= MLA_NOPE) & (lane < MLA_NOPE + half)
    in_hi = (lane >= MLA_NOPE + half) & (lane < MLA_NOPE + MLA_ROPE)
    mla = (jnp.where(in_lo | in_hi, cos, 1.0), jnp.where(in_lo, -sin, 0.0), jnp.where(in_hi, sin, 0.0))
    return moba, mla


def _arrange_mla(w_uq, w_ukv):
    z = lambda r, n: jnp.zeros((r, n), F32)
    dq = MLA_NOPE + MLA_ROPE
    q_cols, k_cols, v_cols = [], [], []
    for hd in range(N_HEADS):
        q_cols += [w_uq[:, hd * dq:(hd + 1) * dq], z(MLA_Q_RANK, LANES - dq)]
        base = hd * (MLA_NOPE + MLA_V)
        k_cols += [w_ukv[:, base:base + MLA_NOPE], z(MLA_KV_RANK, LANES - MLA_NOPE)]
        v_cols += [w_ukv[:, base + MLA_NOPE:base + MLA_NOPE + MLA_V]]
    return (jnp.concatenate(q_cols, axis=1).astype(BF16), jnp.concatenate(k_cols, axis=1).astype(BF16),
            jnp.concatenate(v_cols, axis=1).T.astype(BF16))


def _s5_matrices(a_re, a_im, log_dt, b_re, b_im, c_re, c_im):
    lam = lax.complex(a_re.astype(F32), a_im.astype(F32))
    dt = jnp.exp(log_dt.astype(F32))[:, None]
    lam_bar = jnp.exp(lam * dt)
    b_bar = ((lam_bar - 1.0) / lam)[..., None] * lax.complex(b_re.astype(F32), b_im.astype(F32))
    eye = jnp.eye(S5_GROUPS, dtype=F32)
    blockdiag_in = lambda t: jnp.einsum('gpc,gh->gchp', t, eye).reshape(WIDTH, S5_NSTATE)
    blockdiag_out = lambda t: jnp.einsum('gcp,gh->gphc', t, eye).reshape(S5_NSTATE, WIDTH)
    bm = jnp.concatenate([blockdiag_in(b_bar.real), blockdiag_in(b_bar.imag)], axis=1).astype(BF16)
    cm = jnp.concatenate([blockdiag_out(c_re.astype(F32)), -blockdiag_out(c_im.astype(F32))], axis=0).astype(BF16)
    lam2 = jnp.stack([lam_bar.real.reshape(S5_NSTATE), lam_bar.imag.reshape(S5_NSTATE)], axis=0)
    return bm, lam2, cm


def _full(shape):
    return pl.BlockSpec(shape, lambda *_: (0,) * len(shape))


def _in_call(x, g, w, fb, tri, route, bconst, moba_t, mla_t, gq, gkv, wuq, wukv, wvt, wuvt, tm):
    nb, seq, _ = x.shape
    tok = lambda width: pl.BlockSpec((1, tm, width), lambda b, s: (b, s, 0))
    tab = pl.BlockSpec((tm, LANES), lambda b, s: (s, 0))
    bf = lambda width: jax.ShapeDtypeStruct((nb, seq, width), BF16)
    f32 = lambda width: jax.ShapeDtypeStruct((nb, seq, width), F32)
    val_t = jax.ShapeDtypeStruct((nb, seq // tm, WIDTH, tm), BF16)
    val_t_spec = pl.BlockSpec((1, 1, WIDTH, tm), lambda b, s: (b, s, 0, 0))
    out_shape = (bf(512), bf(512), val_t, f32(256),
                 jax.ShapeDtypeStruct((seq, nb * WIDTH), F32), f32(256),
                 bf(512), bf(512), val_t, f32(256),
                 bf(512), bf(512), val_t, f32(256))
    out_specs = (tok(512), tok(512), val_t_spec, tok(256),
                 pl.BlockSpec((tm, WIDTH), lambda b, s: (s, b)), tok(256),
                 tok(512), tok(512), val_t_spec, tok(256),
                 tok(512), tok(512), val_t_spec, tok(256))
    in_specs = [tok(D_MODEL), _full((1, D_MODEL)), _full((D_MODEL, N_COLS)), _full((1, LANES)),
                _full((tm, tm)), _full((3 * LANES, 2 * LANES)), _full((1, 2 * LANES)),
                tab, tab, tab, tab, tab, tab,
                _full((1, MLA_Q_RANK)), _full((1, MLA_KV_RANK)),
                _full((MLA_Q_RANK, N_HEADS * LANES)), _full((MLA_KV_RANK, N_HEADS * LANES)),
                _full((2 * WIDTH, D_MODEL)), _full((WIDTH, MLA_KV_RANK))]
    return pl.pallas_call(
        functools.partial(_in_kernel, tm=tm),
        grid=(nb, seq // tm),
        in_specs=in_specs, out_specs=out_specs, out_shape=out_shape,
        scratch_shapes=[pltpu.VMEM((1, LANES), F32), pltpu.VMEM((LANES, WIDTH), F32)],
        compiler_params=pltpu.CompilerParams(dimension_semantics=("arbitrary", "arbitrary"),
                                             vmem_limit_bytes=VMEM_LIMIT),
        name="in_proj",
    )(x, g, w, fb, tri, route, bconst, *moba_t, *mla_t, gq, gkv, wuq, wukv, wvt, wuvt)


def _attn_call(q, k, vt, gate, t, name):
    nb, seq, _ = q.shape
    whole = lambda width: pl.BlockSpec((1, seq, width), lambda b, p: (b, 0, p))
    return pl.pallas_call(
        functools.partial(_attn_kernel, t=t, nq=seq // t),
        grid=(nb, 2),
        in_specs=[whole(2 * LANES), whole(2 * LANES),
                  pl.BlockSpec((1, seq // t, LANES, t), lambda b, p: (b, 0, p, 0)), whole(LANES)],
        out_specs=whole(LANES),
        out_shape=jax.ShapeDtypeStruct((nb, seq, WIDTH), BF16),
        scratch_shapes=[pltpu.VMEM((2, t, t), F32), pltpu.VMEM((2, t, t), F32),
                        pltpu.VMEM((2, 1, t), F32), pltpu.VMEM((2, 1, t), F32),
                        pltpu.VMEM((seq // t, 2, 1, t), F32),
                        pltpu.VMEM((seq // t, 2, HEAD_DIM + DEN_ROWS, t), F32)],
        compiler_params=pltpu.CompilerParams(dimension_semantics=("arbitrary", "arbitrary"),
                                             vmem_limit_bytes=VMEM_LIMIT),
        name=name,
    )(q, k, vt, gate)


def _s5_call(u, bm, lam2, cm, d, gw, gb, steps, nbatch):
    rows = u.shape[0]
    blk = steps * nbatch
    nchunks = rows // blk
    prev = lambda c: (jnp.maximum(c - 1, 0), 0)
    return pl.pallas_call(
        functools.partial(_s5_kernel, steps=steps, nbatch=nbatch, nchunks=nchunks),
        grid=(nchunks + 1,),
        in_specs=[pl.BlockSpec((blk, WIDTH), lambda c: (jnp.minimum(c, nchunks - 1), 0)),
                  pl.BlockSpec((blk, WIDTH), prev),
                  _full((WIDTH, 2 * S5_NSTATE)), _full((2, S5_NSTATE)), _full((2 * S5_NSTATE, WIDTH)),
                  _full((1, WIDTH)), _full((WIDTH, WIDTH)), _full((1, WIDTH))],
        out_specs=pl.BlockSpec((blk, WIDTH), prev),
        out_shape=jax.ShapeDtypeStruct((rows, WIDTH), F32),
        scratch_shapes=[pltpu.VMEM((blk, 2 * S5_NSTATE), F32), pltpu.VMEM((blk, 2 * S5_NSTATE), F32),
                        pltpu.VMEM((nbatch, 2 * S5_NSTATE), F32)],
        compiler_params=pltpu.CompilerParams(dimension_semantics=("arbitrary",),
                                             vmem_limit_bytes=VMEM_LIMIT),
        name="s5",
    )(u, u, bm, lam2, cm, d, gw, gb)


def _out_call(x, a, b2d, sg, c, d, w, fg, tm, final):
    nb, seq, _ = x.shape
    tok = lambda width: pl.BlockSpec((1, tm, width), lambda b, s: (b, s, 0))
    return pl.pallas_call(
        functools.partial(_out_kernel, final=final),
        grid=(nb, seq // tm),
        in_specs=[tok(D_MODEL), tok(WIDTH), pl.BlockSpec((tm, WIDTH), lambda b, s: (s, b)), tok(WIDTH),
                  tok(WIDTH), tok(WIDTH), _full((4 * WIDTH, D_MODEL)), _full((1, D_MODEL))],
        out_specs=tok(D_MODEL),
        out_shape=jax.ShapeDtypeStruct(x.shape, F32),
        compiler_params=pltpu.CompilerParams(dimension_semantics=("arbitrary", "arbitrary"),
                                             vmem_limit_bytes=VMEM_LIMIT),
        name="out_proj",
    )(x, a, b2d, sg, c, d, w, fg)


def kernel(x, norm_g, w_in, fox_fb, s5_a_re, s5_a_im, s5_log_dt, s5_b_re, s5_b_im, s5_c_re, s5_c_im, s5_d,
           s5_glu_w, s5_glu_b, mla_q_norm, mla_w_uq, mla_kv_norm, mla_w_ukv, w_out, final_g):
    nb, seq, _ = x.shape
    depth = norm_g.shape[0]
    assert nb == 8, "the S5 recurrence keeps the batch on the 8 sublanes of a vreg"
    tm, steps = _tiles(seq)
    assert seq % tm == 0 and seq % steps == 0 and seq // MOBA_BLOCK <= 32

    tri = jnp.tril(jnp.ones((tm, tm), F32)).astype(BF16)
    route, bconst = _fox_routing()
    moba_t, mla_t = _rope_tables(seq)
    fgain = final_g.astype(F32).reshape(1, D_MODEL)

    for l in range(depth):
        w, wvt = _arrange_w_in(w_in[l])
        fb = jnp.zeros((1, LANES), F32).at[0, :N_HEADS].set(fox_fb[l].astype(F32))
        wuq, wukv, wuvt = _arrange_mla(mla_w_uq[l].astype(F32), mla_w_ukv[l].astype(F32))
        (foxq, foxk, foxv, fg, su, sg, mobq, mobk, mobv, mg, mlaq, mlak, mlav, lg) = _in_call(
            x, norm_g[l].astype(F32).reshape(1, D_MODEL), w, fb, tri, route, bconst, moba_t, mla_t,
            mla_q_norm[l].astype(F32).reshape(1, MLA_Q_RANK), mla_kv_norm[l].astype(F32).reshape(1, MLA_KV_RANK),
            wuq, wukv, wvt, wuvt, tm)

        a_out = _attn_call(foxq, foxk, foxv, fg, tm, "fox_attn")
        c_out = _attn_call(mobq, mobk, mobv, mg, tm, "moba_attn")
        d_out = _attn_call(mlaq, mlak, mlav, lg, tm, "mla_attn")

        bm, lam2, cm = _s5_matrices(s5_a_re[l], s5_a_im[l], s5_log_dt[l], s5_b_re[l], s5_b_im[l],
                                    s5_c_re[l], s5_c_im[l])
        b_rows = _s5_call(su.reshape(seq * nb, WIDTH), bm, lam2, cm,
                          s5_d[l].astype(F32).reshape(1, WIDTH), s5_glu_w[l].astype(BF16),
                          s5_glu_b[l].astype(F32).reshape(1, WIDTH), steps, nb)
        x = _out_call(x, a_out, b_rows.reshape(seq, nb * WIDTH), sg, c_out, d_out,
                      w_out[l].astype(BF16), fgain, tm, final=(l == depth - 1))
    return x
```

```python
import functools

import jax
import jax.numpy as jnp
from jax import lax
from jax.experimental import pallas as pl
from jax.experimental.pallas import tpu as pltpu

F32 = jnp.float32
BF16 = jnp.bfloat16

D_MODEL = 1024
HEAD_DIM = 64
DEN_ROWS = 16
PAST_UNROLL = 8
SKIP_GAP = 152.0
NORM_SLACK = 1.01
N_HEADS = 4
WIDTH = 256
S5_GROUPS = 16
S5_GROUP = 16
S5_STATE = 64
S5_NSTATE = S5_GROUPS * S5_STATE
MOBA_BLOCK = 256
MOBA_TOPK = 3
MAX_BLOCKS = 32
GROUP_HEAD = (1, 3, 0, 2)
LOG2E = 1.4426950408889634
MLA_NOPE = 64
MLA_ROPE = 32
MLA_V = 64
MLA_Q_RANK = 384
MLA_KV_RANK = 128
ROPE_THETA = 10000.0
EPS = 1e-6
NEG = -1e30
LANES = 128
VMEM_LIMIT = 56 * 1024 * 1024

C_FQ, C_FK, C_FG = 0, 256, 512
C_SU, C_SG = 768, 1024
C_MQ, C_MK, C_MG = 1280, 1536, 1792
C_LG = 2048
C_CQ = 2304
C_CKV = 2688
C_FFKR = 2816
N_COLS = 2944
PROJ_GROUPS = ((0, 768), (768, 512), (1280, 768), (2048, 896))
FOX_BIAS_LANES = 6


def _split3(x):
    x1 = x.astype(BF16)
    r1 = x - x1.astype(F32)
    x2 = r1.astype(BF16)
    r2 = r1 - x2.astype(F32)
    return x1, x2, r2.astype(BF16)


def _rope(x, cos, sin_lo, sin_hi, half):
    return (x * cos + pltpu.roll(x, LANES - half, 1) * sin_lo + pltpu.roll(x, half, 1) * sin_hi)


def _rms(x, g):
    return x * lax.rsqrt(jnp.mean(x * x, axis=-1, keepdims=True) + EPS) * g


def _silu(g):
    return g * (1.0 / (1.0 + jnp.exp(-g)))


def _in_kernel(x_ref, g_ref, w_ref, fb_ref, tri_ref, route_ref, bconst_ref,
               mc_ref, msl_ref, msh_ref, lc_ref, lsl_ref, lsh_ref,
               gq_ref, gkv_ref, wuq_ref, wukv_ref, wvt_ref, wuvt_ref,
               foxq_ref, foxk_ref, foxv_ref, fg_ref, su_ref, sg_ref,
               mobq_ref, mobk_ref, mobv_ref, mg_ref,
               mlaq_ref, mlak_ref, mlav_ref, lg_ref, fstat_ref,
               carry_ref, km_ref, *, tm):
    sblk = pl.program_id(1)
    nblk = tm // MOBA_BLOCK

    @pl.when(sblk == 0)
    def _():
        carry_ref[...] = jnp.zeros_like(carry_ref)
        km_ref[...] = jnp.zeros_like(km_ref)

    h = _rms(x_ref[0], g_ref[...]).astype(BF16)

    z = [jnp.dot(h, w_ref[:, c0:c0 + width], preferred_element_type=F32) for c0, width in PROJ_GROUPS]

    def proj(c0, width):
        for zg, (g0, gw) in zip(z, PROJ_GROUPS):
            if g0 <= c0 and c0 + width <= g0 + gw:
                return zg[:, c0 - g0:c0 - g0 + width]
        raise ValueError("column range crosses a projection group")

    lane =lax.broadcasted_iota(jnp.int32, (tm, LANES), 1)
    row =lax.broadcasted_iota(jnp.int32, (tm, LANES), 0)
    low_half = lane < HEAD_DIM

    def proj_t(wt):
        return lax.dot_general(wt, h, (((1,), (1,)), ((), ())), preferred_element_type=F32)

    foxv_ref[0, 0] = proj_t(wvt_ref[0:WIDTH, :]).astype(BF16)
    mobv_ref[0, 0] = proj_t(wvt_ref[WIDTH:2 * WIDTH, :]).astype(BF16)
    fg_ref[0] = proj(C_FG, WIDTH)
    sg_ref[0] = proj(C_SG, WIDTH)
    mg_ref[0] = proj(C_MG, WIDTH)
    lg_ref[0] = proj(C_LG, WIDTH)
    su_ref[...] = proj(C_SU, WIDTH)

    ffkr = proj(C_FFKR, LANES)
    ff = ffkr + fb_ref[...]
    logf = -(jnp.maximum(-ff, 0.0) + jnp.log1p(jnp.exp(-jnp.abs(ff))))
    within3 = jnp.dot(tri_ref[...], jnp.concatenate(_split3(logf), axis=1), preferred_element_type=F32)
    within = within3[:, 0:LANES] + within3[:, LANES:2 * LANES] + within3[:, 2 * LANES:]
    cum = within + carry_ref[...]
    carry_ref[...] = cum[tm - 1:tm, :]
    cum2 = cum * LOG2E
    routed = jnp.dot(jnp.concatenate(_split3(cum2), axis=1), route_ref[...],
                     preferred_element_type=F32) + bconst_ref[...]
    fq = proj(C_FQ, WIDTH) * (HEAD_DIM ** -0.5 * LOG2E)
    fk = proj(C_FK, WIDTH)
    lane_row = lane[0:1, :]
    qmax_row = jnp.zeros((1, LANES), F32)
    kmax_row = jnp.zeros((1, LANES), F32)
    for hd in range(N_HEADS):
        pair = hd // 2
        own = low_half if hd % 2 == 0 else jnp.logical_not(low_half)
        base = _fox_bias_base(hd)
        mine = (lane >= base) & (lane < base + FOX_BIAS_LANES)
        sl = slice(pair * LANES, (pair + 1) * LANES)
        qb = fq[:, sl].astype(BF16)
        kb = fk[:, sl].astype(BF16)
        qa = jnp.where(own, qb, jnp.where(mine, routed[:, 0:LANES], 0.0).astype(BF16))
        ka = jnp.where(own, kb, jnp.where(mine, routed[:, LANES:], 0.0).astype(BF16))
        foxq_ref[0, :, hd * LANES:(hd + 1) * LANES] = qa
        foxk_ref[0, :, hd * LANES:(hd + 1) * LANES] = ka
        for rounded, is_q in ((qb, True), (kb, False)):
            r = jnp.where(own, rounded.astype(F32), 0.0)
            norm = jnp.sqrt(jnp.max(jnp.sum(r * r, axis=1, keepdims=True), axis=0, keepdims=True))
            if is_q:
                qmax_row = jnp.where(lane_row == hd, norm, qmax_row)
            else:
                kmax_row = jnp.where(lane_row == hd, norm, kmax_row)
    srow = lax.broadcasted_iota(jnp.int32, (8, LANES), 0)
    fstat_ref[0, 0] = jnp.where(srow == 0, qmax_row, jnp.where(srow == 1, kmax_row, jnp.where(
        srow == 2, jnp.max(cum2, axis=0, keepdims=True), jnp.where(
            srow == 3, jnp.min(cum2, axis=0, keepdims=True), 0.0))))

    mc, msl, msh = mc_ref[...], msl_ref[...], msh_ref[...]
    mq = proj(C_MQ, WIDTH)
    mk = proj(C_MK, WIDTH)
    q_r = [_rope(mq[:, p * LANES:(p + 1) * LANES], mc, msl, msh, HEAD_DIM // 2) for p in range(2)]
    k_r = [_rope(mk[:, p * LANES:(p + 1) * LANES], mc, msl, msh, HEAD_DIM // 2) for p in range(2)]
    km_row = lax.broadcasted_iota(jnp.int32, (LANES, WIDTH), 0)
    km_lane = lax.broadcasted_iota(jnp.int32, (LANES, WIDTH), 1)
    km_grp = lax.shift_right_logical(km_row, 5)
    km_head = jnp.where(km_grp == 0, GROUP_HEAD[0], jnp.where(km_grp == 1, GROUP_HEAD[1],
                        jnp.where(km_grp == 2, GROUP_HEAD[2], GROUP_HEAD[3])))
    km_own = lax.shift_right_logical(km_lane, 6) == km_head
    kmt = km_ref[...]
    for nb in range(nblk):
        blk = sblk * nblk + nb
        km = jnp.concatenate(
            [jnp.mean(k_r[p][nb * MOBA_BLOCK:(nb + 1) * MOBA_BLOCK, :], axis=0, keepdims=True) for p in range(2)],
            axis=1)
        kmt = jnp.where((km_row & (MAX_BLOCKS - 1)) == blk, jnp.where(km_own, km, 0.0), kmt)
    km_ref[...] = kmt
    gate_t = lax.dot_general(kmt, jnp.concatenate(q_r, axis=1), (((1,), (1,)), ((), ())),
                             precision=lax.Precision.HIGHEST, preferred_element_type=F32)
    cand = lax.broadcasted_iota(jnp.int32, (MAX_BLOCKS, tm), 0).astype(F32)
    tok = lax.broadcasted_iota(jnp.int32, (MAX_BLOCKS, tm), 1)
    blk_tok = (sblk * nblk + lax.shift_right_logical(tok, 8)).astype(F32)
    bias_rows = []
    for grp in range(N_HEADS):
        g = jnp.where(cand < blk_tok, gate_t[grp * MAX_BLOCKS:(grp + 1) * MAX_BLOCKS, :], -jnp.inf)
        chosen = jnp.zeros((MAX_BLOCKS, tm), F32)
        for _ in range(MOBA_TOPK):
            m = jnp.max(g, axis=0, keepdims=True)
            first = jnp.min(jnp.where(g == m, cand, 1e9), axis=0, keepdims=True)
            first = jnp.where(m > -jnp.inf, first, -1.0)
            pick = cand == first
            chosen = jnp.where(pick, 1.0, chosen)
            g = jnp.where(pick, -jnp.inf, g)
        keep = jnp.where(cand == blk_tok, 1.0, chosen)
        bias_rows.append(jnp.where(keep > 0.0, 0.0, NEG))
    sel_bias = jnp.concatenate(bias_rows, axis=0).T
    blk_row = sblk * nblk + lax.shift_right_logical(row, 8)
    onehot = jnp.where((lane & (MAX_BLOCKS - 1)) == blk_row, 1.0, 0.0)
    lane_grp = lax.shift_right_logical(lane, 5)
    for hd in range(N_HEADS):
        pair = hd // 2
        own = low_half if hd % 2 == 0 else jnp.logical_not(low_half)
        mine = lane_grp == GROUP_HEAD.index(hd)
        qa = jnp.where(own, q_r[pair] * (HEAD_DIM ** -0.5 * LOG2E), jnp.where(mine, sel_bias, 0.0))
        ka = jnp.where(own, k_r[pair], jnp.where(mine, onehot, 0.0))
        mobq_ref[0, :, hd * LANES:(hd + 1) * LANES] = qa.astype(BF16)
        mobk_ref[0, :, hd * LANES:(hd + 1) * LANES] = ka.astype(BF16)

    lc, lsl, lsh = lc_ref[...], lsl_ref[...], lsh_ref[...]
    cqn = _rms(proj(C_CQ, MLA_Q_RANK), gq_ref[...]).astype(BF16)
    qf = jnp.dot(cqn, wuq_ref[...], preferred_element_type=F32)
    ckvn = _rms(proj(C_CKV, MLA_KV_RANK), gkv_ref[...]).astype(BF16)
    kv = jnp.dot(ckvn, wukv_ref[...], preferred_element_type=F32)
    kr = jnp.where(low_half, 0.0, _rope(ffkr, lc, lsl, lsh, MLA_ROPE // 2))
    scale = (MLA_NOPE + MLA_ROPE) ** -0.5 * LOG2E
    for hd in range(N_HEADS):
        sl = slice(hd * LANES, (hd + 1) * LANES)
        mlaq_ref[0, :, sl] = (_rope(qf[:, sl], lc, lsl, lsh, MLA_ROPE // 2) * scale).astype(BF16)
        mlak_ref[0, :, sl] = (kv[:, sl] + kr).astype(BF16)
    mlav_ref[0, 0] = lax.dot_general(wuvt_ref[...], ckvn, (((1,), (1,)), ((), ())),
                                     preferred_element_type=F32).astype(BF16)


def _attn_kernel(tbl_ref, q_ref, k_ref, vt_ref, g_ref, o_ref, sa_ref, sb_ref, mxa_ref, mxb_ref, m_ref, acc_ref,
                 cur_ref, *, t, nq):
    bufs = ((sa_ref, mxa_ref), (sb_ref, mxb_ref))
    kpos = lax.broadcasted_iota(jnp.int32, (t, t), 0)
    qpos = lax.broadcasted_iota(jnp.int32, (t, t), 1)
    ones = jnp.ones((DEN_ROWS, t), BF16)
    pair = lambda i, j: (jnp.int32(i), jnp.int32(j))

    def scores(hh, ij, buf):
        s_ref, mx_ref = buf
        qoff = pl.multiple_of(ij[0] * t, t)
        koff = pl.multiple_of(ij[1] * t, t)
        st = lax.dot_general(k_ref[0, pl.ds(koff, t), hh * LANES:(hh + 1) * LANES],
                             q_ref[0, pl.ds(qoff, t), hh * LANES:(hh + 1) * LANES],
                             (((1,), (1,)), ((), ())), preferred_element_type=F32)
        s_ref[hh] = st
        mx_ref[hh] = jnp.max(st, axis=0, keepdims=True)

    def values(hh, j):
        return jnp.concatenate([vt_ref[0, j, hh * HEAD_DIM:(hh + 1) * HEAD_DIM, :], ones], axis=0)

    def first_update(hh, ij, buf):
        i, j = ij
        st = jnp.where(kpos <= qpos, buf[0][hh], NEG)
        mx = jnp.max(st, axis=0, keepdims=True)
        p = jnp.exp2(st - mx).astype(BF16)
        acc_ref[i, hh] = jnp.dot(values(hh, j), p, preferred_element_type=F32)
        m_ref[i, hh] = mx

    def update(hh, ij, buf):
        i, j = ij
        m = m_ref[i, hh]
        m_new = jnp.maximum(m, buf[1][hh])
        alpha = jnp.exp2(m - m_new)
        p = jnp.exp2(buf[0][hh] - m_new).astype(BF16)
        acc_ref[i, hh] = alpha * acc_ref[i, hh] + jnp.dot(values(hh, j), p, preferred_element_type=F32)
        m_ref[i, hh] = m_new

    def stage(nxt, nxt_buf, cur, cur_buf, consume):
        for hh in range(2):
            if nxt is not None:
                scores(hh, nxt, nxt_buf)
            consume(hh, cur, cur_buf)

    tbl = (pl.program_id(0) * 2 + pl.program_id(1)) * (nq + 1)
    first_block = lambda i: tbl_ref[tbl + i]
    n_past = tbl_ref[tbl + nq]
    first_past = pair(min(1, nq - 1), 0)

    for hh in range(2):
        scores(hh, pair(0, 0), bufs[0])

    def diag_of(i):
        inside = i < nq
        return jnp.where(inside, i, first_past[0]), jnp.where(inside, i, first_past[1])

    def two_diagonals(n, carry):
        i = 2 * n
        stage(diag_of(i + 1), bufs[1], (i, i), bufs[0], first_update)
        stage(diag_of(i + 2), bufs[0], (i + 1, i + 1), bufs[1], first_update)
        return carry

    lax.fori_loop(0, nq // 2, two_diagonals, 0)
    if nq % 2 == 1:
        stage(first_past if nq > 1 else None, bufs[1], pair(nq - 1, nq - 1), bufs[0], first_update)
    par = nq % 2

    def succ(ij):
        i, j = ij
        last = j == i - 1
        up = jnp.minimum(i + 1, nq - 1)
        return jnp.where(last, up, i), jnp.where(last, first_block(up), j + 1)

    def past_steps(count, cur):
        for k in range(count):
            nxt = succ(cur)
            stage(nxt, bufs[(par + k + 1) % 2], cur, bufs[(par + k) % 2], update)
            cur = nxt
        return cur

    cur = lax.fori_loop(0, n_past // PAST_UNROLL, lambda _, c: past_steps(PAST_UNROLL, c), first_past)
    cur_ref[0], cur_ref[1] = cur
    count = PAST_UNROLL // 2
    while count:
        @pl.when((n_past & count) != 0)
        def _(count=count):
            cur_ref[0], cur_ref[1] = past_steps(count, (cur_ref[0], cur_ref[1]))
        count //= 2

    def finish_tile(i, carry):
        out_t = jnp.concatenate([acc_ref[i, hh, 0:HEAD_DIM] / acc_ref[i, hh, HEAD_DIM:HEAD_DIM + 1]
                                 for hh in range(2)], axis=0)
        rows = pl.ds(pl.multiple_of(i * t, t), t)
        o_ref[0, rows, :] = (out_t.T * _silu(g_ref[0, rows, :])).astype(BF16)
        return carry

    lax.fori_loop(0, nq, finish_tile, 0)


def _s5_kernel(u_ref, up_ref, bm_ref, lam_ref, cm_ref, d_ref, gw_ref, gb_ref, o_ref, xa_ref, xb_ref, st_ref,
               *, steps, nbatch, nchunks):
    c = pl.program_id(0)

    @pl.when(c == 0)
    def _():
        st_ref[...] = jnp.zeros_like(st_ref)
        xb_ref[...] = jnp.zeros_like(xb_ref)

    lam_re = jnp.broadcast_to(lam_ref[0:1, :], (nbatch, S5_NSTATE))
    lam_im = jnp.broadcast_to(lam_ref[1:2, :], (nbatch, S5_NSTATE))

    def chunk(x_cur, x_prev):
        x_cur[...] = jnp.dot(u_ref[...].astype(BF16), bm_ref[...], preferred_element_type=F32)
        y = jnp.dot(x_prev[...].astype(BF16), cm_ref[...], preferred_element_type=F32) + d_ref[...] * up_ref[...]
        y = 0.5 * y * (1.0 + jnp.tanh(0.7978845608028654 * (y + 0.044715 * (y * y * y))))
        z = jnp.dot(y.astype(BF16), gw_ref[...], preferred_element_type=F32) + gb_ref[...]
        o_ref[...] = y * (1.0 / (1.0 + jnp.exp(-z)))

        def body(t, carry):
            xr, xi = carry
            off = pl.multiple_of(t * nbatch, nbatch)
            nr = lam_re * xr - lam_im * xi + x_cur[pl.ds(off, nbatch), 0:S5_NSTATE]
            ni = lam_re * xi + lam_im * xr + x_cur[pl.ds(off, nbatch), S5_NSTATE:2 * S5_NSTATE]
            x_cur[pl.ds(off, nbatch), 0:S5_NSTATE] = nr
            x_cur[pl.ds(off, nbatch), S5_NSTATE:2 * S5_NSTATE] = ni
            return nr, ni

        @pl.when(c < nchunks)
        def _():
            xr, xi = lax.fori_loop(0, steps, body, (st_ref[:, 0:S5_NSTATE], st_ref[:, S5_NSTATE:2 * S5_NSTATE]))
            st_ref[:, 0:S5_NSTATE] = xr
            st_ref[:, S5_NSTATE:2 * S5_NSTATE] = xi

    @pl.when(c % 2 == 0)
    def _():
        chunk(xa_ref, xb_ref)

    @pl.when(c % 2 == 1)
    def _():
        chunk(xb_ref, xa_ref)


def _out_kernel(x_ref, a_ref, b_ref, sg_ref, c_ref, d_ref, w_ref, fg_ref, o_ref, *, final):
    b = (b_ref[...] * _silu(sg_ref[0])).astype(BF16)
    y = (jnp.dot(a_ref[0], w_ref[0:WIDTH, :], preferred_element_type=F32)
         + jnp.dot(b, w_ref[WIDTH:2 * WIDTH, :], preferred_element_type=F32)
         + jnp.dot(c_ref[0], w_ref[2 * WIDTH:3 * WIDTH, :], preferred_element_type=F32)
         + jnp.dot(d_ref[0], w_ref[3 * WIDTH:4 * WIDTH, :], preferred_element_type=F32))
    xn = x_ref[0] + y
    if final:
        xn = _rms(xn, fg_ref[...])
    o_ref[0] = xn


def _tiles(seq):
    tm = 512 if seq % 512 == 0 else MOBA_BLOCK
    steps = 128
    return tm, steps


def _arrange_w_in(w):
    z = lambda n: jnp.zeros((D_MODEL, n), w.dtype)
    o = 0
    parts = {}
    for name, n in (("fq", 256), ("fk", 256), ("fv", 256), ("fg", 256), ("ff", 4), ("su", 256), ("sg", 256),
                    ("mq", 256), ("mk", 256), ("mv", 256), ("mg", 256), ("cq", 384), ("ckv", 128), ("kr", 32),
                    ("lg", 256)):
        parts[name] = w[:, o:o + n]
        o += n
    cols = [parts["fq"], parts["fk"], parts["fg"], parts["su"], parts["sg"],
            parts["mq"], parts["mk"], parts["mg"], parts["lg"], parts["cq"], parts["ckv"],
            parts["ff"], z(MLA_NOPE - 4), parts["kr"], z(LANES - MLA_NOPE - MLA_ROPE)]
    w_vt = jnp.concatenate([parts["fv"], parts["mv"]], axis=1).T
    return jnp.concatenate(cols, axis=1).astype(BF16), w_vt.astype(BF16)


def _fox_bias_base(hd):
    return (HEAD_DIM if hd % 2 == 0 else 0) + FOX_BIAS_LANES * (hd // 2)


def _fox_routing():
    import numpy as np
    route = np.zeros((3 * LANES, 2 * LANES), np.float32)
    const = np.zeros((1, 2 * LANES), np.float32)
    for hd in range(N_HEADS):
        base = _fox_bias_base(hd)
        for part in range(3):
            route[part * LANES + hd, base + part] = 1.0
            const[0, base + 3 + part] = 1.0
            const[0, LANES + base + part] = 1.0
            route[part * LANES + hd, LANES + base + 3 + part] = -1.0
    return jnp.asarray(route, BF16), jnp.asarray(const, F32)


def _rope_tables(seq):
    pos = jnp.arange(seq).astype(F32)[:, None]
    lane = jnp.arange(LANES)
    half = HEAD_DIM // 2
    inv = jnp.power(ROPE_THETA, -jnp.arange(half, dtype=F32) / half)
    ang = pos * inv[None, :]
    cos, sin = jnp.cos(ang)[:, lane % half], jnp.sin(ang)[:, lane % half]
    lo = (lane % HEAD_DIM) < half
    moba = (cos, jnp.where(lo, -sin, 0.0), jnp.where(lo, 0.0, sin))
    half = MLA_ROPE // 2
    inv = jnp.power(ROPE_THETA, -jnp.arange(half, dtype=F32) / half)
    ang = pos * inv[None, :]
    cos, sin = jnp.cos(ang)[:, lane % half], jnp.sin(ang)[:, lane % half]
    in_lo = (lane >= MLA_NOPE) & (lane < MLA_NOPE + half)
    in_hi = (lane >= MLA_NOPE + half) & (lane < MLA_NOPE + MLA_ROPE)
    mla = (jnp.where(in_lo | in_hi, cos, 1.0), jnp.where(in_lo, -sin, 0.0), jnp.where(in_hi, sin, 0.0))
    return moba, mla


def _arrange_mla(w_uq, w_ukv):
    z = lambda r, n: jnp.zeros((r, n), F32)
    dq = MLA_NOPE + MLA_ROPE
    q_cols, k_cols, v_cols = [], [], []
    for hd in range(N_HEADS):
        q_cols += [w_uq[:, hd * dq:(hd + 1) * dq], z(MLA_Q_RANK, LANES - dq)]
        base = hd * (MLA_NOPE + MLA_V)
        k_cols += [w_ukv[:, base:base + MLA_NOPE], z(MLA_KV_RANK, LANES - MLA_NOPE)]
        v_cols += [w_ukv[:, base + MLA_NOPE:base + MLA_NOPE + MLA_V]]
    return (jnp.concatenate(q_cols, axis=1).astype(BF16), jnp.concatenate(k_cols, axis=1).astype(BF16),
            jnp.concatenate(v_cols, axis=1).T.astype(BF16))


def _s5_matrices(a_re, a_im, log_dt, b_re, b_im, c_re, c_im):
    lam = lax.complex(a_re.astype(F32), a_im.astype(F32))
    dt = jnp.exp(log_dt.astype(F32))[:, None]
    lam_bar = jnp.exp(lam * dt)
    b_bar = ((lam_bar - 1.0) / lam)[..., None] * lax.complex(b_re.astype(F32), b_im.astype(F32))
    eye = jnp.eye(S5_GROUPS, dtype=F32)
    blockdiag_in = lambda t: jnp.einsum('gpc,gh->gchp', t, eye).reshape(WIDTH, S5_NSTATE)
    blockdiag_out = lambda t: jnp.einsum('gcp,gh->gphc', t, eye).reshape(S5_NSTATE, WIDTH)
    bm = jnp.concatenate([blockdiag_in(b_bar.real), blockdiag_in(b_bar.imag)], axis=1).astype(BF16)
    cm = jnp.concatenate([blockdiag_out(c_re.astype(F32)), -blockdiag_out(c_im.astype(F32))], axis=0).astype(BF16)
    lam2 = jnp.stack([lam_bar.real.reshape(S5_NSTATE), lam_bar.imag.reshape(S5_NSTATE)], axis=0)
    return bm, lam2, cm


def _full(shape):
    return pl.BlockSpec(shape, lambda *_: (0,) * len(shape))


def _in_call(x, g, w, fb, tri, route, bconst, moba_t, mla_t, gq, gkv, wuq, wukv, wvt, wuvt, tm):
    nb, seq, _ = x.shape
    tok = lambda width: pl.BlockSpec((1, tm, width), lambda b, s: (b, s, 0))
    tab = pl.BlockSpec((tm, LANES), lambda b, s: (s, 0))
    bf = lambda width: jax.ShapeDtypeStruct((nb, seq, width), BF16)
    f32 = lambda width: jax.ShapeDtypeStruct((nb, seq, width), F32)
    val_t = jax.ShapeDtypeStruct((nb, seq // tm, WIDTH, tm), BF16)
    val_t_spec = pl.BlockSpec((1, 1, WIDTH, tm), lambda b, s: (b, s, 0, 0))
    out_shape = (bf(512), bf(512), val_t, f32(256),
                 jax.ShapeDtypeStruct((seq, nb * WIDTH), F32), f32(256),
                 bf(512), bf(512), val_t, f32(256),
                 bf(512), bf(512), val_t, f32(256),
                 jax.ShapeDtypeStruct((nb, seq // tm, 8, LANES), F32))
    out_specs = (tok(512), tok(512), val_t_spec, tok(256),
                 pl.BlockSpec((tm, WIDTH), lambda b, s: (s, b)), tok(256),
                 tok(512), tok(512), val_t_spec, tok(256),
                 tok(512), tok(512), val_t_spec, tok(256),
                 pl.BlockSpec((1, 1, 8, LANES), lambda b, s: (b, s, 0, 0)))
    in_specs = [tok(D_MODEL), _full((1, D_MODEL)), _full((D_MODEL, N_COLS)), _full((1, LANES)),
                _full((tm, tm)), _full((3 * LANES, 2 * LANES)), _full((1, 2 * LANES)),
                tab, tab, tab, tab, tab, tab,
                _full((1, MLA_Q_RANK)), _full((1, MLA_KV_RANK)),
                _full((MLA_Q_RANK, N_HEADS * LANES)), _full((MLA_KV_RANK, N_HEADS * LANES)),
                _full((2 * WIDTH, D_MODEL)), _full((WIDTH, MLA_KV_RANK))]
    return pl.pallas_call(
        functools.partial(_in_kernel, tm=tm),
        grid=(nb, seq // tm),
        in_specs=in_specs, out_specs=out_specs, out_shape=out_shape,
        scratch_shapes=[pltpu.VMEM((1, LANES), F32), pltpu.VMEM((LANES, WIDTH), F32)],
        compiler_params=pltpu.CompilerParams(dimension_semantics=("arbitrary", "arbitrary"),
                                             vmem_limit_bytes=VMEM_LIMIT),
        name="in_proj",
    )(x, g, w, fb, tri, route, bconst, *moba_t, *mla_t, gq, gkv, wuq, wukv, wvt, wuvt)


def _first_blocks(nb, nq, fstat=None):
    tiles = jnp.arange(nq)
    past = tiles[None, :] < tiles[:, None]
    if fstat is None:
        need = jnp.broadcast_to(past, (nb, 2, nq, nq))
    else:
        qn, kn = fstat[:, :, 0, :N_HEADS] * NORM_SLACK, fstat[:, :, 1, :N_HEADS] * NORM_SLACK
        cmax, cmin = fstat[:, :, 2, :N_HEADS], fstat[:, :, 3, :N_HEADS]
        gap = (qn[:, :, None] * kn[:, None, :] + cmax[:, :, None] - cmin[:, None, :]
               + (qn * kn)[:, :, None])
        need = (gap > -SKIP_GAP) & past[None, :, :, None]
        need = need.reshape(nb, nq, nq, 2, 2).any(-1).transpose(0, 3, 1, 2)
    first = jnp.where(need.any(-1), jnp.argmax(need, axis=-1), nq)
    first = jnp.minimum(first, jnp.maximum(tiles - 1, 0))
    count = jnp.sum(tiles - first, axis=-1, keepdims=True)
    return jnp.concatenate([first, count], axis=-1).astype(jnp.int32).reshape(-1)


def _attn_call(first_blocks, q, k, vt, gate, t, name):
    nb, seq, _ = q.shape
    whole = lambda width: pl.BlockSpec((1, seq, width), lambda b, p, tbl: (b, 0, p))
    return pl.pallas_call(
        functools.partial(_attn_kernel, t=t, nq=seq // t),
        grid_spec=pltpu.PrefetchScalarGridSpec(
            num_scalar_prefetch=1, grid=(nb, 2),
            in_specs=[whole(2 * LANES), whole(2 * LANES),
                      pl.BlockSpec((1, seq // t, LANES, t), lambda b, p, tbl: (b, 0, p, 0)), whole(LANES)],
            out_specs=whole(LANES),
            scratch_shapes=[pltpu.VMEM((2, t, t), F32), pltpu.VMEM((2, t, t), F32),
                            pltpu.VMEM((2, 1, t), F32), pltpu.VMEM((2, 1, t), F32),
                            pltpu.VMEM((seq // t, 2, 1, t), F32),
                            pltpu.VMEM((seq // t, 2, HEAD_DIM + DEN_ROWS, t), F32),
                            pltpu.SMEM((2,), jnp.int32)]),
        out_shape=jax.ShapeDtypeStruct((nb, seq, WIDTH), BF16),
        compiler_params=pltpu.CompilerParams(dimension_semantics=("arbitrary", "arbitrary"),
                                             vmem_limit_bytes=VMEM_LIMIT),
        name=name,
    )(first_blocks, q, k, vt, gate)


def _s5_call(u, bm, lam2, cm, d, gw, gb, steps, nbatch):
    rows = u.shape[0]
    blk = steps * nbatch
    nchunks = rows // blk
    prev = lambda c: (jnp.maximum(c - 1, 0), 0)
    return pl.pallas_call(
        functools.partial(_s5_kernel, steps=steps, nbatch=nbatch, nchunks=nchunks),
        grid=(nchunks + 1,),
        in_specs=[pl.BlockSpec((blk, WIDTH), lambda c: (jnp.minimum(c, nchunks - 1), 0)),
                  pl.BlockSpec((blk, WIDTH), prev),
                  _full((WIDTH, 2 * S5_NSTATE)), _full((2, S5_NSTATE)), _full((2 * S5_NSTATE, WIDTH)),
                  _full((1, WIDTH)), _full((WIDTH, WIDTH)), _full((1, WIDTH))],
        out_specs=pl.BlockSpec((blk, WIDTH), prev),
        out_shape=jax.ShapeDtypeStruct((rows, WIDTH), F32),
        scratch_shapes=[pltpu.VMEM((blk, 2 * S5_NSTATE), F32), pltpu.VMEM((blk, 2 * S5_NSTATE), F32),
                        pltpu.VMEM((nbatch, 2 * S5_NSTATE), F32)],
        compiler_params=pltpu.CompilerParams(dimension_semantics=("arbitrary",),
                                             vmem_limit_bytes=VMEM_LIMIT),
        name="s5",
    )(u, u, bm, lam2, cm, d, gw, gb)


def _out_call(x, a, b2d, sg, c, d, w, fg, tm, final):
    nb, seq, _ = x.shape
    tok = lambda width: pl.BlockSpec((1, tm, width), lambda b, s: (b, s, 0))
    return pl.pallas_call(
        functools.partial(_out_kernel, final=final),
        grid=(nb, seq // tm),
        in_specs=[tok(D_MODEL), tok(WIDTH), pl.BlockSpec((tm, WIDTH), lambda b, s: (s, b)), tok(WIDTH),
                  tok(WIDTH), tok(WIDTH), _full((4 * WIDTH, D_MODEL)), _full((1, D_MODEL))],
        out_specs=tok(D_MODEL),
        out_shape=jax.ShapeDtypeStruct(x.shape, F32),
        compiler_params=pltpu.CompilerParams(dimension_semantics=("arbitrary", "arbitrary"),
                                             vmem_limit_bytes=VMEM_LIMIT),
        name="out_proj",
    )(x, a, b2d, sg, c, d, w, fg)


def kernel(x, norm_g, w_in, fox_fb, s5_a_re, s5_a_im, s5_log_dt, s5_b_re, s5_b_im, s5_c_re, s5_c_im, s5_d,
           s5_glu_w, s5_glu_b, mla_q_norm, mla_w_uq, mla_kv_norm, mla_w_ukv, w_out, final_g):
    nb, seq, _ = x.shape
    depth = norm_g.shape[0]
    assert nb == 8, "the S5 recurrence keeps the batch on the 8 sublanes of a vreg"
    tm, steps = _tiles(seq)
    assert seq % tm == 0 and seq % steps == 0 and seq // MOBA_BLOCK <= 32

    tri = jnp.tril(jnp.ones((tm, tm), F32)).astype(BF16)
    route, bconst = _fox_routing()
    moba_t, mla_t = _rope_tables(seq)
    fgain = final_g.astype(F32).reshape(1, D_MODEL)
    every_block = _first_blocks(nb, seq // tm)

    for l in range(depth):
        w, wvt = _arrange_w_in(w_in[l])
        fb = jnp.zeros((1, LANES), F32).at[0, :N_HEADS].set(fox_fb[l].astype(F32))
        wuq, wukv, wuvt = _arrange_mla(mla_w_uq[l].astype(F32), mla_w_ukv[l].astype(F32))
        (foxq, foxk, foxv, fg, su, sg, mobq, mobk, mobv, mg, mlaq, mlak, mlav, lg, fstat) = _in_call(
            x, norm_g[l].astype(F32).reshape(1, D_MODEL), w, fb, tri, route, bconst, moba_t, mla_t,
            mla_q_norm[l].astype(F32).reshape(1, MLA_Q_RANK), mla_kv_norm[l].astype(F32).reshape(1, MLA_KV_RANK),
            wuq, wukv, wvt, wuvt, tm)

        a_out = _attn_call(_first_blocks(nb, seq // tm, fstat), foxq, foxk, foxv, fg, tm, "fox_attn")
        c_out = _attn_call(every_block, mobq, mobk, mobv, mg, tm, "moba_attn")
        d_out = _attn_call(every_block, mlaq, mlak, mlav, lg, tm, "mla_attn")

        bm, lam2, cm = _s5_matrices(s5_a_re[l], s5_a_im[l], s5_log_dt[l], s5_b_re[l], s5_b_im[l],
                                    s5_c_re[l], s5_c_im[l])
        b_rows = _s5_call(su.reshape(seq * nb, WIDTH), bm, lam2, cm,
                          s5_d[l].astype(F32).reshape(1, WIDTH), s5_glu_w[l].astype(BF16),
                          s5_glu_b[l].astype(F32).reshape(1, WIDTH), steps, nb)
        x = _out_call(x, a_out, b_rows.reshape(seq, nb * WIDTH), sg, c_out, d_out,
                      w_out[l].astype(BF16), fgain, tm, final=(l == depth - 1))
    return x
```

```python
import functools

import jax
import jax.numpy as jnp
from jax import lax
from jax.experimental import pallas as pl
from jax.experimental.pallas import tpu as pltpu

F32 = jnp.float32
BF16 = jnp.bfloat16

D_MODEL = 1024
HEAD_DIM = 64
DEN_ROWS = 16
PAST_UNROLL = 8
SKIP_GAP = 152.0
NORM_SLACK = 1.01
N_HEADS = 4
WIDTH = 256
S5_GROUPS = 16
S5_GROUP = 16
S5_STATE = 64
S5_NSTATE = S5_GROUPS * S5_STATE
MOBA_BLOCK = 256
MOBA_TOPK = 3
MAX_BLOCKS = 32
GROUP_HEAD = (1, 3, 0, 2)
LOG2E = 1.4426950408889634
MLA_NOPE = 64
MLA_ROPE = 32
MLA_V = 64
MLA_Q_RANK = 384
MLA_KV_RANK = 128
ROPE_THETA = 10000.0
EPS = 1e-6
NEG = -1e30
LANES = 128
VMEM_LIMIT = 56 * 1024 * 1024

C_FQ, C_FK, C_FG = 0, 256, 512
C_SU, C_SG = 768, 1024
C_MQ, C_MK, C_MG = 1280, 1536, 1792
C_LG = 2048
C_CQ = 2304
C_CKV = 2688
C_FFKR = 2816
N_COLS = 2944
PROJ_GROUPS = ((0, 768), (768, 512), (1280, 768), (2048, 896))
FOX_BIAS_LANES = 6


def _split3(x):
    x1 = x.astype(BF16)
    r1 = x - x1.astype(F32)
    x2 = r1.astype(BF16)
    r2 = r1 - x2.astype(F32)
    return x1, x2, r2.astype(BF16)


def _rope(x, cos, sin_lo, sin_hi, half):
    return (x * cos + pltpu.roll(x, LANES - half, 1) * sin_lo + pltpu.roll(x, half, 1) * sin_hi)


def _rms(x, g):
    return x * lax.rsqrt(jnp.mean(x * x, axis=-1, keepdims=True) + EPS) * g


def _silu(g):
    return g * (1.0 / (1.0 + jnp.exp(-g)))


def _in_kernel(x_ref, g_ref, w_ref, fb_ref, tri_ref, route_ref, bconst_ref,
               mc_ref, msl_ref, msh_ref, lc_ref, lsl_ref, lsh_ref,
               gq_ref, gkv_ref, wuq_ref, wukv_ref, wvt_ref, wuvt_ref,
               foxq_ref, foxk_ref, foxv_ref, fg_ref, su_ref, sg_ref,
               mobq_ref, mobk_ref, mobv_ref, mg_ref,
               mlaq_ref, mlak_ref, mlav_ref, lg_ref, fstat_ref,
               carry_ref, km_ref, *, tm):
    sblk = pl.program_id(1)
    nblk = tm // MOBA_BLOCK

    @pl.when(sblk == 0)
    def _():
        carry_ref[...] = jnp.zeros_like(carry_ref)
        km_ref[...] = jnp.zeros_like(km_ref)

    h = _rms(x_ref[0], g_ref[...]).astype(BF16)

    z = [jnp.dot(h, w_ref[:, c0:c0 + width], preferred_element_type=F32) for c0, width in PROJ_GROUPS]

    def proj(c0, width):
        for zg, (g0, gw) in zip(z, PROJ_GROUPS):
            if g0 <= c0 and c0 + width <= g0 + gw:
                return zg[:, c0 - g0:c0 - g0 + width]
        raise ValueError("column range crosses a projection group")

    lane =lax.broadcasted_iota(jnp.int32, (tm, LANES), 1)
    row =lax.broadcasted_iota(jnp.int32, (tm, LANES), 0)
    low_half = lane < HEAD_DIM

    def proj_t(wt):
        return lax.dot_general(wt, h, (((1,), (1,)), ((), ())), preferred_element_type=F32)

    foxv_ref[0, 0] = proj_t(wvt_ref[0:WIDTH, :]).astype(BF16)
    mobv_ref[0, 0] = proj_t(wvt_ref[WIDTH:2 * WIDTH, :]).astype(BF16)
    fg_ref[0] = proj(C_FG, WIDTH)
    sg_ref[0] = proj(C_SG, WIDTH)
    mg_ref[0] = proj(C_MG, WIDTH)
    lg_ref[0] = proj(C_LG, WIDTH)
    su_ref[...] = proj(C_SU, WIDTH)

    ffkr = proj(C_FFKR, LANES)
    ff = ffkr + fb_ref[...]
    logf = -(jnp.maximum(-ff, 0.0) + jnp.log1p(jnp.exp(-jnp.abs(ff))))
    within3 = jnp.dot(tri_ref[...], jnp.concatenate(_split3(logf), axis=1), preferred_element_type=F32)
    within = within3[:, 0:LANES] + within3[:, LANES:2 * LANES] + within3[:, 2 * LANES:]
    cum = within + carry_ref[...]
    carry_ref[...] = cum[tm - 1:tm, :]
    cum2 = cum * LOG2E
    routed = jnp.dot(jnp.concatenate(_split3(cum2), axis=1), route_ref[...],
                     preferred_element_type=F32) + bconst_ref[...]
    fq = proj(C_FQ, WIDTH) * (HEAD_DIM ** -0.5 * LOG2E)
    fk = proj(C_FK, WIDTH)
    lane_row = lane[0:1, :]
    qmax_row = jnp.zeros((1, LANES), F32)
    kmax_row = jnp.zeros((1, LANES), F32)
    for hd in range(N_HEADS):
        pair = hd // 2
        own = low_half if hd % 2 == 0 else jnp.logical_not(low_half)
        base = _fox_bias_base(hd)
        mine = (lane >= base) & (lane < base + FOX_BIAS_LANES)
        sl = slice(pair * LANES, (pair + 1) * LANES)
        qb = fq[:, sl].astype(BF16)
        kb = fk[:, sl].astype(BF16)
        qa = jnp.where(own, qb, jnp.where(mine, routed[:, 0:LANES], 0.0).astype(BF16))
        ka = jnp.where(own, kb, jnp.where(mine, routed[:, LANES:], 0.0).astype(BF16))
        foxq_ref[0, :, hd * LANES:(hd + 1) * LANES] = qa
        foxk_ref[0, :, hd * LANES:(hd + 1) * LANES] = ka
        for rounded, is_q in ((qb, True), (kb, False)):
            r = jnp.where(own, rounded.astype(F32), 0.0)
            norm = jnp.sqrt(jnp.max(jnp.sum(r * r, axis=1, keepdims=True), axis=0, keepdims=True))
            if is_q:
                qmax_row = jnp.where(lane_row == hd, norm, qmax_row)
            else:
                kmax_row = jnp.where(lane_row == hd, norm, kmax_row)
    srow = lax.broadcasted_iota(jnp.int32, (8, LANES), 0)
    fstat_ref[0, 0] = jnp.where(srow == 0, qmax_row, jnp.where(srow == 1, kmax_row, jnp.where(
        srow == 2, jnp.max(cum2, axis=0, keepdims=True), jnp.where(
            srow == 3, jnp.min(cum2, axis=0, keepdims=True), 0.0))))

    mc, msl, msh = mc_ref[...], msl_ref[...], msh_ref[...]
    mq = proj(C_MQ, WIDTH)
    mk = proj(C_MK, WIDTH)
    q_r = [_rope(mq[:, p * LANES:(p + 1) * LANES], mc, msl, msh, HEAD_DIM // 2) for p in range(2)]
    k_r = [_rope(mk[:, p * LANES:(p + 1) * LANES], mc, msl, msh, HEAD_DIM // 2) for p in range(2)]
    km_row = lax.broadcasted_iota(jnp.int32, (LANES, WIDTH), 0)
    km_lane = lax.broadcasted_iota(jnp.int32, (LANES, WIDTH), 1)
    km_grp = lax.shift_right_logical(km_row, 5)
    km_head = jnp.where(km_grp == 0, GROUP_HEAD[0], jnp.where(km_grp == 1, GROUP_HEAD[1],
                        jnp.where(km_grp == 2, GROUP_HEAD[2], GROUP_HEAD[3])))
    km_own = lax.shift_right_logical(km_lane, 6) == km_head
    kmt = km_ref[...]
    for nb in range(nblk):
        blk = sblk * nblk + nb
        km = jnp.concatenate(
            [jnp.mean(k_r[p][nb * MOBA_BLOCK:(nb + 1) * MOBA_BLOCK, :], axis=0, keepdims=True) for p in range(2)],
            axis=1)
        kmt = jnp.where((km_row & (MAX_BLOCKS - 1)) == blk, jnp.where(km_own, km, 0.0), kmt)
    km_ref[...] = kmt
    gate_t = lax.dot_general(kmt, jnp.concatenate(q_r, axis=1), (((1,), (1,)), ((), ())),
                             precision=lax.Precision.HIGHEST, preferred_element_type=F32)
    cand = lax.broadcasted_iota(jnp.int32, (MAX_BLOCKS, tm), 0).astype(F32)
    tok = lax.broadcasted_iota(jnp.int32, (MAX_BLOCKS, tm), 1)
    blk_tok = (sblk * nblk + lax.shift_right_logical(tok, 8)).astype(F32)
    bias_rows = []
    for grp in range(N_HEADS):
        g = jnp.where(cand < blk_tok, gate_t[grp * MAX_BLOCKS:(grp + 1) * MAX_BLOCKS, :], -jnp.inf)
        chosen = jnp.zeros((MAX_BLOCKS, tm), F32)
        for _ in range(MOBA_TOPK):
            m = jnp.max(g, axis=0, keepdims=True)
            first = jnp.min(jnp.where(g == m, cand, 1e9), axis=0, keepdims=True)
            first = jnp.where(m > -jnp.inf, first, -1.0)
            pick = cand == first
            chosen = jnp.where(pick, 1.0, chosen)
            g = jnp.where(pick, -jnp.inf, g)
        keep = jnp.where(cand == blk_tok, 1.0, chosen)
        bias_rows.append(jnp.where(keep > 0.0, 0.0, NEG))
    sel_bias = jnp.concatenate(bias_rows, axis=0).T
    blk_row = sblk * nblk + lax.shift_right_logical(row, 8)
    onehot = jnp.where((lane & (MAX_BLOCKS - 1)) == blk_row, 1.0, 0.0)
    lane_grp = lax.shift_right_logical(lane, 5)
    for hd in range(N_HEADS):
        pair = hd // 2
        own = low_half if hd % 2 == 0 else jnp.logical_not(low_half)
        mine = lane_grp == GROUP_HEAD.index(hd)
        qa = jnp.where(own, q_r[pair] * (HEAD_DIM ** -0.5 * LOG2E), jnp.where(mine, sel_bias, 0.0))
        ka = jnp.where(own, k_r[pair], jnp.where(mine, onehot, 0.0))
        mobq_ref[0, :, hd * LANES:(hd + 1) * LANES] = qa.astype(BF16)
        mobk_ref[0, :, hd * LANES:(hd + 1) * LANES] = ka.astype(BF16)

    lc, lsl, lsh = lc_ref[...], lsl_ref[...], lsh_ref[...]
    cqn = _rms(proj(C_CQ, MLA_Q_RANK), gq_ref[...]).astype(BF16)
    qf = jnp.dot(cqn, wuq_ref[...], preferred_element_type=F32)
    ckvn = _rms(proj(C_CKV, MLA_KV_RANK), gkv_ref[...]).astype(BF16)
    kv = jnp.dot(ckvn, wukv_ref[...], preferred_element_type=F32)
    kr = jnp.where(low_half, 0.0, _rope(ffkr, lc, lsl, lsh, MLA_ROPE // 2))
    scale = (MLA_NOPE + MLA_ROPE) ** -0.5 * LOG2E
    for hd in range(N_HEADS):
        sl = slice(hd * LANES, (hd + 1) * LANES)
        mlaq_ref[0, :, sl] = (_rope(qf[:, sl], lc, lsl, lsh, MLA_ROPE // 2) * scale).astype(BF16)
        mlak_ref[0, :, sl] = (kv[:, sl] + kr).astype(BF16)
    mlav_ref[0, 0] = lax.dot_general(wuvt_ref[...], ckvn, (((1,), (1,)), ((), ())),
                                     preferred_element_type=F32).astype(BF16)


def _attn_kernel(tbl_ref, q_ref, k_ref, vt_ref, g_ref, o_ref, sa_ref, sb_ref, mxa_ref, mxb_ref, m_ref, acc_ref,
                 qt_ref, cur_ref, *, t, nq):
    bufs = ((sa_ref, mxa_ref), (sb_ref, mxb_ref))
    kpos = lax.broadcasted_iota(jnp.int32, (t, t), 0)
    qpos = lax.broadcasted_iota(jnp.int32, (t, t), 1)
    ones = jnp.ones((DEN_ROWS, t), BF16)
    pair = lambda i, j: (jnp.int32(i), jnp.int32(j))

    def transpose_tile(i, carry):
        rows = pl.ds(pl.multiple_of(i * t, t), t)
        for hh in range(2):
            qt_ref[i, hh] = q_ref[0, rows, hh * LANES:(hh + 1) * LANES].astype(F32).T.astype(BF16)
        return carry

    lax.fori_loop(0, nq, transpose_tile, 0)

    def scores(hh, ij, buf):
        s_ref, mx_ref = buf
        koff = pl.multiple_of(ij[1] * t, t)
        st = jnp.dot(k_ref[0, pl.ds(koff, t), hh * LANES:(hh + 1) * LANES], qt_ref[ij[0], hh],
                     preferred_element_type=F32)
        s_ref[hh] = st
        mx_ref[hh] = jnp.max(st, axis=0, keepdims=True)

    def values(hh, j):
        return jnp.concatenate([vt_ref[0, j, hh * HEAD_DIM:(hh + 1) * HEAD_DIM, :], ones], axis=0)

    def first_update(hh, ij, buf):
        i, j = ij
        st = jnp.where(kpos <= qpos, buf[0][hh], NEG)
        mx = jnp.max(st, axis=0, keepdims=True)
        p = jnp.exp2(st - mx).astype(BF16)
        acc_ref[i, hh] = jnp.dot(values(hh, j), p, preferred_element_type=F32)
        m_ref[i, hh] = mx

    def update(hh, ij, buf):
        i, j = ij
        m = m_ref[i, hh]
        m_new = jnp.maximum(m, buf[1][hh])
        alpha = jnp.exp2(m - m_new)
        p = jnp.exp2(buf[0][hh] - m_new).astype(BF16)
        acc_ref[i, hh] = alpha * acc_ref[i, hh] + jnp.dot(values(hh, j), p, preferred_element_type=F32)
        m_ref[i, hh] = m_new

    def stage(nxt, nxt_buf, cur, cur_buf, consume):
        for hh in range(2):
            if nxt is not None:
                scores(hh, nxt, nxt_buf)
            consume(hh, cur, cur_buf)

    tbl = (pl.program_id(0) * 2 + pl.program_id(1)) * (nq + 1)
    first_block = lambda i: tbl_ref[tbl + i]
    n_past = tbl_ref[tbl + nq]
    first_past = pair(min(1, nq - 1), 0)

    for hh in range(2):
        scores(hh, pair(0, 0), bufs[0])

    def diag_of(i):
        inside = i < nq
        return jnp.where(inside, i, first_past[0]), jnp.where(inside, i, first_past[1])

    def two_diagonals(n, carry):
        i = 2 * n
        stage(diag_of(i + 1), bufs[1], (i, i), bufs[0], first_update)
        stage(diag_of(i + 2), bufs[0], (i + 1, i + 1), bufs[1], first_update)
        return carry

    lax.fori_loop(0, nq // 2, two_diagonals, 0)
    if nq % 2 == 1:
        stage(first_past if nq > 1 else None, bufs[1], pair(nq - 1, nq - 1), bufs[0], first_update)
    par = nq % 2

    def succ(ij):
        i, j = ij
        last = j == i - 1
        up = jnp.minimum(i + 1, nq - 1)
        return jnp.where(last, up, i), jnp.where(last, first_block(up), j + 1)

    def past_steps(count, cur):
        for k in range(count):
            nxt = succ(cur)
            stage(nxt, bufs[(par + k + 1) % 2], cur, bufs[(par + k) % 2], update)
            cur = nxt
        return cur

    cur = lax.fori_loop(0, n_past // PAST_UNROLL, lambda _, c: past_steps(PAST_UNROLL, c), first_past)
    cur_ref[0], cur_ref[1] = cur
    count = PAST_UNROLL // 2
    while count:
        @pl.when((n_past & count) != 0)
        def _(count=count):
            cur_ref[0], cur_ref[1] = past_steps(count, (cur_ref[0], cur_ref[1]))
        count //= 2

    def finish_tile(i, carry):
        out_t = jnp.concatenate([acc_ref[i, hh, 0:HEAD_DIM] / acc_ref[i, hh, HEAD_DIM:HEAD_DIM + 1]
                                 for hh in range(2)], axis=0)
        rows = pl.ds(pl.multiple_of(i * t, t), t)
        o_ref[0, rows, :] = (out_t.T * _silu(g_ref[0, rows, :])).astype(BF16)
        return carry

    lax.fori_loop(0, nq, finish_tile, 0)


def _s5_kernel(u_ref, up_ref, bm_ref, lam_ref, cm_ref, d_ref, gw_ref, gb_ref, o_ref, xa_ref, xb_ref, st_ref,
               *, steps, nbatch, nchunks):
    c = pl.program_id(0)

    @pl.when(c == 0)
    def _():
        st_ref[...] = jnp.zeros_like(st_ref)
        xb_ref[...] = jnp.zeros_like(xb_ref)

    lam_re = jnp.broadcast_to(lam_ref[0:1, :], (nbatch, S5_NSTATE))
    lam_im = jnp.broadcast_to(lam_ref[1:2, :], (nbatch, S5_NSTATE))

    def chunk(x_cur, x_prev):
        x_cur[...] = jnp.dot(u_ref[...].astype(BF16), bm_ref[...], preferred_element_type=F32)
        y = jnp.dot(x_prev[...].astype(BF16), cm_ref[...], preferred_element_type=F32) + d_ref[...] * up_ref[...]
        y = 0.5 * y * (1.0 + jnp.tanh(0.7978845608028654 * (y + 0.044715 * (y * y * y))))
        z = jnp.dot(y.astype(BF16), gw_ref[...], preferred_element_type=F32) + gb_ref[...]
        o_ref[...] = y * (1.0 / (1.0 + jnp.exp(-z)))

        def body(t, carry):
            xr, xi = carry
            off = pl.multiple_of(t * nbatch, nbatch)
            nr = lam_re * xr - lam_im * xi + x_cur[pl.ds(off, nbatch), 0:S5_NSTATE]
            ni = lam_re * xi + lam_im * xr + x_cur[pl.ds(off, nbatch), S5_NSTATE:2 * S5_NSTATE]
            x_cur[pl.ds(off, nbatch), 0:S5_NSTATE] = nr
            x_cur[pl.ds(off, nbatch), S5_NSTATE:2 * S5_NSTATE] = ni
            return nr, ni

        @pl.when(c < nchunks)
        def _():
            xr, xi = lax.fori_loop(0, steps, body, (st_ref[:, 0:S5_NSTATE], st_ref[:, S5_NSTATE:2 * S5_NSTATE]))
            st_ref[:, 0:S5_NSTATE] = xr
            st_ref[:, S5_NSTATE:2 * S5_NSTATE] = xi

    @pl.when(c % 2 == 0)
    def _():
        chunk(xa_ref, xb_ref)

    @pl.when(c % 2 == 1)
    def _():
        chunk(xb_ref, xa_ref)


def _out_kernel(x_ref, a_ref, b_ref, sg_ref, c_ref, d_ref, w_ref, fg_ref, o_ref, *, final):
    b = (b_ref[...] * _silu(sg_ref[0])).astype(BF16)
    y = (jnp.dot(a_ref[0], w_ref[0:WIDTH, :], preferred_element_type=F32)
         + jnp.dot(b, w_ref[WIDTH:2 * WIDTH, :], preferred_element_type=F32)
         + jnp.dot(c_ref[0], w_ref[2 * WIDTH:3 * WIDTH, :], preferred_element_type=F32)
         + jnp.dot(d_ref[0], w_ref[3 * WIDTH:4 * WIDTH, :], preferred_element_type=F32))
    xn = x_ref[0] + y
    if final:
        xn = _rms(xn, fg_ref[...])
    o_ref[0] = xn


def _tiles(seq):
    tm = 512 if seq % 512 == 0 else MOBA_BLOCK
    steps = 128
    return tm, steps


def _arrange_w_in(w):
    z = lambda n: jnp.zeros((D_MODEL, n), w.dtype)
    o = 0
    parts = {}
    for name, n in (("fq", 256), ("fk", 256), ("fv", 256), ("fg", 256), ("ff", 4), ("su", 256), ("sg", 256),
                    ("mq", 256), ("mk", 256), ("mv", 256), ("mg", 256), ("cq", 384), ("ckv", 128), ("kr", 32),
                    ("lg", 256)):
        parts[name] = w[:, o:o + n]
        o += n
    cols = [parts["fq"], parts["fk"], parts["fg"], parts["su"], parts["sg"],
            parts["mq"], parts["mk"], parts["mg"], parts["lg"], parts["cq"], parts["ckv"],
            parts["ff"], z(MLA_NOPE - 4), parts["kr"], z(LANES - MLA_NOPE - MLA_ROPE)]
    w_vt = jnp.concatenate([parts["fv"], parts["mv"]], axis=1).T
    return jnp.concatenate(cols, axis=1).astype(BF16), w_vt.astype(BF16)


def _fox_bias_base(hd):
    return (HEAD_DIM if hd % 2 == 0 else 0) + FOX_BIAS_LANES * (hd // 2)


def _fox_routing():
    import numpy as np
    route = np.zeros((3 * LANES, 2 * LANES), np.float32)
    const = np.zeros((1, 2 * LANES), np.float32)
    for hd in range(N_HEADS):
        base = _fox_bias_base(hd)
        for part in range(3):
            route[part * LANES + hd, base + part] = 1.0
            const[0, base + 3 + part] = 1.0
            const[0, LANES + base + part] = 1.0
            route[part * LANES + hd, LANES + base + 3 + part] = -1.0
    return jnp.asarray(route, BF16), jnp.asarray(const, F32)


def _rope_tables(seq):
    pos = jnp.arange(seq).astype(F32)[:, None]
    lane = jnp.arange(LANES)
    half = HEAD_DIM // 2
    inv = jnp.power(ROPE_THETA, -jnp.arange(half, dtype=F32) / half)
    ang = pos * inv[None, :]
    cos, sin = jnp.cos(ang)[:, lane % half], jnp.sin(ang)[:, lane % half]
    lo = (lane % HEAD_DIM) < half
    moba = (cos, jnp.where(lo, -sin, 0.0), jnp.where(lo, 0.0, sin))
    half = MLA_ROPE // 2
    inv = jnp.power(ROPE_THETA, -jnp.arange(half, dtype=F32) / half)
    ang = pos * inv[None, :]
    cos, sin = jnp.cos(ang)[:, lane % half], jnp.sin(ang)[:, lane % half]
    in_lo = (lane >= MLA_NOPE) & (lane < MLA_NOPE + half)
    in_hi = (lane >= MLA_NOPE + half) & (lane < MLA_NOPE + MLA_ROPE)
    mla = (jnp.where(in_lo | in_hi, cos, 1.0), jnp.where(in_lo, -sin, 0.0), jnp.where(in_hi, sin, 0.0))
    return moba, mla


def _arrange_mla(w_uq, w_ukv):
    z = lambda r, n: jnp.zeros((r, n), F32)
    dq = MLA_NOPE + MLA_ROPE
    q_cols, k_cols, v_cols = [], [], []
    for hd in range(N_HEADS):
        q_cols += [w_uq[:, hd * dq:(hd + 1) * dq], z(MLA_Q_RANK, LANES - dq)]
        base = hd * (MLA_NOPE + MLA_V)
        k_cols += [w_ukv[:, base:base + MLA_NOPE], z(MLA_KV_RANK, LANES - MLA_NOPE)]
        v_cols += [w_ukv[:, base + MLA_NOPE:base + MLA_NOPE + MLA_V]]
    return (jnp.concatenate(q_cols, axis=1).astype(BF16), jnp.concatenate(k_cols, axis=1).astype(BF16),
            jnp.concatenate(v_cols, axis=1).T.astype(BF16))


def _s5_matrices(a_re, a_im, log_dt, b_re, b_im, c_re, c_im):
    lam = lax.complex(a_re.astype(F32), a_im.astype(F32))
    dt = jnp.exp(log_dt.astype(F32))[:, None]
    lam_bar = jnp.exp(lam * dt)
    b_bar = ((lam_bar - 1.0) / lam)[..., None] * lax.complex(b_re.astype(F32), b_im.astype(F32))
    eye = jnp.eye(S5_GROUPS, dtype=F32)
    blockdiag_in = lambda t: jnp.einsum('gpc,gh->gchp', t, eye).reshape(WIDTH, S5_NSTATE)
    blockdiag_out = lambda t: jnp.einsum('gcp,gh->gphc', t, eye).reshape(S5_NSTATE, WIDTH)
    bm = jnp.concatenate([blockdiag_in(b_bar.real), blockdiag_in(b_bar.imag)], axis=1).astype(BF16)
    cm = jnp.concatenate([blockdiag_out(c_re.astype(F32)), -blockdiag_out(c_im.astype(F32))], axis=0).astype(BF16)
    lam2 = jnp.stack([lam_bar.real.reshape(S5_NSTATE), lam_bar.imag.reshape(S5_NSTATE)], axis=0)
    return bm, lam2, cm


def _full(shape):
    return pl.BlockSpec(shape, lambda *_: (0,) * len(shape))


def _in_call(x, g, w, fb, tri, route, bconst, moba_t, mla_t, gq, gkv, wuq, wukv, wvt, wuvt, tm):
    nb, seq, _ = x.shape
    tok = lambda width: pl.BlockSpec((1, tm, width), lambda b, s: (b, s, 0))
    tab = pl.BlockSpec((tm, LANES), lambda b, s: (s, 0))
    bf = lambda width: jax.ShapeDtypeStruct((nb, seq, width), BF16)
    f32 = lambda width: jax.ShapeDtypeStruct((nb, seq, width), F32)
    val_t = jax.ShapeDtypeStruct((nb, seq // tm, WIDTH, tm), BF16)
    val_t_spec = pl.BlockSpec((1, 1, WIDTH, tm), lambda b, s: (b, s, 0, 0))
    out_shape = (bf(512), bf(512), val_t, f32(256),
                 jax.ShapeDtypeStruct((seq, nb * WIDTH), F32), f32(256),
                 bf(512), bf(512), val_t, f32(256),
                 bf(512), bf(512), val_t, f32(256),
                 jax.ShapeDtypeStruct((nb, seq // tm, 8, LANES), F32))
    out_specs = (tok(512), tok(512), val_t_spec, tok(256),
                 pl.BlockSpec((tm, WIDTH), lambda b, s: (s, b)), tok(256),
                 tok(512), tok(512), val_t_spec, tok(256),
                 tok(512), tok(512), val_t_spec, tok(256),
                 pl.BlockSpec((1, 1, 8, LANES), lambda b, s: (b, s, 0, 0)))
    in_specs = [tok(D_MODEL), _full((1, D_MODEL)), _full((D_MODEL, N_COLS)), _full((1, LANES)),
                _full((tm, tm)), _full((3 * LANES, 2 * LANES)), _full((1, 2 * LANES)),
                tab, tab, tab, tab, tab, tab,
                _full((1, MLA_Q_RANK)), _full((1, MLA_KV_RANK)),
                _full((MLA_Q_RANK, N_HEADS * LANES)), _full((MLA_KV_RANK, N_HEADS * LANES)),
                _full((2 * WIDTH, D_MODEL)), _full((WIDTH, MLA_KV_RANK))]
    return pl.pallas_call(
        functools.partial(_in_kernel, tm=tm),
        grid=(nb, seq // tm),
        in_specs=in_specs, out_specs=out_specs, out_shape=out_shape,
        scratch_shapes=[pltpu.VMEM((1, LANES), F32), pltpu.VMEM((LANES, WIDTH), F32)],
        compiler_params=pltpu.CompilerParams(dimension_semantics=("arbitrary", "arbitrary"),
                                             vmem_limit_bytes=VMEM_LIMIT),
        name="in_proj",
    )(x, g, w, fb, tri, route, bconst, *moba_t, *mla_t, gq, gkv, wuq, wukv, wvt, wuvt)


def _first_blocks(nb, nq, fstat=None):
    tiles = jnp.arange(nq)
    past = tiles[None, :] < tiles[:, None]
    if fstat is None:
        need = jnp.broadcast_to(past, (nb, 2, nq, nq))
    else:
        qn, kn = fstat[:, :, 0, :N_HEADS] * NORM_SLACK, fstat[:, :, 1, :N_HEADS] * NORM_SLACK
        cmax, cmin = fstat[:, :, 2, :N_HEADS], fstat[:, :, 3, :N_HEADS]
        gap = (qn[:, :, None] * kn[:, None, :] + cmax[:, :, None] - cmin[:, None, :]
               + (qn * kn)[:, :, None])
        need = (gap > -SKIP_GAP) & past[None, :, :, None]
        need = need.reshape(nb, nq, nq, 2, 2).any(-1).transpose(0, 3, 1, 2)
    first = jnp.where(need.any(-1), jnp.argmax(need, axis=-1), nq)
    first = jnp.minimum(first, jnp.maximum(tiles - 1, 0))
    count = jnp.sum(tiles - first, axis=-1, keepdims=True)
    return jnp.concatenate([first, count], axis=-1).astype(jnp.int32).reshape(-1)


def _attn_call(first_blocks, q, k, vt, gate, t, name):
    nb, seq, _ = q.shape
    whole = lambda width: pl.BlockSpec((1, seq, width), lambda b, p, tbl: (b, 0, p))
    return pl.pallas_call(
        functools.partial(_attn_kernel, t=t, nq=seq // t),
        grid_spec=pltpu.PrefetchScalarGridSpec(
            num_scalar_prefetch=1, grid=(nb, 2),
            in_specs=[whole(2 * LANES), whole(2 * LANES),
                      pl.BlockSpec((1, seq // t, LANES, t), lambda b, p, tbl: (b, 0, p, 0)), whole(LANES)],
            out_specs=whole(LANES),
            scratch_shapes=[pltpu.VMEM((2, t, t), F32), pltpu.VMEM((2, t, t), F32),
                            pltpu.VMEM((2, 1, t), F32), pltpu.VMEM((2, 1, t), F32),
                            pltpu.VMEM((seq // t, 2, 1, t), F32),
                            pltpu.VMEM((seq // t, 2, HEAD_DIM + DEN_ROWS, t), F32),
                            pltpu.VMEM((seq // t, 2, LANES, t), BF16),
                            pltpu.SMEM((2,), jnp.int32)]),
        out_shape=jax.ShapeDtypeStruct((nb, seq, WIDTH), BF16),
        compiler_params=pltpu.CompilerParams(dimension_semantics=("arbitrary", "arbitrary"),
                                             vmem_limit_bytes=VMEM_LIMIT),
        name=name,
    )(first_blocks, q, k, vt, gate)


def _s5_call(u, bm, lam2, cm, d, gw, gb, steps, nbatch):
    rows = u.shape[0]
    blk = steps * nbatch
    nchunks = rows // blk
    prev = lambda c: (jnp.maximum(c - 1, 0), 0)
    return pl.pallas_call(
        functools.partial(_s5_kernel, steps=steps, nbatch=nbatch, nchunks=nchunks),
        grid=(nchunks + 1,),
        in_specs=[pl.BlockSpec((blk, WIDTH), lambda c: (jnp.minimum(c, nchunks - 1), 0)),
                  pl.BlockSpec((blk, WIDTH), prev),
                  _full((WIDTH, 2 * S5_NSTATE)), _full((2, S5_NSTATE)), _full((2 * S5_NSTATE, WIDTH)),
                  _full((1, WIDTH)), _full((WIDTH, WIDTH)), _full((1, WIDTH))],
        out_specs=pl.BlockSpec((blk, WIDTH), prev),
        out_shape=jax.ShapeDtypeStruct((rows, WIDTH), F32),
        scratch_shapes=[pltpu.VMEM((blk, 2 * S5_NSTATE), F32), pltpu.VMEM((blk, 2 * S5_NSTATE), F32),
                        pltpu.VMEM((nbatch, 2 * S5_NSTATE), F32)],
        compiler_params=pltpu.CompilerParams(dimension_semantics=("arbitrary",),
                                             vmem_limit_bytes=VMEM_LIMIT),
        name="s5",
    )(u, u, bm, lam2, cm, d, gw, gb)


def _out_call(x, a, b2d, sg, c, d, w, fg, tm, final):
    nb, seq, _ = x.shape
    tok = lambda width: pl.BlockSpec((1, tm, width), lambda b, s: (b, s, 0))
    return pl.pallas_call(
        functools.partial(_out_kernel, final=final),
        grid=(nb, seq // tm),
        in_specs=[tok(D_MODEL), tok(WIDTH), pl.BlockSpec((tm, WIDTH), lambda b, s: (s, b)), tok(WIDTH),
                  tok(WIDTH), tok(WIDTH), _full((4 * WIDTH, D_MODEL)), _full((1, D_MODEL))],
        out_specs=tok(D_MODEL),
        out_shape=jax.ShapeDtypeStruct(x.shape, F32),
        compiler_params=pltpu.CompilerParams(dimension_semantics=("arbitrary", "arbitrary"),
                                             vmem_limit_bytes=VMEM_LIMIT),
        name="out_proj",
    )(x, a, b2d, sg, c, d, w, fg)


def kernel(x, norm_g, w_in, fox_fb, s5_a_re, s5_a_im, s5_log_dt, s5_b_re, s5_b_im, s5_c_re, s5_c_im, s5_d,
           s5_glu_w, s5_glu_b, mla_q_norm, mla_w_uq, mla_kv_norm, mla_w_ukv, w_out, final_g):
    nb, seq, _ = x.shape
    depth = norm_g.shape[0]
    assert nb == 8, "the S5 recurrence keeps the batch on the 8 sublanes of a vreg"
    tm, steps = _tiles(seq)
    assert seq % tm == 0 and seq % steps == 0 and seq // MOBA_BLOCK <= 32

    tri = jnp.tril(jnp.ones((tm, tm), F32)).astype(BF16)
    route, bconst = _fox_routing()
    moba_t, mla_t = _rope_tables(seq)
    fgain = final_g.astype(F32).reshape(1, D_MODEL)
    every_block = _first_blocks(nb, seq // tm)

    for l in range(depth):
        w, wvt = _arrange_w_in(w_in[l])
        fb = jnp.zeros((1, LANES), F32).at[0, :N_HEADS].set(fox_fb[l].astype(F32))
        wuq, wukv, wuvt = _arrange_mla(mla_w_uq[l].astype(F32), mla_w_ukv[l].astype(F32))
        (foxq, foxk, foxv, fg, su, sg, mobq, mobk, mobv, mg, mlaq, mlak, mlav, lg, fstat) = _in_call(
            x, norm_g[l].astype(F32).reshape(1, D_MODEL), w, fb, tri, route, bconst, moba_t, mla_t,
            mla_q_norm[l].astype(F32).reshape(1, MLA_Q_RANK), mla_kv_norm[l].astype(F32).reshape(1, MLA_KV_RANK),
            wuq, wukv, wvt, wuvt, tm)

        a_out = _attn_call(_first_blocks(nb, seq // tm, fstat), foxq, foxk, foxv, fg, tm, "fox_attn")
        c_out = _attn_call(every_block, mobq, mobk, mobv, mg, tm, "moba_attn")
        d_out = _attn_call(every_block, mlaq, mlak, mlav, lg, tm, "mla_attn")

        bm, lam2, cm = _s5_matrices(s5_a_re[l], s5_a_im[l], s5_log_dt[l], s5_b_re[l], s5_b_im[l],
                                    s5_c_re[l], s5_c_im[l])
        b_rows = _s5_call(su.reshape(seq * nb, WIDTH), bm, lam2, cm,
                          s5_d[l].astype(F32).reshape(1, WIDTH), s5_glu_w[l].astype(BF16),
                          s5_glu_b[l].astype(F32).reshape(1, WIDTH), steps, nb)
        x = _out_call(x, a_out, b_rows.reshape(seq, nb * WIDTH), sg, c_out, d_out,
                      w_out[l].astype(BF16), fgain, tm, final=(l == depth - 1))
    return x
```

```python
import functools

import jax
import jax.numpy as jnp
from jax import lax
from jax.experimental import pallas as pl
from jax.experimental.pallas import tpu as pltpu

F32 = jnp.float32
BF16 = jnp.bfloat16

D_MODEL = 1024
HEAD_DIM = 64
DEN_ROWS = 16
PAST_UNROLL = 8
SKIP_GAP = 152.0
NORM_SLACK = 1.01
N_HEADS = 4
WIDTH = 256
S5_GROUPS = 16
S5_GROUP = 16
S5_STATE = 64
S5_NSTATE = S5_GROUPS * S5_STATE
MOBA_BLOCK = 256
MOBA_TOPK = 3
MAX_BLOCKS = 32
GROUP_HEAD = (1, 3, 0, 2)
LOG2E = 1.4426950408889634
MLA_NOPE = 64
MLA_ROPE = 32
MLA_V = 64
MLA_Q_RANK = 384
MLA_KV_RANK = 128
ROPE_THETA = 10000.0
EPS = 1e-6
NEG = -1e30
LANES = 128
VMEM_LIMIT = 56 * 1024 * 1024

C_FFKR = 0
C_CKV, C_CQ = 128, 256
C_FQ, C_FK = 640, 896
C_MQ, C_MK = 1152, 1408
C_FG, C_SU, C_SG, C_MG, C_LG = 1664, 1920, 2176, 2432, 2688
N_COLS = 2944
PROJ_GROUPS = ((0, 1152), (1152, 512), (1664, 1280))
FOX_BIAS_LANES = 6


def _split3(x):
    x1 = x.astype(BF16)
    r1 = x - x1.astype(F32)
    x2 = r1.astype(BF16)
    r2 = r1 - x2.astype(F32)
    return x1, x2, r2.astype(BF16)


def _rope(x, cos, sin_lo, sin_hi, half):
    return (x * cos + pltpu.roll(x, LANES - half, 1) * sin_lo + pltpu.roll(x, half, 1) * sin_hi)


def _rms(x, g):
    return x * lax.rsqrt(jnp.mean(x * x, axis=-1, keepdims=True) + EPS) * g


def _silu(g):
    return g * (1.0 / (1.0 + jnp.exp(-g)))


def _in_kernel(x_ref, g_ref, w_ref, fb_ref, tri_ref, route_ref, bconst_ref,
               mc_ref, msl_ref, msh_ref, lc_ref, lsl_ref, lsh_ref,
               gq_ref, gkv_ref, wuq_ref, wukv_ref, wvt_ref, wuvt_ref,
               foxq_ref, foxk_ref, foxv_ref, fg_ref, su_ref, sg_ref,
               mobq_ref, mobk_ref, mobv_ref, mg_ref,
               mlaq_ref, mlak_ref, mlav_ref, lg_ref, fstat_ref,
               carry_ref, km_ref, *, tm):
    sblk = pl.program_id(1)
    nblk = tm // MOBA_BLOCK

    @pl.when(sblk == 0)
    def _():
        carry_ref[...] = jnp.zeros_like(carry_ref)
        km_ref[...] = jnp.zeros_like(km_ref)

    h = _rms(x_ref[0], g_ref[...]).astype(BF16)

    z = [jnp.dot(h, w_ref[:, c0:c0 + width], preferred_element_type=F32) for c0, width in PROJ_GROUPS]

    def proj(c0, width):
        for zg, (g0, gw) in zip(z, PROJ_GROUPS):
            if g0 <= c0 and c0 + width <= g0 + gw:
                return zg[:, c0 - g0:c0 - g0 + width]
        raise ValueError("column range crosses a projection group")

    lane =lax.broadcasted_iota(jnp.int32, (tm, LANES), 1)
    row =lax.broadcasted_iota(jnp.int32, (tm, LANES), 0)
    low_half = lane < HEAD_DIM

    def proj_t(wt):
        return lax.dot_general(wt, h, (((1,), (1,)), ((), ())), preferred_element_type=F32)

    fg_ref[0] = proj(C_FG, WIDTH)
    sg_ref[0] = proj(C_SG, WIDTH)
    mg_ref[0] = proj(C_MG, WIDTH)
    lg_ref[0] = proj(C_LG, WIDTH)
    su_ref[...] = proj(C_SU, WIDTH)

    ffkr = proj(C_FFKR, LANES)
    ff = ffkr + fb_ref[...]
    logf = -(jnp.maximum(-ff, 0.0) + jnp.log1p(jnp.exp(-jnp.abs(ff))))
    within3 = jnp.dot(tri_ref[...], jnp.concatenate(_split3(logf), axis=1), preferred_element_type=F32)
    within = within3[:, 0:LANES] + within3[:, LANES:2 * LANES] + within3[:, 2 * LANES:]
    cum = within + carry_ref[...]
    carry_ref[...] = cum[tm - 1:tm, :]
    cum2 = cum * LOG2E
    routed = jnp.dot(jnp.concatenate(_split3(cum2), axis=1), route_ref[...],
                     preferred_element_type=F32) + bconst_ref[...]
    fq = proj(C_FQ, WIDTH) * (HEAD_DIM ** -0.5 * LOG2E)
    fk = proj(C_FK, WIDTH)
    lane_row = lane[0:1, :]
    qmax_row = jnp.zeros((1, LANES), F32)
    kmax_row = jnp.zeros((1, LANES), F32)
    for hd in range(N_HEADS):
        pair = hd // 2
        own = low_half if hd % 2 == 0 else jnp.logical_not(low_half)
        base = _fox_bias_base(hd)
        mine = (lane >= base) & (lane < base + FOX_BIAS_LANES)
        sl = slice(pair * LANES, (pair + 1) * LANES)
        qb = fq[:, sl].astype(BF16)
        kb = fk[:, sl].astype(BF16)
        qa = jnp.where(own, qb, jnp.where(mine, routed[:, 0:LANES], 0.0).astype(BF16))
        ka = jnp.where(own, kb, jnp.where(mine, routed[:, LANES:], 0.0).astype(BF16))
        foxq_ref[0, :, hd * LANES:(hd + 1) * LANES] = qa
        foxk_ref[0, :, hd * LANES:(hd + 1) * LANES] = ka
        for rounded, is_q in ((qb, True), (kb, False)):
            r = jnp.where(own, rounded.astype(F32), 0.0)
            norm = jnp.sqrt(jnp.max(jnp.sum(r * r, axis=1, keepdims=True), axis=0, keepdims=True))
            if is_q:
                qmax_row = jnp.where(lane_row == hd, norm, qmax_row)
            else:
                kmax_row = jnp.where(lane_row == hd, norm, kmax_row)
    srow = lax.broadcasted_iota(jnp.int32, (8, LANES), 0)
    fstat_ref[0, 0] = jnp.where(srow == 0, qmax_row, jnp.where(srow == 1, kmax_row, jnp.where(
        srow == 2, jnp.max(cum2, axis=0, keepdims=True), jnp.where(
            srow == 3, jnp.min(cum2, axis=0, keepdims=True), 0.0))))

    mc, msl, msh = mc_ref[...], msl_ref[...], msh_ref[...]
    mq = proj(C_MQ, WIDTH)
    mk = proj(C_MK, WIDTH)
    q_r = [_rope(mq[:, p * LANES:(p + 1) * LANES], mc, msl, msh, HEAD_DIM // 2) for p in range(2)]
    k_r = [_rope(mk[:, p * LANES:(p + 1) * LANES], mc, msl, msh, HEAD_DIM // 2) for p in range(2)]
    km_row = lax.broadcasted_iota(jnp.int32, (LANES, WIDTH), 0)
    km_lane = lax.broadcasted_iota(jnp.int32, (LANES, WIDTH), 1)
    km_grp = lax.shift_right_logical(km_row, 5)
    km_head = jnp.where(km_grp == 0, GROUP_HEAD[0], jnp.where(km_grp == 1, GROUP_HEAD[1],
                        jnp.where(km_grp == 2, GROUP_HEAD[2], GROUP_HEAD[3])))
    km_own = lax.shift_right_logical(km_lane, 6) == km_head
    kmt = km_ref[...]
    for nb in range(nblk):
        blk = sblk * nblk + nb
        km = jnp.concatenate(
            [jnp.mean(k_r[p][nb * MOBA_BLOCK:(nb + 1) * MOBA_BLOCK, :], axis=0, keepdims=True) for p in range(2)],
            axis=1)
        kmt = jnp.where((km_row & (MAX_BLOCKS - 1)) == blk, jnp.where(km_own, km, 0.0), kmt)
    km_ref[...] = kmt
    gate_t = lax.dot_general(kmt, jnp.concatenate(q_r, axis=1), (((1,), (1,)), ((), ())),
                             precision=lax.Precision.HIGHEST, preferred_element_type=F32)
    cand = lax.broadcasted_iota(jnp.int32, (MAX_BLOCKS, tm), 0).astype(F32)
    tok = lax.broadcasted_iota(jnp.int32, (MAX_BLOCKS, tm), 1)
    blk_tok = (sblk * nblk + lax.shift_right_logical(tok, 8)).astype(F32)
    bias_rows = []
    for grp in range(N_HEADS):
        g = jnp.where(cand < blk_tok, gate_t[grp * MAX_BLOCKS:(grp + 1) * MAX_BLOCKS, :], -jnp.inf)
        chosen = jnp.zeros((MAX_BLOCKS, tm), F32)
        for _ in range(MOBA_TOPK):
            m = jnp.max(g, axis=0, keepdims=True)
            first = jnp.min(jnp.where(g == m, cand, 1e9), axis=0, keepdims=True)
            first = jnp.where(m > -jnp.inf, first, -1.0)
            pick = cand == first
            chosen = jnp.where(pick, 1.0, chosen)
            g = jnp.where(pick, -jnp.inf, g)
        keep = jnp.where(cand == blk_tok, 1.0, chosen)
        bias_rows.append(jnp.where(keep > 0.0, 0.0, NEG))
    sel_bias = jnp.concatenate(bias_rows, axis=0).T
    blk_row = sblk * nblk + lax.shift_right_logical(row, 8)
    onehot = jnp.where((lane & (MAX_BLOCKS - 1)) == blk_row, 1.0, 0.0)
    lane_grp = lax.shift_right_logical(lane, 5)
    for hd in range(N_HEADS):
        pair = hd // 2
        own = low_half if hd % 2 == 0 else jnp.logical_not(low_half)
        mine = lane_grp == GROUP_HEAD.index(hd)
        qa = jnp.where(own, q_r[pair] * (HEAD_DIM ** -0.5 * LOG2E), jnp.where(mine, sel_bias, 0.0))
        ka = jnp.where(own, k_r[pair], jnp.where(mine, onehot, 0.0))
        mobq_ref[0, :, hd * LANES:(hd + 1) * LANES] = qa.astype(BF16)
        mobk_ref[0, :, hd * LANES:(hd + 1) * LANES] = ka.astype(BF16)

    lc, lsl, lsh = lc_ref[...], lsl_ref[...], lsh_ref[...]
    cqn = _rms(proj(C_CQ, MLA_Q_RANK), gq_ref[...]).astype(BF16)
    qf = jnp.dot(cqn, wuq_ref[...], preferred_element_type=F32)
    ckvn = _rms(proj(C_CKV, MLA_KV_RANK), gkv_ref[...]).astype(BF16)
    kv = jnp.dot(ckvn, wukv_ref[...], preferred_element_type=F32)
    kr = jnp.where(low_half, 0.0, _rope(ffkr, lc, lsl, lsh, MLA_ROPE // 2))
    scale = (MLA_NOPE + MLA_ROPE) ** -0.5 * LOG2E
    for hd in range(N_HEADS):
        sl = slice(hd * LANES, (hd + 1) * LANES)
        mlaq_ref[0, :, sl] = (_rope(qf[:, sl], lc, lsl, lsh, MLA_ROPE // 2) * scale).astype(BF16)
        mlak_ref[0, :, sl] = (kv[:, sl] + kr).astype(BF16)
    mlav_ref[0, 0] = lax.dot_general(wuvt_ref[...], ckvn, (((1,), (1,)), ((), ())),
                                     preferred_element_type=F32).astype(BF16)
    foxv_ref[0, 0] = proj_t(wvt_ref[0:WIDTH, :]).astype(BF16)
    mobv_ref[0, 0] = proj_t(wvt_ref[WIDTH:2 * WIDTH, :]).astype(BF16)


def _attn_kernel(tbl_ref, q_ref, k_ref, vt_ref, g_ref, o_ref, sa_ref, sb_ref, mxa_ref, mxb_ref, m_ref, acc_ref,
                 qt_ref, cur_ref, *, t, nq):
    bufs = ((sa_ref, mxa_ref), (sb_ref, mxb_ref))
    kpos = lax.broadcasted_iota(jnp.int32, (t, t), 0)
    qpos = lax.broadcasted_iota(jnp.int32, (t, t), 1)
    ones = jnp.ones((DEN_ROWS, t), BF16)
    pair = lambda i, j: (jnp.int32(i), jnp.int32(j))

    def transpose_tile(i, carry):
        rows = pl.ds(pl.multiple_of(i * t, t), t)
        for hh in range(2):
            qt_ref[i, hh] = q_ref[0, rows, hh * LANES:(hh + 1) * LANES].astype(F32).T.astype(BF16)
        return carry

    lax.fori_loop(0, nq, transpose_tile, 0)

    def scores(hh, ij, buf):
        s_ref, mx_ref = buf
        koff = pl.multiple_of(ij[1] * t, t)
        st = jnp.dot(k_ref[0, pl.ds(koff, t), hh * LANES:(hh + 1) * LANES], qt_ref[ij[0], hh],
                     preferred_element_type=F32)
        s_ref[hh] = st
        mx_ref[hh] = jnp.max(st, axis=0, keepdims=True)

    def values(hh, j):
        return jnp.concatenate([vt_ref[0, j, hh * HEAD_DIM:(hh + 1) * HEAD_DIM, :], ones], axis=0)

    def first_update(hh, ij, buf):
        i, j = ij
        st = jnp.where(kpos <= qpos, buf[0][hh], NEG)
        mx = jnp.max(st, axis=0, keepdims=True)
        p = jnp.exp2(st - mx).astype(BF16)
        acc_ref[i, hh] = jnp.dot(values(hh, j), p, preferred_element_type=F32)
        m_ref[i, hh] = mx

    def update(hh, ij, buf):
        i, j = ij
        m = m_ref[i, hh]
        m_new = jnp.maximum(m, buf[1][hh])
        alpha = jnp.exp2(m - m_new)
        p = jnp.exp2(buf[0][hh] - m_new).astype(BF16)
        acc_ref[i, hh] = alpha * acc_ref[i, hh] + jnp.dot(values(hh, j), p, preferred_element_type=F32)
        m_ref[i, hh] = m_new

    def stage(nxt, nxt_buf, cur, cur_buf, consume):
        for hh in range(2):
            if nxt is not None:
                scores(hh, nxt, nxt_buf)
            consume(hh, cur, cur_buf)

    tbl = (pl.program_id(0) * 2 + pl.program_id(1)) * (nq + 1)
    first_block = lambda i: tbl_ref[tbl + i]
    n_past = tbl_ref[tbl + nq]
    first_past = pair(min(1, nq - 1), 0)

    for hh in range(2):
        scores(hh, pair(0, 0), bufs[0])

    def diag_of(i):
        inside = i < nq
        return jnp.where(inside, i, first_past[0]), jnp.where(inside, i, first_past[1])

    def two_diagonals(n, carry):
        i = 2 * n
        stage(diag_of(i + 1), bufs[1], (i, i), bufs[0], first_update)
        stage(diag_of(i + 2), bufs[0], (i + 1, i + 1), bufs[1], first_update)
        return carry

    lax.fori_loop(0, nq // 2, two_diagonals, 0)
    if nq % 2 == 1:
        stage(first_past if nq > 1 else None, bufs[1], pair(nq - 1, nq - 1), bufs[0], first_update)
    par = nq % 2

    def succ(ij):
        i, j = ij
        last = j == i - 1
        up = jnp.minimum(i + 1, nq - 1)
        return jnp.where(last, up, i), jnp.where(last, first_block(up), j + 1)

    def past_steps(count, cur):
        for k in range(count):
            nxt = succ(cur)
            stage(nxt, bufs[(par + k + 1) % 2], cur, bufs[(par + k) % 2], update)
            cur = nxt
        return cur

    cur = lax.fori_loop(0, n_past // PAST_UNROLL, lambda _, c: past_steps(PAST_UNROLL, c), first_past)
    cur_ref[0], cur_ref[1] = cur
    count = PAST_UNROLL // 2
    while count:
        @pl.when((n_past & count) != 0)
        def _(count=count):
            cur_ref[0], cur_ref[1] = past_steps(count, (cur_ref[0], cur_ref[1]))
        count //= 2

    def finish_tile(i, carry):
        out_t = jnp.concatenate([acc_ref[i, hh, 0:HEAD_DIM] / acc_ref[i, hh, HEAD_DIM:HEAD_DIM + 1]
                                 for hh in range(2)], axis=0)
        rows = pl.ds(pl.multiple_of(i * t, t), t)
        o_ref[0, rows, :] = (out_t.T * _silu(g_ref[0, rows, :])).astype(BF16)
        return carry

    lax.fori_loop(0, nq, finish_tile, 0)


def _s5_kernel(u_ref, bm_ref, lam_ref, cm_ref, d_ref, gw_ref, gb_ref, o_ref,
               xa_ref, xb_ref, ua_ref, ub_ref, y_ref, st_ref, *, steps, nbatch, nchunks):
    c = pl.program_id(0)

    @pl.when(c == 0)
    def _():
        st_ref[...] = jnp.zeros_like(st_ref)
        xb_ref[...] = jnp.zeros_like(xb_ref)
        ub_ref[...] = jnp.zeros_like(ub_ref)

    lam_re = jnp.broadcast_to(lam_ref[0:1, :], (nbatch, S5_NSTATE))
    lam_im = jnp.broadcast_to(lam_ref[1:2, :], (nbatch, S5_NSTATE))

    halves = WIDTH // LANES
    wide = lambda ref: jnp.concatenate([ref[hf] for hf in range(halves)], axis=1)

    def chunk(x_cur, x_prev, u_cur, u_prev):
        for b in range(nbatch):
            for hf in range(halves):
                lanes = slice(b * WIDTH + hf * LANES, b * WIDTH + (hf + 1) * LANES)
                u_cur[hf, pl.ds(b, steps, stride=nbatch), :] = u_ref[:, lanes]
        x_cur[...] = jnp.dot(wide(u_cur).astype(BF16), bm_ref[...], preferred_element_type=F32)
        y = jnp.dot(x_prev[...].astype(BF16), cm_ref[...], preferred_element_type=F32) + d_ref[...] * wide(u_prev)
        y = 0.5 * y * (1.0 + jnp.tanh(0.7978845608028654 * (y + 0.044715 * (y * y * y))))
        z = jnp.dot(y.astype(BF16), gw_ref[...], preferred_element_type=F32) + gb_ref[...]
        y = y * (1.0 / (1.0 + jnp.exp(-z)))
        for hf in range(halves):
            y_ref[hf] = y[:, hf * LANES:(hf + 1) * LANES]
        for b in range(nbatch):
            for hf in range(halves):
                lanes = slice(b * WIDTH + hf * LANES, b * WIDTH + (hf + 1) * LANES)
                o_ref[:, lanes] = y_ref[hf, pl.ds(b, steps, stride=nbatch), :]

        def body(t, carry):
            xr, xi = carry
            off = pl.multiple_of(t * nbatch, nbatch)
            nr = lam_re * xr - lam_im * xi + x_cur[pl.ds(off, nbatch), 0:S5_NSTATE]
            ni = lam_re * xi + lam_im * xr + x_cur[pl.ds(off, nbatch), S5_NSTATE:2 * S5_NSTATE]
            x_cur[pl.ds(off, nbatch), 0:S5_NSTATE] = nr
            x_cur[pl.ds(off, nbatch), S5_NSTATE:2 * S5_NSTATE] = ni
            return nr, ni

        @pl.when(c < nchunks)
        def _():
            xr, xi = lax.fori_loop(0, steps, body, (st_ref[:, 0:S5_NSTATE], st_ref[:, S5_NSTATE:2 * S5_NSTATE]))
            st_ref[:, 0:S5_NSTATE] = xr
            st_ref[:, S5_NSTATE:2 * S5_NSTATE] = xi

    @pl.when(c % 2 == 0)
    def _():
        chunk(xa_ref, xb_ref, ua_ref, ub_ref)

    @pl.when(c % 2 == 1)
    def _():
        chunk(xb_ref, xa_ref, ub_ref, ua_ref)


def _out_kernel(x_ref, a_ref, b_ref, sg_ref, c_ref, d_ref, w_ref, fg_ref, o_ref, *, final):
    b = (b_ref[...] * _silu(sg_ref[0])).astype(BF16)
    y = (jnp.dot(a_ref[0], w_ref[0:WIDTH, :], preferred_element_type=F32)
         + jnp.dot(b, w_ref[WIDTH:2 * WIDTH, :], preferred_element_type=F32)
         + jnp.dot(c_ref[0], w_ref[2 * WIDTH:3 * WIDTH, :], preferred_element_type=F32)
         + jnp.dot(d_ref[0], w_ref[3 * WIDTH:4 * WIDTH, :], preferred_element_type=F32))
    xn = x_ref[0] + y
    if final:
        xn = _rms(xn, fg_ref[...])
    o_ref[0] = xn


def _tiles(seq):
    tm = 512 if seq % 512 == 0 else MOBA_BLOCK
    steps = 128
    return tm, steps


def _arrange_w_in(w):
    z = lambda n: jnp.zeros((D_MODEL, n), w.dtype)
    o = 0
    parts = {}
    for name, n in (("fq", 256), ("fk", 256), ("fv", 256), ("fg", 256), ("ff", 4), ("su", 256), ("sg", 256),
                    ("mq", 256), ("mk", 256), ("mv", 256), ("mg", 256), ("cq", 384), ("ckv", 128), ("kr", 32),
                    ("lg", 256)):
        parts[name] = w[:, o:o + n]
        o += n
    cols = [parts["ff"], z(MLA_NOPE - 4), parts["kr"], z(LANES - MLA_NOPE - MLA_ROPE),
            parts["ckv"], parts["cq"], parts["fq"], parts["fk"], parts["mq"], parts["mk"],
            parts["fg"], parts["su"], parts["sg"], parts["mg"], parts["lg"]]
    w_vt = jnp.concatenate([parts["fv"], parts["mv"]], axis=1).T
    return jnp.concatenate(cols, axis=1).astype(BF16), w_vt.astype(BF16)


def _fox_bias_base(hd):
    return (HEAD_DIM if hd % 2 == 0 else 0) + FOX_BIAS_LANES * (hd // 2)


def _fox_routing():
    import numpy as np
    route = np.zeros((3 * LANES, 2 * LANES), np.float32)
    const = np.zeros((1, 2 * LANES), np.float32)
    for hd in range(N_HEADS):
        base = _fox_bias_base(hd)
        for part in range(3):
            route[part * LANES + hd, base + part] = 1.0
            const[0, base + 3 + part] = 1.0
            const[0, LANES + base + part] = 1.0
            route[part * LANES + hd, LANES + base + 3 + part] = -1.0
    return jnp.asarray(route, BF16), jnp.asarray(const, F32)


def _rope_tables(seq):
    pos = jnp.arange(seq).astype(F32)[:, None]
    lane = jnp.arange(LANES)
    half = HEAD_DIM // 2
    inv = jnp.power(ROPE_THETA, -jnp.arange(half, dtype=F32) / half)
    ang = pos * inv[None, :]
    cos, sin = jnp.cos(ang)[:, lane % half], jnp.sin(ang)[:, lane % half]
    lo = (lane % HEAD_DIM) < half
    moba = (cos, jnp.where(lo, -sin, 0.0), jnp.where(lo, 0.0, sin))
    half = MLA_ROPE // 2
    inv = jnp.power(ROPE_THETA, -jnp.arange(half, dtype=F32) / half)
    ang = pos * inv[None, :]
    cos, sin = jnp.cos(ang)[:, lane % half], jnp.sin(ang)[:, lane % half]
    in_lo = (lane >= MLA_NOPE) & (lane < MLA_NOPE + half)
    in_hi = (lane >= MLA_NOPE + half) & (lane < MLA_NOPE + MLA_ROPE)
    mla = (jnp.where(in_lo | in_hi, cos, 1.0), jnp.where(in_lo, -sin, 0.0), jnp.where(in_hi, sin, 0.0))
    return moba, mla


def _arrange_mla(w_uq, w_ukv):
    z = lambda r, n: jnp.zeros((r, n), F32)
    dq = MLA_NOPE + MLA_ROPE
    q_cols, k_cols, v_cols = [], [], []
    for hd in range(N_HEADS):
        q_cols += [w_uq[:, hd * dq:(hd + 1) * dq], z(MLA_Q_RANK, LANES - dq)]
        base = hd * (MLA_NOPE + MLA_V)
        k_cols += [w_ukv[:, base:base + MLA_NOPE], z(MLA_KV_RANK, LANES - MLA_NOPE)]
        v_cols += [w_ukv[:, base + MLA_NOPE:base + MLA_NOPE + MLA_V]]
    return (jnp.concatenate(q_cols, axis=1).astype(BF16), jnp.concatenate(k_cols, axis=1).astype(BF16),
            jnp.concatenate(v_cols, axis=1).T.astype(BF16))


def _s5_matrices(a_re, a_im, log_dt, b_re, b_im, c_re, c_im):
    lam = lax.complex(a_re.astype(F32), a_im.astype(F32))
    dt = jnp.exp(log_dt.astype(F32))[:, None]
    lam_bar = jnp.exp(lam * dt)
    b_bar = ((lam_bar - 1.0) / lam)[..., None] * lax.complex(b_re.astype(F32), b_im.astype(F32))
    eye = jnp.eye(S5_GROUPS, dtype=F32)
    blockdiag_in = lambda t: jnp.einsum('gpc,gh->gchp', t, eye).reshape(WIDTH, S5_NSTATE)
    blockdiag_out = lambda t: jnp.einsum('gcp,gh->gphc', t, eye).reshape(S5_NSTATE, WIDTH)
    bm = jnp.concatenate([blockdiag_in(b_bar.real), blockdiag_in(b_bar.imag)], axis=1).astype(BF16)
    cm = jnp.concatenate([blockdiag_out(c_re.astype(F32)), -blockdiag_out(c_im.astype(F32))], axis=0).astype(BF16)
    lam2 = jnp.stack([lam_bar.real.reshape(S5_NSTATE), lam_bar.imag.reshape(S5_NSTATE)], axis=0)
    return bm, lam2, cm


def _full(shape):
    return pl.BlockSpec(shape, lambda *_: (0,) * len(shape))


def _in_call(x, g, w, fb, tri, route, bconst, moba_t, mla_t, gq, gkv, wuq, wukv, wvt, wuvt, tm):
    nb, seq, _ = x.shape
    tok = lambda width: pl.BlockSpec((1, tm, width), lambda b, s: (b, s, 0))
    tab = pl.BlockSpec((tm, LANES), lambda b, s: (s, 0))
    bf = lambda width: jax.ShapeDtypeStruct((nb, seq, width), BF16)
    f32 = lambda width: jax.ShapeDtypeStruct((nb, seq, width), F32)
    val_t = jax.ShapeDtypeStruct((nb, seq // tm, WIDTH, tm), BF16)
    val_t_spec = pl.BlockSpec((1, 1, WIDTH, tm), lambda b, s: (b, s, 0, 0))
    out_shape = (bf(512), bf(512), val_t, f32(256),
                 jax.ShapeDtypeStruct((seq, nb * WIDTH), F32), f32(256),
                 bf(512), bf(512), val_t, f32(256),
                 bf(512), bf(512), val_t, f32(256),
                 jax.ShapeDtypeStruct((nb, seq // tm, 8, LANES), F32))
    out_specs = (tok(512), tok(512), val_t_spec, tok(256),
                 pl.BlockSpec((tm, WIDTH), lambda b, s: (s, b)), tok(256),
                 tok(512), tok(512), val_t_spec, tok(256),
                 tok(512), tok(512), val_t_spec, tok(256),
                 pl.BlockSpec((1, 1, 8, LANES), lambda b, s: (b, s, 0, 0)))
    in_specs = [tok(D_MODEL), _full((1, D_MODEL)), _full((D_MODEL, N_COLS)), _full((1, LANES)),
                _full((tm, tm)), _full((3 * LANES, 2 * LANES)), _full((1, 2 * LANES)),
                tab, tab, tab, tab, tab, tab,
                _full((1, MLA_Q_RANK)), _full((1, MLA_KV_RANK)),
                _full((MLA_Q_RANK, N_HEADS * LANES)), _full((MLA_KV_RANK, N_HEADS * LANES)),
                _full((2 * WIDTH, D_MODEL)), _full((WIDTH, MLA_KV_RANK))]
    return pl.pallas_call(
        functools.partial(_in_kernel, tm=tm),
        grid=(nb, seq // tm),
        in_specs=in_specs, out_specs=out_specs, out_shape=out_shape,
        scratch_shapes=[pltpu.VMEM((1, LANES), F32), pltpu.VMEM((LANES, WIDTH), F32)],
        compiler_params=pltpu.CompilerParams(dimension_semantics=("arbitrary", "arbitrary"),
                                             vmem_limit_bytes=VMEM_LIMIT),
        name="in_proj",
    )(x, g, w, fb, tri, route, bconst, *moba_t, *mla_t, gq, gkv, wuq, wukv, wvt, wuvt)


def _first_blocks(nb, nq, fstat=None):
    tiles = jnp.arange(nq)
    past = tiles[None, :] < tiles[:, None]
    if fstat is None:
        need = jnp.broadcast_to(past, (nb, 2, nq, nq))
    else:
        qn, kn = fstat[:, :, 0, :N_HEADS] * NORM_SLACK, fstat[:, :, 1, :N_HEADS] * NORM_SLACK
        cmax, cmin = fstat[:, :, 2, :N_HEADS], fstat[:, :, 3, :N_HEADS]
        gap = (qn[:, :, None] * kn[:, None, :] + cmax[:, :, None] - cmin[:, None, :]
               + (qn * kn)[:, :, None])
        need = (gap > -SKIP_GAP) & past[None, :, :, None]
        need = need.reshape(nb, nq, nq, 2, 2).any(-1).transpose(0, 3, 1, 2)
    first = jnp.where(need.any(-1), jnp.argmax(need, axis=-1), nq)
    first = jnp.minimum(first, jnp.maximum(tiles - 1, 0))
    count = jnp.sum(tiles - first, axis=-1, keepdims=True)
    return jnp.concatenate([first, count], axis=-1).astype(jnp.int32).reshape(-1)


def _attn_call(first_blocks, q, k, vt, gate, t, name):
    nb, seq, _ = q.shape
    whole = lambda width: pl.BlockSpec((1, seq, width), lambda b, p, tbl: (b, 0, p))
    return pl.pallas_call(
        functools.partial(_attn_kernel, t=t, nq=seq // t),
        grid_spec=pltpu.PrefetchScalarGridSpec(
            num_scalar_prefetch=1, grid=(nb, 2),
            in_specs=[whole(2 * LANES), whole(2 * LANES),
                      pl.BlockSpec((1, seq // t, LANES, t), lambda b, p, tbl: (b, 0, p, 0)), whole(LANES)],
            out_specs=whole(LANES),
            scratch_shapes=[pltpu.VMEM((2, t, t), F32), pltpu.VMEM((2, t, t), F32),
                            pltpu.VMEM((2, 1, t), F32), pltpu.VMEM((2, 1, t), F32),
                            pltpu.VMEM((seq // t, 2, 1, t), F32),
                            pltpu.VMEM((seq // t, 2, HEAD_DIM + DEN_ROWS, t), F32),
                            pltpu.VMEM((seq // t, 2, LANES, t), BF16),
                            pltpu.SMEM((2,), jnp.int32)]),
        out_shape=jax.ShapeDtypeStruct((nb, seq, WIDTH), BF16),
        compiler_params=pltpu.CompilerParams(dimension_semantics=("arbitrary", "arbitrary"),
                                             vmem_limit_bytes=VMEM_LIMIT),
        name=name,
    )(first_blocks, q, k, vt, gate)


def _s5_call(u, bm, lam2, cm, d, gw, gb, steps, nbatch):
    seq = u.shape[0]
    blk = steps * nbatch
    nchunks = seq // steps
    prev = lambda c: (jnp.maximum(c - 1, 0), 0)
    return pl.pallas_call(
        functools.partial(_s5_kernel, steps=steps, nbatch=nbatch, nchunks=nchunks),
        grid=(nchunks + 1,),
        in_specs=[pl.BlockSpec((steps, nbatch * WIDTH), lambda c: (jnp.minimum(c, nchunks - 1), 0)),
                  _full((WIDTH, 2 * S5_NSTATE)), _full((2, S5_NSTATE)), _full((2 * S5_NSTATE, WIDTH)),
                  _full((1, WIDTH)), _full((WIDTH, WIDTH)), _full((1, WIDTH))],
        out_specs=pl.BlockSpec((steps, nbatch * WIDTH), prev),
        out_shape=jax.ShapeDtypeStruct((seq, nbatch * WIDTH), F32),
        scratch_shapes=[pltpu.VMEM((blk, 2 * S5_NSTATE), F32), pltpu.VMEM((blk, 2 * S5_NSTATE), F32),
                        *[pltpu.VMEM((WIDTH // LANES, blk, LANES), F32)] * 3,
                        pltpu.VMEM((nbatch, 2 * S5_NSTATE), F32)],
        compiler_params=pltpu.CompilerParams(dimension_semantics=("arbitrary",),
                                             vmem_limit_bytes=VMEM_LIMIT),
        name="s5",
    )(u, bm, lam2, cm, d, gw, gb)


def _out_call(x, a, b2d, sg, c, d, w, fg, tm, final):
    nb, seq, _ = x.shape
    tok = lambda width: pl.BlockSpec((1, tm, width), lambda b, s: (b, s, 0))
    return pl.pallas_call(
        functools.partial(_out_kernel, final=final),
        grid=(nb, seq // tm),
        in_specs=[tok(D_MODEL), tok(WIDTH), pl.BlockSpec((tm, WIDTH), lambda b, s: (s, b)), tok(WIDTH),
                  tok(WIDTH), tok(WIDTH), _full((4 * WIDTH, D_MODEL)), _full((1, D_MODEL))],
        out_specs=tok(D_MODEL),
        out_shape=jax.ShapeDtypeStruct(x.shape, F32),
        compiler_params=pltpu.CompilerParams(dimension_semantics=("arbitrary", "arbitrary"),
                                             vmem_limit_bytes=VMEM_LIMIT),
        name="out_proj",
    )(x, a, b2d, sg, c, d, w, fg)


def kernel(x, norm_g, w_in, fox_fb, s5_a_re, s5_a_im, s5_log_dt, s5_b_re, s5_b_im, s5_c_re, s5_c_im, s5_d,
           s5_glu_w, s5_glu_b, mla_q_norm, mla_w_uq, mla_kv_norm, mla_w_ukv, w_out, final_g):
    nb, seq, _ = x.shape
    depth = norm_g.shape[0]
    assert nb == 8, "the S5 recurrence keeps the batch on the 8 sublanes of a vreg"
    tm, steps = _tiles(seq)
    assert seq % tm == 0 and seq % steps == 0 and seq // MOBA_BLOCK <= 32

    tri = jnp.tril(jnp.ones((tm, tm), F32)).astype(BF16)
    route, bconst = _fox_routing()
    moba_t, mla_t = _rope_tables(seq)
    fgain = final_g.astype(F32).reshape(1, D_MODEL)
    every_block = _first_blocks(nb, seq // tm)

    for l in range(depth):
        w, wvt = _arrange_w_in(w_in[l])
        fb = jnp.zeros((1, LANES), F32).at[0, :N_HEADS].set(fox_fb[l].astype(F32))
        wuq, wukv, wuvt = _arrange_mla(mla_w_uq[l].astype(F32), mla_w_ukv[l].astype(F32))
        (foxq, foxk, foxv, fg, su, sg, mobq, mobk, mobv, mg, mlaq, mlak, mlav, lg, fstat) = _in_call(
            x, norm_g[l].astype(F32).reshape(1, D_MODEL), w, fb, tri, route, bconst, moba_t, mla_t,
            mla_q_norm[l].astype(F32).reshape(1, MLA_Q_RANK), mla_kv_norm[l].astype(F32).reshape(1, MLA_KV_RANK),
            wuq, wukv, wvt, wuvt, tm)

        a_out = _attn_call(_first_blocks(nb, seq // tm, fstat), foxq, foxk, foxv, fg, tm, "fox_attn")
        c_out = _attn_call(every_block, mobq, mobk, mobv, mg, tm, "moba_attn")
        d_out = _attn_call(every_block, mlaq, mlak, mlav, lg, tm, "mla_attn")

        bm, lam2, cm = _s5_matrices(s5_a_re[l], s5_a_im[l], s5_log_dt[l], s5_b_re[l], s5_b_im[l],
                                    s5_c_re[l], s5_c_im[l])
        b_out = _s5_call(su, bm, lam2, cm, s5_d[l].astype(F32).reshape(1, WIDTH), s5_glu_w[l].astype(BF16),
                         s5_glu_b[l].astype(F32).reshape(1, WIDTH), steps, nb)
        x = _out_call(x, a_out, b_out, sg, c_out, d_out, w_out[l].astype(BF16), fgain, tm,
                      final=(l == depth - 1))
    return x
```

```python
import functools

import jax
import jax.numpy as jnp
from jax import lax
from jax.experimental import pallas as pl
from jax.experimental.pallas import tpu as pltpu

F32 = jnp.float32
BF16 = jnp.bfloat16

D_MODEL = 1024
HEAD_DIM = 64
DEN_ROWS = 16
PAST_UNROLL = 8
SKIP_GAP = 152.0
NORM_SLACK = 1.01
N_HEADS = 4
WIDTH = 256
S5_GROUPS = 16
S5_GROUP = 16
S5_STATE = 64
S5_NSTATE = S5_GROUPS * S5_STATE
MOBA_BLOCK = 256
MOBA_TOPK = 3
MAX_BLOCKS = 32
GROUP_HEAD = (1, 3, 0, 2)
LOG2E = 1.4426950408889634
MLA_NOPE = 64
MLA_ROPE = 32
MLA_V = 64
MLA_Q_RANK = 384
MLA_KV_RANK = 128
ROPE_THETA = 10000.0
EPS = 1e-6
NEG = -1e30
LANES = 128
VMEM_LIMIT = 56 * 1024 * 1024

C_FFKR = 0
C_CKV, C_CQ = 128, 256
C_FQ, C_FK = 640, 896
C_MQ, C_MK = 1152, 1408
C_FG, C_SU, C_SG, C_MG, C_LG = 1664, 1920, 2176, 2432, 2688
N_COLS = 2944
PROJ_GROUPS = ((0, 1152), (1152, 512), (1664, 1280))
FOX_BIAS_LANES = 6


def _split3(x):
    x1 = x.astype(BF16)
    r1 = x - x1.astype(F32)
    x2 = r1.astype(BF16)
    r2 = r1 - x2.astype(F32)
    return x1, x2, r2.astype(BF16)


def _rope(x, cos, sin_lo, sin_hi, half):
    return (x * cos + pltpu.roll(x, LANES - half, 1) * sin_lo + pltpu.roll(x, half, 1) * sin_hi)


def _rms(x, g):
    return x * lax.rsqrt(jnp.mean(x * x, axis=-1, keepdims=True) + EPS) * g


def _silu(g):
    return g * (1.0 / (1.0 + jnp.exp(-g)))


def _in_kernel(x_ref, g_ref, w_ref, fb_ref, tri_ref, route_ref, bconst_ref,
               mc_ref, msl_ref, msh_ref, lc_ref, lsl_ref, lsh_ref,
               gq_ref, gkv_ref, wuq_ref, wukv_ref, wvt_ref, wuvt_ref,
               foxq_ref, foxk_ref, foxv_ref, fg_ref, su_ref, sg_ref,
               mobq_ref, mobk_ref, mobv_ref, mg_ref,
               mlaq_ref, mlak_ref, mlav_ref, lg_ref, fstat_ref,
               carry_ref, km_ref, *, tm):
    sblk = pl.program_id(1)
    nblk = tm // MOBA_BLOCK

    @pl.when(sblk == 0)
    def _():
        carry_ref[...] = jnp.zeros_like(carry_ref)
        km_ref[...] = jnp.zeros_like(km_ref)

    h = _rms(x_ref[0], g_ref[...]).astype(BF16)

    z = [jnp.dot(h, w_ref[:, c0:c0 + width], preferred_element_type=F32) for c0, width in PROJ_GROUPS]

    def proj(c0, width):
        for zg, (g0, gw) in zip(z, PROJ_GROUPS):
            if g0 <= c0 and c0 + width <= g0 + gw:
                return zg[:, c0 - g0:c0 - g0 + width]
        raise ValueError("column range crosses a projection group")

    lane =lax.broadcasted_iota(jnp.int32, (tm, LANES), 1)
    row =lax.broadcasted_iota(jnp.int32, (tm, LANES), 0)
    low_half = lane < HEAD_DIM

    def proj_t(wt):
        return lax.dot_general(wt, h, (((1,), (1,)), ((), ())), preferred_element_type=F32)

    fg_ref[0] = proj(C_FG, WIDTH)
    sg_ref[0] = proj(C_SG, WIDTH)
    mg_ref[0] = proj(C_MG, WIDTH)
    lg_ref[0] = proj(C_LG, WIDTH)
    su_ref[...] = proj(C_SU, WIDTH)

    ffkr = proj(C_FFKR, LANES)
    ff = ffkr + fb_ref[...]
    logf = -(jnp.maximum(-ff, 0.0) + jnp.log1p(jnp.exp(-jnp.abs(ff))))
    within3 = jnp.dot(tri_ref[...], jnp.concatenate(_split3(logf), axis=1), preferred_element_type=F32)
    within = within3[:, 0:LANES] + within3[:, LANES:2 * LANES] + within3[:, 2 * LANES:]
    cum = within + carry_ref[...]
    carry_ref[...] = cum[tm - 1:tm, :]
    cum2 = cum * LOG2E
    routed = jnp.dot(jnp.concatenate(_split3(cum2), axis=1), route_ref[...],
                     preferred_element_type=F32) + bconst_ref[...]
    fq = proj(C_FQ, WIDTH) * (HEAD_DIM ** -0.5 * LOG2E)
    fk = proj(C_FK, WIDTH)
    lane_row = lane[0:1, :]
    qmax_row = jnp.zeros((1, LANES), F32)
    kmax_row = jnp.zeros((1, LANES), F32)
    for hd in range(N_HEADS):
        pair = hd // 2
        own = low_half if hd % 2 == 0 else jnp.logical_not(low_half)
        base = _fox_bias_base(hd)
        mine = (lane >= base) & (lane < base + FOX_BIAS_LANES)
        sl = slice(pair * LANES, (pair + 1) * LANES)
        qb = fq[:, sl].astype(BF16)
        kb = fk[:, sl].astype(BF16)
        qa = jnp.where(own, qb, jnp.where(mine, routed[:, 0:LANES], 0.0).astype(BF16))
        ka = jnp.where(own, kb, jnp.where(mine, routed[:, LANES:], 0.0).astype(BF16))
        foxq_ref[0, :, hd * LANES:(hd + 1) * LANES] = qa
        foxk_ref[0, :, hd * LANES:(hd + 1) * LANES] = ka
        for rounded, is_q in ((qb, True), (kb, False)):
            r = jnp.where(own, rounded.astype(F32), 0.0)
            norm = jnp.sqrt(jnp.max(jnp.sum(r * r, axis=1, keepdims=True), axis=0, keepdims=True))
            if is_q:
                qmax_row = jnp.where(lane_row == hd, norm, qmax_row)
            else:
                kmax_row = jnp.where(lane_row == hd, norm, kmax_row)
    srow = lax.broadcasted_iota(jnp.int32, (8, LANES), 0)
    fstat_ref[0, 0] = jnp.where(srow == 0, qmax_row, jnp.where(srow == 1, kmax_row, jnp.where(
        srow == 2, jnp.max(cum2, axis=0, keepdims=True), jnp.where(
            srow == 3, jnp.min(cum2, axis=0, keepdims=True), 0.0))))

    mc, msl, msh = mc_ref[...], msl_ref[...], msh_ref[...]
    mq = proj(C_MQ, WIDTH)
    mk = proj(C_MK, WIDTH)
    q_r = [_rope(mq[:, p * LANES:(p + 1) * LANES], mc, msl, msh, HEAD_DIM // 2) for p in range(2)]
    k_r = [_rope(mk[:, p * LANES:(p + 1) * LANES], mc, msl, msh, HEAD_DIM // 2) for p in range(2)]
    km_row = lax.broadcasted_iota(jnp.int32, (LANES, WIDTH), 0)
    km_lane = lax.broadcasted_iota(jnp.int32, (LANES, WIDTH), 1)
    km_grp = lax.shift_right_logical(km_row, 5)
    km_head = jnp.where(km_grp == 0, GROUP_HEAD[0], jnp.where(km_grp == 1, GROUP_HEAD[1],
                        jnp.where(km_grp == 2, GROUP_HEAD[2], GROUP_HEAD[3])))
    km_own = lax.shift_right_logical(km_lane, 6) == km_head
    kmt = km_ref[...]
    for nb in range(nblk):
        blk = sblk * nblk + nb
        km = jnp.concatenate(
            [jnp.mean(k_r[p][nb * MOBA_BLOCK:(nb + 1) * MOBA_BLOCK, :], axis=0, keepdims=True) for p in range(2)],
            axis=1)
        kmt = jnp.where((km_row & (MAX_BLOCKS - 1)) == blk, jnp.where(km_own, km, 0.0), kmt)
    km_ref[...] = kmt
    gate_t = lax.dot_general(kmt, jnp.concatenate(q_r, axis=1), (((1,), (1,)), ((), ())),
                             precision=lax.Precision.HIGHEST, preferred_element_type=F32)
    cand = lax.broadcasted_iota(jnp.int32, (MAX_BLOCKS, tm), 0).astype(F32)
    tok = lax.broadcasted_iota(jnp.int32, (MAX_BLOCKS, tm), 1)
    blk_tok = (sblk * nblk + lax.shift_right_logical(tok, 8)).astype(F32)
    bias_rows = []
    for grp in range(N_HEADS):
        g = jnp.where(cand < blk_tok, gate_t[grp * MAX_BLOCKS:(grp + 1) * MAX_BLOCKS, :], -jnp.inf)
        chosen = jnp.zeros((MAX_BLOCKS, tm), F32)
        for _ in range(MOBA_TOPK):
            m = jnp.max(g, axis=0, keepdims=True)
            first = jnp.min(jnp.where(g == m, cand, 1e9), axis=0, keepdims=True)
            first = jnp.where(m > -jnp.inf, first, -1.0)
            pick = cand == first
            chosen = jnp.where(pick, 1.0, chosen)
            g = jnp.where(pick, -jnp.inf, g)
        keep = jnp.where(cand == blk_tok, 1.0, chosen)
        bias_rows.append(jnp.where(keep > 0.0, 0.0, NEG))
    sel_bias = jnp.concatenate(bias_rows, axis=0).T
    blk_row = sblk * nblk + lax.shift_right_logical(row, 8)
    onehot = jnp.where((lane & (MAX_BLOCKS - 1)) == blk_row, 1.0, 0.0)
    lane_grp = lax.shift_right_logical(lane, 5)
    for hd in range(N_HEADS):
        pair = hd // 2
        own = low_half if hd % 2 == 0 else jnp.logical_not(low_half)
        mine = lane_grp == GROUP_HEAD.index(hd)
        qa = jnp.where(own, q_r[pair] * (HEAD_DIM ** -0.5 * LOG2E), jnp.where(mine, sel_bias, 0.0))
        ka = jnp.where(own, k_r[pair], jnp.where(mine, onehot, 0.0))
        mobq_ref[0, :, hd * LANES:(hd + 1) * LANES] = qa.astype(BF16)
        mobk_ref[0, :, hd * LANES:(hd + 1) * LANES] = ka.astype(BF16)

    lc, lsl, lsh = lc_ref[...], lsl_ref[...], lsh_ref[...]
    cqn = _rms(proj(C_CQ, MLA_Q_RANK), gq_ref[...]).astype(BF16)
    qf = jnp.dot(cqn, wuq_ref[...], preferred_element_type=F32)
    ckvn = _rms(proj(C_CKV, MLA_KV_RANK), gkv_ref[...]).astype(BF16)
    kv = jnp.dot(ckvn, wukv_ref[...], preferred_element_type=F32)
    kr = jnp.where(low_half, 0.0, _rope(ffkr, lc, lsl, lsh, MLA_ROPE // 2))
    scale = (MLA_NOPE + MLA_ROPE) ** -0.5 * LOG2E
    for hd in range(N_HEADS):
        sl = slice(hd * LANES, (hd + 1) * LANES)
        mlaq_ref[0, :, sl] = (_rope(qf[:, sl], lc, lsl, lsh, MLA_ROPE // 2) * scale).astype(BF16)
        mlak_ref[0, :, sl] = (kv[:, sl] + kr).astype(BF16)
    mlav_ref[0, 0] = lax.dot_general(wuvt_ref[...], ckvn, (((1,), (1,)), ((), ())),
                                     preferred_element_type=F32).astype(BF16)
    foxv_ref[0, 0] = proj_t(wvt_ref[0:WIDTH, :]).astype(BF16)
    mobv_ref[0, 0] = proj_t(wvt_ref[WIDTH:2 * WIDTH, :]).astype(BF16)


def _attn_kernel(tbl_ref, q_ref, k_ref, vt_ref, g_ref, o_ref, sa_ref, sb_ref, mxa_ref, mxb_ref, m_ref, acc_ref,
                 qt_ref, cur_ref, *, t, nq):
    bufs = ((sa_ref, mxa_ref), (sb_ref, mxb_ref))
    half = t // 2
    causal = (lax.broadcasted_iota(jnp.int32, (half, half), 0)
              <= lax.broadcasted_iota(jnp.int32, (half, half), 1))
    ones = jnp.ones((DEN_ROWS, t), BF16)
    pair = lambda i, j: (jnp.int32(i), jnp.int32(j))

    def transpose_tile(i):
        i = jnp.minimum(i, nq - 1)
        rows = pl.ds(pl.multiple_of(i * t, t), t)
        for hh in range(2):
            qt_ref[i, hh] = q_ref[0, rows, hh * LANES:(hh + 1) * LANES].astype(F32).T.astype(BF16)

    def scores(hh, ij, buf):
        s_ref, mx_ref = buf
        koff = pl.multiple_of(ij[1] * t, t)
        st = jnp.dot(k_ref[0, pl.ds(koff, t), hh * LANES:(hh + 1) * LANES], qt_ref[ij[0], hh],
                     preferred_element_type=F32)
        s_ref[hh] = st
        mx_ref[hh] = jnp.max(st, axis=0, keepdims=True)

    def values(hh, j):
        return jnp.concatenate([vt_ref[0, j, hh * HEAD_DIM:(hh + 1) * HEAD_DIM, :], ones], axis=0)

    def diag_scores(hh, i, buf):
        koff = pl.multiple_of(i * t, t)
        qt = qt_ref[i, hh]
        buf[0][hh, 0:half, :] = jnp.dot(k_ref[0, pl.ds(koff, half), hh * LANES:(hh + 1) * LANES], qt,
                                        preferred_element_type=F32)
        buf[0][hh, half:, half:] = jnp.dot(k_ref[0, pl.ds(koff + half, half), hh * LANES:(hh + 1) * LANES],
                                           qt[:, half:], preferred_element_type=F32)

    def first_update(hh, i, buf):
        lo = buf[0][hh, 0:half, :]
        lo_lo = jnp.where(causal, lo[:, 0:half], NEG)
        hi_hi = jnp.where(causal, buf[0][hh, half:, half:], NEG)
        mx_lo = jnp.max(lo_lo, axis=0, keepdims=True)
        mx_hi = jnp.maximum(jnp.max(lo[:, half:], axis=0, keepdims=True), jnp.max(hi_hi, axis=0, keepdims=True))
        p_lo = jnp.concatenate([jnp.exp2(lo_lo - mx_lo), jnp.exp2(lo[:, half:] - mx_hi)], axis=1).astype(BF16)
        p_hi = jnp.exp2(hi_hi - mx_hi).astype(BF16)
        va = values(hh, i)
        acc_lo = jnp.dot(va[:, 0:half], p_lo, preferred_element_type=F32)
        acc_hi = jnp.dot(va[:, half:], p_hi, preferred_element_type=F32)
        acc_ref[i, hh, :, 0:half] = acc_lo[:, 0:half]
        acc_ref[i, hh, :, half:] = acc_lo[:, half:] + acc_hi
        m_ref[i, hh] = jnp.concatenate([mx_lo, mx_hi], axis=1)

    def update(hh, ij, buf):
        i, j = ij
        m = m_ref[i, hh]
        m_new = jnp.maximum(m, buf[1][hh])
        alpha = jnp.exp2(m - m_new)
        p = jnp.exp2(buf[0][hh] - m_new).astype(BF16)
        acc_ref[i, hh] = alpha * acc_ref[i, hh] + jnp.dot(values(hh, j), p, preferred_element_type=F32)
        m_ref[i, hh] = m_new

    def stage(nxt, nxt_buf, cur, cur_buf, consume):
        for hh in range(2):
            if nxt is not None:
                scores(hh, nxt, nxt_buf)
            consume(hh, cur, cur_buf)

    tbl = (pl.program_id(0) * 2 + pl.program_id(1)) * (nq + 1)
    first_block = lambda i: tbl_ref[tbl + i]
    n_past = tbl_ref[tbl + nq]
    first_past = pair(min(1, nq - 1), 0)

    def diag_stage(nxt_diag, nxt_past, nxt_buf, cur, cur_buf):
        for hh in range(2):
            if nxt_diag is not None:
                diag_scores(hh, nxt_diag, nxt_buf)
            if nxt_past is not None:
                scores(hh, nxt_past, nxt_buf)
            first_update(hh, cur, cur_buf)

    for i in range(min(3, nq)):
        transpose_tile(jnp.int32(i))
    for hh in range(2):
        diag_scores(hh, jnp.int32(0), bufs[0])

    def two_diagonals(n, carry):
        i = 2 * n
        diag_stage(i + 1, None, bufs[1], i, bufs[0])
        diag_stage(i + 2, None, bufs[0], i + 1, bufs[1])
        transpose_tile(i + 3)
        transpose_tile(i + 4)
        return carry

    looped = 2 * ((nq - 1) // 2)
    lax.fori_loop(0, looped // 2, two_diagonals, 0)
    for i in range(looped, nq):
        last = i == nq - 1
        diag_stage(None if last else jnp.int32(i + 1), first_past if last and nq > 1 else None,
                   bufs[(i + 1) % 2], jnp.int32(i), bufs[i % 2])
    par = nq % 2

    def succ(ij):
        i, j = ij
        last = j == i - 1
        up = jnp.minimum(i + 1, nq - 1)
        return jnp.where(last, up, i), jnp.where(last, first_block(up), j + 1)

    def past_steps(count, cur):
        for k in range(count):
            nxt = succ(cur)
            stage(nxt, bufs[(par + k + 1) % 2], cur, bufs[(par + k) % 2], update)
            cur = nxt
        return cur

    cur = lax.fori_loop(0, n_past // PAST_UNROLL, lambda _, c: past_steps(PAST_UNROLL, c), first_past)
    cur_ref[0], cur_ref[1] = cur
    count = PAST_UNROLL // 2
    while count:
        @pl.when((n_past & count) != 0)
        def _(count=count):
            cur_ref[0], cur_ref[1] = past_steps(count, (cur_ref[0], cur_ref[1]))
        count //= 2

    def finish_tile(i, carry):
        out_t = jnp.concatenate([acc_ref[i, hh, 0:HEAD_DIM] / acc_ref[i, hh, HEAD_DIM:HEAD_DIM + 1]
                                 for hh in range(2)], axis=0)
        rows = pl.ds(pl.multiple_of(i * t, t), t)
        o_ref[0, rows, :] = (out_t.T * _silu(g_ref[0, rows, :])).astype(BF16)
        return carry

    lax.fori_loop(0, nq, finish_tile, 0)


def _s5_kernel(u_ref, bm_ref, lam_ref, cm_ref, d_ref, gw_ref, gb_ref, o_ref,
               xa_ref, xb_ref, ua_ref, ub_ref, y_ref, st_ref, *, steps, nbatch, nchunks):
    c = pl.program_id(0)

    @pl.when(c == 0)
    def _():
        st_ref[...] = jnp.zeros_like(st_ref)
        xb_ref[...] = jnp.zeros_like(xb_ref)
        ub_ref[...] = jnp.zeros_like(ub_ref)

    lam_re = jnp.broadcast_to(lam_ref[0:1, :], (nbatch, S5_NSTATE))
    lam_im = jnp.broadcast_to(lam_ref[1:2, :], (nbatch, S5_NSTATE))

    halves = WIDTH // LANES
    wide = lambda ref: jnp.concatenate([ref[hf] for hf in range(halves)], axis=1)

    def chunk(x_cur, x_prev, u_cur, u_prev):
        for b in range(nbatch):
            for hf in range(halves):
                lanes = slice(b * WIDTH + hf * LANES, b * WIDTH + (hf + 1) * LANES)
                u_cur[hf, pl.ds(b, steps, stride=nbatch), :] = u_ref[:, lanes]
        x_cur[...] = jnp.dot(wide(u_cur).astype(BF16), bm_ref[...], preferred_element_type=F32)
        y = jnp.dot(x_prev[...].astype(BF16), cm_ref[...], preferred_element_type=F32) + d_ref[...] * wide(u_prev)
        y = 0.5 * y * (1.0 + jnp.tanh(0.7978845608028654 * (y + 0.044715 * (y * y * y))))
        z = jnp.dot(y.astype(BF16), gw_ref[...], preferred_element_type=F32) + gb_ref[...]
        y = y * (1.0 / (1.0 + jnp.exp(-z)))
        for hf in range(halves):
            y_ref[hf] = y[:, hf * LANES:(hf + 1) * LANES]
        for b in range(nbatch):
            for hf in range(halves):
                lanes = slice(b * WIDTH + hf * LANES, b * WIDTH + (hf + 1) * LANES)
                o_ref[:, lanes] = y_ref[hf, pl.ds(b, steps, stride=nbatch), :]

        def body(t, carry):
            xr, xi = carry
            off = pl.multiple_of(t * nbatch, nbatch)
            nr = lam_re * xr - lam_im * xi + x_cur[pl.ds(off, nbatch), 0:S5_NSTATE]
            ni = lam_re * xi + lam_im * xr + x_cur[pl.ds(off, nbatch), S5_NSTATE:2 * S5_NSTATE]
            x_cur[pl.ds(off, nbatch), 0:S5_NSTATE] = nr
            x_cur[pl.ds(off, nbatch), S5_NSTATE:2 * S5_NSTATE] = ni
            return nr, ni

        @pl.when(c < nchunks)
        def _():
            xr, xi = lax.fori_loop(0, steps, body, (st_ref[:, 0:S5_NSTATE], st_ref[:, S5_NSTATE:2 * S5_NSTATE]))
            st_ref[:, 0:S5_NSTATE] = xr
            st_ref[:, S5_NSTATE:2 * S5_NSTATE] = xi

    @pl.when(c % 2 == 0)
    def _():
        chunk(xa_ref, xb_ref, ua_ref, ub_ref)

    @pl.when(c % 2 == 1)
    def _():
        chunk(xb_ref, xa_ref, ub_ref, ua_ref)


def _out_kernel(x_ref, a_ref, b_ref, sg_ref, c_ref, d_ref, w_ref, fg_ref, o_ref, *, final):
    b = (b_ref[...] * _silu(sg_ref[0])).astype(BF16)
    y = (jnp.dot(a_ref[0], w_ref[0:WIDTH, :], preferred_element_type=F32)
         + jnp.dot(b, w_ref[WIDTH:2 * WIDTH, :], preferred_element_type=F32)
         + jnp.dot(c_ref[0], w_ref[2 * WIDTH:3 * WIDTH, :], preferred_element_type=F32)
         + jnp.dot(d_ref[0], w_ref[3 * WIDTH:4 * WIDTH, :], preferred_element_type=F32))
    xn = x_ref[0] + y
    if final:
        xn = _rms(xn, fg_ref[...])
    o_ref[0] = xn


def _tiles(seq):
    tm = 512 if seq % 512 == 0 else MOBA_BLOCK
    steps = 128
    return tm, steps


def _arrange_w_in(w):
    z = lambda n: jnp.zeros((D_MODEL, n), w.dtype)
    o = 0
    parts = {}
    for name, n in (("fq", 256), ("fk", 256), ("fv", 256), ("fg", 256), ("ff", 4), ("su", 256), ("sg", 256),
                    ("mq", 256), ("mk", 256), ("mv", 256), ("mg", 256), ("cq", 384), ("ckv", 128), ("kr", 32),
                    ("lg", 256)):
        parts[name] = w[:, o:o + n]
        o += n
    cols = [parts["ff"], z(MLA_NOPE - 4), parts["kr"], z(LANES - MLA_NOPE - MLA_ROPE),
            parts["ckv"], parts["cq"], parts["fq"], parts["fk"], parts["mq"], parts["mk"],
            parts["fg"], parts["su"], parts["sg"], parts["mg"], parts["lg"]]
    w_vt = jnp.concatenate([parts["fv"], parts["mv"]], axis=1).T
    return jnp.concatenate(cols, axis=1).astype(BF16), w_vt.astype(BF16)


def _fox_bias_base(hd):
    return (HEAD_DIM if hd % 2 == 0 else 0) + FOX_BIAS_LANES * (hd // 2)


def _fox_routing():
    import numpy as np
    route = np.zeros((3 * LANES, 2 * LANES), np.float32)
    const = np.zeros((1, 2 * LANES), np.float32)
    for hd in range(N_HEADS):
        base = _fox_bias_base(hd)
        for part in range(3):
            route[part * LANES + hd, base + part] = 1.0
            const[0, base + 3 + part] = 1.0
            const[0, LANES + base + part] = 1.0
            route[part * LANES + hd, LANES + base + 3 + part] = -1.0
    return jnp.asarray(route, BF16), jnp.asarray(const, F32)


def _rope_tables(seq):
    pos = jnp.arange(seq).astype(F32)[:, None]
    lane = jnp.arange(LANES)
    half = HEAD_DIM // 2
    inv = jnp.power(ROPE_THETA, -jnp.arange(half, dtype=F32) / half)
    ang = pos * inv[None, :]
    cos, sin = jnp.cos(ang)[:, lane % half], jnp.sin(ang)[:, lane % half]
    lo = (lane % HEAD_DIM) < half
    moba = (cos, jnp.where(lo, -sin, 0.0), jnp.where(lo, 0.0, sin))
    half = MLA_ROPE // 2
    inv = jnp.power(ROPE_THETA, -jnp.arange(half, dtype=F32) / half)
    ang = pos * inv[None, :]
    cos, sin = jnp.cos(ang)[:, lane % half], jnp.sin(ang)[:, lane % half]
    in_lo = (lane >= MLA_NOPE) & (lane < MLA_NOPE + half)
    in_hi = (lane >= MLA_NOPE + half) & (lane < MLA_NOPE + MLA_ROPE)
    mla = (jnp.where(in_lo | in_hi, cos, 1.0), jnp.where(in_lo, -sin, 0.0), jnp.where(in_hi, sin, 0.0))
    return moba, mla


def _arrange_mla(w_uq, w_ukv):
    z = lambda r, n: jnp.zeros((r, n), F32)
    dq = MLA_NOPE + MLA_ROPE
    q_cols, k_cols, v_cols = [], [], []
    for hd in range(N_HEADS):
        q_cols += [w_uq[:, hd * dq:(hd + 1) * dq], z(MLA_Q_RANK, LANES - dq)]
        base = hd * (MLA_NOPE + MLA_V)
        k_cols += [w_ukv[:, base:base + MLA_NOPE], z(MLA_KV_RANK, LANES - MLA_NOPE)]
        v_cols += [w_ukv[:, base + MLA_NOPE:base + MLA_NOPE + MLA_V]]
    return (jnp.concatenate(q_cols, axis=1).astype(BF16), jnp.concatenate(k_cols, axis=1).astype(BF16),
            jnp.concatenate(v_cols, axis=1).T.astype(BF16))


def _s5_matrices(a_re, a_im, log_dt, b_re, b_im, c_re, c_im):
    lam = lax.complex(a_re.astype(F32), a_im.astype(F32))
    dt = jnp.exp(log_dt.astype(F32))[:, None]
    lam_bar = jnp.exp(lam * dt)
    b_bar = ((lam_bar - 1.0) / lam)[..., None] * lax.complex(b_re.astype(F32), b_im.astype(F32))
    eye = jnp.eye(S5_GROUPS, dtype=F32)
    blockdiag_in = lambda t: jnp.einsum('gpc,gh->gchp', t, eye).reshape(WIDTH, S5_NSTATE)
    blockdiag_out = lambda t: jnp.einsum('gcp,gh->gphc', t, eye).reshape(S5_NSTATE, WIDTH)
    bm = jnp.concatenate([blockdiag_in(b_bar.real), blockdiag_in(b_bar.imag)], axis=1).astype(BF16)
    cm = jnp.concatenate([blockdiag_out(c_re.astype(F32)), -blockdiag_out(c_im.astype(F32))], axis=0).astype(BF16)
    lam2 = jnp.stack([lam_bar.real.reshape(S5_NSTATE), lam_bar.imag.reshape(S5_NSTATE)], axis=0)
    return bm, lam2, cm


def _full(shape):
    return pl.BlockSpec(shape, lambda *_: (0,) * len(shape))


def _in_call(x, g, w, fb, tri, route, bconst, moba_t, mla_t, gq, gkv, wuq, wukv, wvt, wuvt, tm):
    nb, seq, _ = x.shape
    tok = lambda width: pl.BlockSpec((1, tm, width), lambda b, s: (b, s, 0))
    tab = pl.BlockSpec((tm, LANES), lambda b, s: (s, 0))
    bf = lambda width: jax.ShapeDtypeStruct((nb, seq, width), BF16)
    f32 = lambda width: jax.ShapeDtypeStruct((nb, seq, width), F32)
    val_t = jax.ShapeDtypeStruct((nb, seq // tm, WIDTH, tm), BF16)
    val_t_spec = pl.BlockSpec((1, 1, WIDTH, tm), lambda b, s: (b, s, 0, 0))
    out_shape = (bf(512), bf(512), val_t, f32(256),
                 jax.ShapeDtypeStruct((seq, nb * WIDTH), F32), f32(256),
                 bf(512), bf(512), val_t, f32(256),
                 bf(512), bf(512), val_t, f32(256),
                 jax.ShapeDtypeStruct((nb, seq // tm, 8, LANES), F32))
    out_specs = (tok(512), tok(512), val_t_spec, tok(256),
                 pl.BlockSpec((tm, WIDTH), lambda b, s: (s, b)), tok(256),
                 tok(512), tok(512), val_t_spec, tok(256),
                 tok(512), tok(512), val_t_spec, tok(256),
                 pl.BlockSpec((1, 1, 8, LANES), lambda b, s: (b, s, 0, 0)))
    in_specs = [tok(D_MODEL), _full((1, D_MODEL)), _full((D_MODEL, N_COLS)), _full((1, LANES)),
                _full((tm, tm)), _full((3 * LANES, 2 * LANES)), _full((1, 2 * LANES)),
                tab, tab, tab, tab, tab, tab,
                _full((1, MLA_Q_RANK)), _full((1, MLA_KV_RANK)),
                _full((MLA_Q_RANK, N_HEADS * LANES)), _full((MLA_KV_RANK, N_HEADS * LANES)),
                _full((2 * WIDTH, D_MODEL)), _full((WIDTH, MLA_KV_RANK))]
    return pl.pallas_call(
        functools.partial(_in_kernel, tm=tm),
        grid=(nb, seq // tm),
        in_specs=in_specs, out_specs=out_specs, out_shape=out_shape,
        scratch_shapes=[pltpu.VMEM((1, LANES), F32), pltpu.VMEM((LANES, WIDTH), F32)],
        compiler_params=pltpu.CompilerParams(dimension_semantics=("arbitrary", "arbitrary"),
                                             vmem_limit_bytes=VMEM_LIMIT),
        name="in_proj",
    )(x, g, w, fb, tri, route, bconst, *moba_t, *mla_t, gq, gkv, wuq, wukv, wvt, wuvt)


def _first_blocks(nb, nq, fstat=None):
    tiles = jnp.arange(nq)
    past = tiles[None, :] < tiles[:, None]
    if fstat is None:
        need = jnp.broadcast_to(past, (nb, 2, nq, nq))
    else:
        qn, kn = fstat[:, :, 0, :N_HEADS] * NORM_SLACK, fstat[:, :, 1, :N_HEADS] * NORM_SLACK
        cmax, cmin = fstat[:, :, 2, :N_HEADS], fstat[:, :, 3, :N_HEADS]
        gap = (qn[:, :, None] * kn[:, None, :] + cmax[:, :, None] - cmin[:, None, :]
               + (qn * kn)[:, :, None])
        need = (gap > -SKIP_GAP) & past[None, :, :, None]
        need = need.reshape(nb, nq, nq, 2, 2).any(-1).transpose(0, 3, 1, 2)
    first = jnp.where(need.any(-1), jnp.argmax(need, axis=-1), nq)
    first = jnp.minimum(first, jnp.maximum(tiles - 1, 0))
    count = jnp.sum(tiles - first, axis=-1, keepdims=True)
    return jnp.concatenate([first, count], axis=-1).astype(jnp.int32).reshape(-1)


def _attn_call(first_blocks, q, k, vt, gate, t, name):
    nb, seq, _ = q.shape
    whole = lambda width: pl.BlockSpec((1, seq, width), lambda b, p, tbl: (b, 0, p))
    return pl.pallas_call(
        functools.partial(_attn_kernel, t=t, nq=seq // t),
        grid_spec=pltpu.PrefetchScalarGridSpec(
            num_scalar_prefetch=1, grid=(nb, 2),
            in_specs=[whole(2 * LANES), whole(2 * LANES),
                      pl.BlockSpec((1, seq // t, LANES, t), lambda b, p, tbl: (b, 0, p, 0)), whole(LANES)],
            out_specs=whole(LANES),
            scratch_shapes=[pltpu.VMEM((2, t, t), F32), pltpu.VMEM((2, t, t), F32),
                            pltpu.VMEM((2, 1, t), F32), pltpu.VMEM((2, 1, t), F32),
                            pltpu.VMEM((seq // t, 2, 1, t), F32),
                            pltpu.VMEM((seq // t, 2, HEAD_DIM + DEN_ROWS, t), F32),
                            pltpu.VMEM((seq // t, 2, LANES, t), BF16),
                            pltpu.SMEM((2,), jnp.int32)]),
        out_shape=jax.ShapeDtypeStruct((nb, seq, WIDTH), BF16),
        compiler_params=pltpu.CompilerParams(dimension_semantics=("arbitrary", "arbitrary"),
                                             vmem_limit_bytes=VMEM_LIMIT),
        name=name,
    )(first_blocks, q, k, vt, gate)


def _s5_call(u, bm, lam2, cm, d, gw, gb, steps, nbatch):
    seq = u.shape[0]
    blk = steps * nbatch
    nchunks = seq // steps
    prev = lambda c: (jnp.maximum(c - 1, 0), 0)
    return pl.pallas_call(
        functools.partial(_s5_kernel, steps=steps, nbatch=nbatch, nchunks=nchunks),
        grid=(nchunks + 1,),
        in_specs=[pl.BlockSpec((steps, nbatch * WIDTH), lambda c: (jnp.minimum(c, nchunks - 1), 0)),
                  _full((WIDTH, 2 * S5_NSTATE)), _full((2, S5_NSTATE)), _full((2 * S5_NSTATE, WIDTH)),
                  _full((1, WIDTH)), _full((WIDTH, WIDTH)), _full((1, WIDTH))],
        out_specs=pl.BlockSpec((steps, nbatch * WIDTH), prev),
        out_shape=jax.ShapeDtypeStruct((seq, nbatch * WIDTH), F32),
        scratch_shapes=[pltpu.VMEM((blk, 2 * S5_NSTATE), F32), pltpu.VMEM((blk, 2 * S5_NSTATE), F32),
                        *[pltpu.VMEM((WIDTH // LANES, blk, LANES), F32)] * 3,
                        pltpu.VMEM((nbatch, 2 * S5_NSTATE), F32)],
        compiler_params=pltpu.CompilerParams(dimension_semantics=("arbitrary",),
                                             vmem_limit_bytes=VMEM_LIMIT),
        name="s5",
    )(u, bm, lam2, cm, d, gw, gb)


def _out_call(x, a, b2d, sg, c, d, w, fg, tm, final):
    nb, seq, _ = x.shape
    tok = lambda width: pl.BlockSpec((1, tm, width), lambda b, s: (b, s, 0))
    return pl.pallas_call(
        functools.partial(_out_kernel, final=final),
        grid=(nb, seq // tm),
        in_specs=[tok(D_MODEL), tok(WIDTH), pl.BlockSpec((tm, WIDTH), lambda b, s: (s, b)), tok(WIDTH),
                  tok(WIDTH), tok(WIDTH), _full((4 * WIDTH, D_MODEL)), _full((1, D_MODEL))],
        out_specs=tok(D_MODEL),
        out_shape=jax.ShapeDtypeStruct(x.shape, F32),
        compiler_params=pltpu.CompilerParams(dimension_semantics=("arbitrary", "arbitrary"),
                                             vmem_limit_bytes=VMEM_LIMIT),
        name="out_proj",
    )(x, a, b2d, sg, c, d, w, fg)


def kernel(x, norm_g, w_in, fox_fb, s5_a_re, s5_a_im, s5_log_dt, s5_b_re, s5_b_im, s5_c_re, s5_c_im, s5_d,
           s5_glu_w, s5_glu_b, mla_q_norm, mla_w_uq, mla_kv_norm, mla_w_ukv, w_out, final_g):
    nb, seq, _ = x.shape
    depth = norm_g.shape[0]
    assert nb == 8, "the S5 recurrence keeps the batch on the 8 sublanes of a vreg"
    tm, steps = _tiles(seq)
    assert seq % tm == 0 and seq % steps == 0 and seq // MOBA_BLOCK <= 32

    tri = jnp.tril(jnp.ones((tm, tm), F32)).astype(BF16)
    route, bconst = _fox_routing()
    moba_t, mla_t = _rope_tables(seq)
    fgain = final_g.astype(F32).reshape(1, D_MODEL)
    every_block = _first_blocks(nb, seq // tm)

    for l in range(depth):
        w, wvt = _arrange_w_in(w_in[l])
        fb = jnp.zeros((1, LANES), F32).at[0, :N_HEADS].set(fox_fb[l].astype(F32))
        wuq, wukv, wuvt = _arrange_mla(mla_w_uq[l].astype(F32), mla_w_ukv[l].astype(F32))
        (foxq, foxk, foxv, fg, su, sg, mobq, mobk, mobv, mg, mlaq, mlak, mlav, lg, fstat) = _in_call(
            x, norm_g[l].astype(F32).reshape(1, D_MODEL), w, fb, tri, route, bconst, moba_t, mla_t,
            mla_q_norm[l].astype(F32).reshape(1, MLA_Q_RANK), mla_kv_norm[l].astype(F32).reshape(1, MLA_KV_RANK),
            wuq, wukv, wvt, wuvt, tm)

        a_out = _attn_call(_first_blocks(nb, seq // tm, fstat), foxq, foxk, foxv, fg, tm, "fox_attn")
        c_out = _attn_call(every_block, mobq, mobk, mobv, mg, tm, "moba_attn")
        d_out = _attn_call(every_block, mlaq, mlak, mlav, lg, tm, "mla_attn")

        bm, lam2, cm = _s5_matrices(s5_a_re[l], s5_a_im[l], s5_log_dt[l], s5_b_re[l], s5_b_im[l],
                                    s5_c_re[l], s5_c_im[l])
        b_out = _s5_call(su, bm, lam2, cm, s5_d[l].astype(F32).reshape(1, WIDTH), s5_glu_w[l].astype(BF16),
                         s5_glu_b[l].astype(F32).reshape(1, WIDTH), steps, nb)
        x = _out_call(x, a_out, b_out, sg, c_out, d_out, w_out[l].astype(BF16), fgain, tm,
                      final=(l == depth - 1))
    return x
```

```python
import functools

import jax
import jax.numpy as jnp
from jax import lax
from jax.experimental import pallas as pl
from jax.experimental.pallas import tpu as pltpu

F32 = jnp.float32
BF16 = jnp.bfloat16

D_MODEL = 1024
HEAD_DIM = 64
DEN_ROWS = 16
PAST_UNROLL = 16
SKIP_GAP = 152.0
NORM_SLACK = 1.01
N_HEADS = 4
WIDTH = 256
S5_GROUPS = 16
S5_GROUP = 16
S5_STATE = 64
S5_NSTATE = S5_GROUPS * S5_STATE
MOBA_BLOCK = 256
MOBA_TOPK = 3
MAX_BLOCKS = 32
GROUP_HEAD = (1, 3, 0, 2)
LOG2E = 1.4426950408889634
MLA_NOPE = 64
MLA_ROPE = 32
MLA_V = 64
MLA_Q_RANK = 384
MLA_KV_RANK = 128
ROPE_THETA = 10000.0
EPS = 1e-6
NEG = -1e30
LANES = 128
VMEM_LIMIT = 56 * 1024 * 1024

C_FFKR = 0
C_CKV, C_CQ = 128, 256
C_FQ, C_FK = 640, 896
C_MQ, C_MK = 1152, 1408
C_FG, C_SU, C_SG, C_MG, C_LG = 1664, 1920, 2176, 2432, 2688
N_COLS = 2944
PROJ_GROUPS = ((0, 1152), (1152, 512), (1664, 1280))
FOX_BIAS_LANES = 6


def _split3(x):
    x1 = x.astype(BF16)
    r1 = x - x1.astype(F32)
    x2 = r1.astype(BF16)
    r2 = r1 - x2.astype(F32)
    return x1, x2, r2.astype(BF16)


def _rope(x, cos, sin_lo, sin_hi, half):
    return (x * cos + pltpu.roll(x, LANES - half, 1) * sin_lo + pltpu.roll(x, half, 1) * sin_hi)


def _rms(x, g):
    return x * lax.rsqrt(jnp.mean(x * x, axis=-1, keepdims=True) + EPS) * g


def _silu(g):
    return g * (1.0 / (1.0 + jnp.exp(-g)))


def _in_kernel(x_ref, g_ref, w_ref, fb_ref, tri_ref, route_ref, bconst_ref,
               mc_ref, msl_ref, msh_ref, lc_ref, lsl_ref, lsh_ref,
               gq_ref, gkv_ref, wuq_ref, wukv_ref, wvt_ref, wuvt_ref,
               foxq_ref, foxk_ref, foxv_ref, fg_ref, su_ref, sg_ref,
               mobq_ref, mobk_ref, mobv_ref, mg_ref,
               mlaq_ref, mlak_ref, mlav_ref, lg_ref, fstat_ref,
               carry_ref, km_ref, *, tm):
    sblk = pl.program_id(1)
    nblk = tm // MOBA_BLOCK

    @pl.when(sblk == 0)
    def _():
        carry_ref[...] = jnp.zeros_like(carry_ref)
        km_ref[...] = jnp.zeros_like(km_ref)

    h = _rms(x_ref[0], g_ref[...]).astype(BF16)

    z = {}

    def proj(c0, width):
        for g0, gw in PROJ_GROUPS:
            if g0 <= c0 and c0 + width <= g0 + gw:
                if g0 not in z:
                    z[g0] = jnp.dot(h, w_ref[:, g0:g0 + gw], preferred_element_type=F32)
                return z[g0][:, c0 - g0:c0 - g0 + width]
        raise ValueError("column range crosses a projection group")

    lane = lax.broadcasted_iota(jnp.int32, (tm, LANES), 1)
    row = lax.broadcasted_iota(jnp.int32, (tm, LANES), 0)
    low_half = lane < HEAD_DIM

    def proj_t(wt):
        return lax.dot_general(wt, h, (((1,), (1,)), ((), ())), preferred_element_type=F32)

    for g0, gw in PROJ_GROUPS:
        proj(g0, gw)
    fg_ref[0] = proj(C_FG, WIDTH)
    sg_ref[0] = proj(C_SG, WIDTH)
    mg_ref[0] = proj(C_MG, WIDTH)
    lg_ref[0] = proj(C_LG, WIDTH)
    su_ref[...] = proj(C_SU, WIDTH)

    ffkr = proj(C_FFKR, LANES)
    ff = ffkr + fb_ref[...]
    logf = -(jnp.maximum(-ff, 0.0) + jnp.log1p(jnp.exp(-jnp.abs(ff))))
    within3 = jnp.dot(tri_ref[...], jnp.concatenate(_split3(logf), axis=1), preferred_element_type=F32)
    within = within3[:, 0:LANES] + within3[:, LANES:2 * LANES] + within3[:, 2 * LANES:]
    cum = within + carry_ref[...]
    carry_ref[...] = cum[tm - 1:tm, :]
    cum2 = cum * LOG2E
    routed = jnp.dot(jnp.concatenate(_split3(cum2), axis=1), route_ref[...],
                     preferred_element_type=F32) + bconst_ref[...]
    fq = proj(C_FQ, WIDTH) * (HEAD_DIM ** -0.5 * LOG2E)
    fk = proj(C_FK, WIDTH)
    lane_row = lane[0:1, :]
    qmax_row = jnp.zeros((1, LANES), F32)
    kmax_row = jnp.zeros((1, LANES), F32)
    for hd in range(N_HEADS):
        pair = hd // 2
        own = low_half if hd % 2 == 0 else jnp.logical_not(low_half)
        base = _fox_bias_base(hd)
        mine = (lane >= base) & (lane < base + FOX_BIAS_LANES)
        sl = slice(pair * LANES, (pair + 1) * LANES)
        qb = fq[:, sl].astype(BF16)
        kb = fk[:, sl].astype(BF16)
        qa = jnp.where(own, qb, jnp.where(mine, routed[:, 0:LANES], 0.0).astype(BF16))
        ka = jnp.where(own, kb, jnp.where(mine, routed[:, LANES:], 0.0).astype(BF16))
        foxq_ref[0, :, hd * LANES:(hd + 1) * LANES] = qa
        foxk_ref[0, :, hd * LANES:(hd + 1) * LANES] = ka
        for rounded, is_q in ((qb, True), (kb, False)):
            r = jnp.where(own, rounded.astype(F32), 0.0)
            norm = jnp.sqrt(jnp.max(jnp.sum(r * r, axis=1, keepdims=True), axis=0, keepdims=True))
            if is_q:
                qmax_row = jnp.where(lane_row == hd, norm, qmax_row)
            else:
                kmax_row = jnp.where(lane_row == hd, norm, kmax_row)
    srow = lax.broadcasted_iota(jnp.int32, (8, LANES), 0)
    fstat_ref[0, 0] = jnp.where(srow == 0, qmax_row, jnp.where(srow == 1, kmax_row, jnp.where(
        srow == 2, jnp.max(cum2, axis=0, keepdims=True), jnp.where(
            srow == 3, jnp.min(cum2, axis=0, keepdims=True), 0.0))))

    mc, msl, msh = mc_ref[...], msl_ref[...], msh_ref[...]
    mq = proj(C_MQ, WIDTH)
    mk = proj(C_MK, WIDTH)
    q_r = [_rope(mq[:, p * LANES:(p + 1) * LANES], mc, msl, msh, HEAD_DIM // 2) for p in range(2)]
    k_r = [_rope(mk[:, p * LANES:(p + 1) * LANES], mc, msl, msh, HEAD_DIM // 2) for p in range(2)]
    km_row = lax.broadcasted_iota(jnp.int32, (LANES, WIDTH), 0)
    km_lane = lax.broadcasted_iota(jnp.int32, (LANES, WIDTH), 1)
    km_grp = lax.shift_right_logical(km_row, 5)
    km_head = jnp.where(km_grp == 0, GROUP_HEAD[0], jnp.where(km_grp == 1, GROUP_HEAD[1],
                        jnp.where(km_grp == 2, GROUP_HEAD[2], GROUP_HEAD[3])))
    km_own = lax.shift_right_logical(km_lane, 6) == km_head
    kmt = km_ref[...]
    for nb in range(nblk):
        blk = sblk * nblk + nb
        km = jnp.concatenate(
            [jnp.mean(k_r[p][nb * MOBA_BLOCK:(nb + 1) * MOBA_BLOCK, :], axis=0, keepdims=True) for p in range(2)],
            axis=1)
        kmt = jnp.where((km_row & (MAX_BLOCKS - 1)) == blk, jnp.where(km_own, km, 0.0), kmt)
    km_ref[...] = kmt
    gate_t = lax.dot_general(kmt, jnp.concatenate(q_r, axis=1), (((1,), (1,)), ((), ())),
                             precision=lax.Precision.HIGHEST, preferred_element_type=F32)
    cand = lax.broadcasted_iota(jnp.int32, (MAX_BLOCKS, tm), 0).astype(F32)
    tok = lax.broadcasted_iota(jnp.int32, (MAX_BLOCKS, tm), 1)
    blk_tok = (sblk * nblk + lax.shift_right_logical(tok, 8)).astype(F32)
    bias_rows = []
    for grp in range(N_HEADS):
        g = jnp.where(cand < blk_tok, gate_t[grp * MAX_BLOCKS:(grp + 1) * MAX_BLOCKS, :], -jnp.inf)
        chosen = jnp.zeros((MAX_BLOCKS, tm), F32)
        for _ in range(MOBA_TOPK):
            m = jnp.max(g, axis=0, keepdims=True)
            first = jnp.min(jnp.where(g == m, cand, 1e9), axis=0, keepdims=True)
            first = jnp.where(m > -jnp.inf, first, -1.0)
            pick = cand == first
            chosen = jnp.where(pick, 1.0, chosen)
            g = jnp.where(pick, -jnp.inf, g)
        keep = jnp.where(cand == blk_tok, 1.0, chosen)
        bias_rows.append(jnp.where(keep > 0.0, 0.0, NEG))
    sel_bias = jnp.concatenate(bias_rows, axis=0).T
    blk_row = sblk * nblk + lax.shift_right_logical(row, 8)
    onehot = jnp.where((lane & (MAX_BLOCKS - 1)) == blk_row, 1.0, 0.0)
    lane_grp = lax.shift_right_logical(lane, 5)
    for hd in range(N_HEADS):
        pair = hd // 2
        own = low_half if hd % 2 == 0 else jnp.logical_not(low_half)
        mine = lane_grp == GROUP_HEAD.index(hd)
        qa = jnp.where(own, q_r[pair] * (HEAD_DIM ** -0.5 * LOG2E), jnp.where(mine, sel_bias, 0.0))
        ka = jnp.where(own, k_r[pair], jnp.where(mine, onehot, 0.0))
        mobq_ref[0, :, hd * LANES:(hd + 1) * LANES] = qa.astype(BF16)
        mobk_ref[0, :, hd * LANES:(hd + 1) * LANES] = ka.astype(BF16)

    lc, lsl, lsh = lc_ref[...], lsl_ref[...], lsh_ref[...]
    cqn = _rms(proj(C_CQ, MLA_Q_RANK), gq_ref[...]).astype(BF16)
    qf = jnp.dot(cqn, wuq_ref[...], preferred_element_type=F32)
    ckvn = _rms(proj(C_CKV, MLA_KV_RANK), gkv_ref[...]).astype(BF16)
    kv = jnp.dot(ckvn, wukv_ref[...], preferred_element_type=F32)
    kr = jnp.where(low_half, 0.0, _rope(ffkr, lc, lsl, lsh, MLA_ROPE // 2))
    scale = (MLA_NOPE + MLA_ROPE) ** -0.5 * LOG2E
    for hd in range(N_HEADS):
        sl = slice(hd * LANES, (hd + 1) * LANES)
        mlaq_ref[0, :, sl] = (_rope(qf[:, sl], lc, lsl, lsh, MLA_ROPE // 2) * scale).astype(BF16)
        mlak_ref[0, :, sl] = (kv[:, sl] + kr).astype(BF16)
    mlav_ref[0, 0] = lax.dot_general(wuvt_ref[...], ckvn, (((1,), (1,)), ((), ())),
                                     preferred_element_type=F32).astype(BF16)
    foxv_ref[0, 0] = proj_t(wvt_ref[0:WIDTH, :]).astype(BF16)
    mobv_ref[0, 0] = proj_t(wvt_ref[WIDTH:2 * WIDTH, :]).astype(BF16)


def _attn_kernel(tbl_ref, q_ref, k_ref, vt_ref, g_ref, o_ref, sa_ref, sb_ref, mxa_ref, mxb_ref, m_ref, acc_ref,
                 qt_ref, cur_ref, *, t, nq):
    bufs = ((sa_ref, mxa_ref), (sb_ref, mxb_ref))
    half = t // 2
    causal = (lax.broadcasted_iota(jnp.int32, (half, half), 0)
              <= lax.broadcasted_iota(jnp.int32, (half, half), 1))
    ones = jnp.ones((DEN_ROWS, t), BF16)
    pair = lambda i, j: (jnp.int32(i), jnp.int32(j))

    def transpose_tile(i):
        i = jnp.minimum(i, nq - 1)
        rows = pl.ds(pl.multiple_of(i * t, t), t)
        for hh in range(2):
            qt_ref[i, hh] = q_ref[0, rows, hh * LANES:(hh + 1) * LANES].astype(F32).T.astype(BF16)

    def scores(hh, ij, buf):
        s_ref, mx_ref = buf
        koff = pl.multiple_of(ij[1] * t, t)
        st = jnp.dot(k_ref[0, pl.ds(koff, t), hh * LANES:(hh + 1) * LANES], qt_ref[ij[0], hh],
                     preferred_element_type=F32)
        s_ref[hh] = st
        mx_ref[hh] = jnp.max(st, axis=0, keepdims=True)

    def values(hh, j):
        return jnp.concatenate([vt_ref[0, j, hh * HEAD_DIM:(hh + 1) * HEAD_DIM, :], ones], axis=0)

    def diag_scores(hh, i, buf):
        koff = pl.multiple_of(i * t, t)
        qt = qt_ref[i, hh]
        buf[0][hh, 0:half, :] = jnp.dot(k_ref[0, pl.ds(koff, half), hh * LANES:(hh + 1) * LANES], qt,
                                        preferred_element_type=F32)
        buf[0][hh, half:, half:] = jnp.dot(k_ref[0, pl.ds(koff + half, half), hh * LANES:(hh + 1) * LANES],
                                           qt[:, half:], preferred_element_type=F32)

    def first_update(hh, i, buf):
        lo = buf[0][hh, 0:half, :]
        lo_lo = jnp.where(causal, lo[:, 0:half], NEG)
        hi_hi = jnp.where(causal, buf[0][hh, half:, half:], NEG)
        mx_lo = jnp.max(lo_lo, axis=0, keepdims=True)
        mx_hi = jnp.maximum(jnp.max(lo[:, half:], axis=0, keepdims=True), jnp.max(hi_hi, axis=0, keepdims=True))
        p_lo = jnp.concatenate([jnp.exp2(lo_lo - mx_lo), jnp.exp2(lo[:, half:] - mx_hi)], axis=1).astype(BF16)
        p_hi = jnp.exp2(hi_hi - mx_hi).astype(BF16)
        va = values(hh, i)
        acc_lo = jnp.dot(va[:, 0:half], p_lo, preferred_element_type=F32)
        acc_hi = jnp.dot(va[:, half:], p_hi, preferred_element_type=F32)
        acc_ref[i, hh, :, 0:half] = acc_lo[:, 0:half]
        acc_ref[i, hh, :, half:] = acc_lo[:, half:] + acc_hi
        m_ref[i, hh] = jnp.concatenate([mx_lo, mx_hi], axis=1)

    def update(hh, ij, buf):
        i, j = ij
        m = m_ref[i, hh]
        m_new = jnp.maximum(m, buf[1][hh])
        alpha = jnp.exp2(m - m_new)
        p = jnp.exp2(buf[0][hh] - m_new).astype(BF16)
        acc_ref[i, hh] = alpha * acc_ref[i, hh] + jnp.dot(values(hh, j), p, preferred_element_type=F32)
        m_ref[i, hh] = m_new

    def stage(nxt, nxt_buf, cur, cur_buf, consume):
        for hh in range(2):
            if nxt is not None:
                scores(hh, nxt, nxt_buf)
            consume(hh, cur, cur_buf)

    tbl = (pl.program_id(0) * 2 + pl.program_id(1)) * (nq + 1)
    first_block = lambda i: tbl_ref[tbl + i]
    n_past = tbl_ref[tbl + nq]
    first_past = pair(min(1, nq - 1), 0)

    def diag_stage(nxt_diag, nxt_past, nxt_buf, cur, cur_buf):
        for hh in range(2):
            if nxt_diag is not None:
                diag_scores(hh, nxt_diag, nxt_buf)
            if nxt_past is not None:
                scores(hh, nxt_past, nxt_buf)
            first_update(hh, cur, cur_buf)

    for i in range(min(3, nq)):
        transpose_tile(jnp.int32(i))
    for hh in range(2):
        diag_scores(hh, jnp.int32(0), bufs[0])

    def two_diagonals(n, carry):
        i = 2 * n
        diag_stage(i + 1, None, bufs[1], i, bufs[0])
        diag_stage(i + 2, None, bufs[0], i + 1, bufs[1])
        transpose_tile(i + 3)
        transpose_tile(i + 4)
        return carry

    looped = 2 * ((nq - 1) // 2)
    lax.fori_loop(0, looped // 2, two_diagonals, 0)
    for i in range(looped, nq):
        last = i == nq - 1
        diag_stage(None if last else jnp.int32(i + 1), first_past if last and nq > 1 else None,
                   bufs[(i + 1) % 2], jnp.int32(i), bufs[i % 2])
    par = nq % 2

    def succ(ij):
        i, j = ij
        last = j == i - 1
        up = jnp.minimum(i + 1, nq - 1)
        return jnp.where(last, up, i), jnp.where(last, first_block(up), j + 1)

    def past_steps(count, cur):
        for k in range(count):
            nxt = succ(cur)
            stage(nxt, bufs[(par + k + 1) % 2], cur, bufs[(par + k) % 2], update)
            cur = nxt
        return cur

    cur = lax.fori_loop(0, n_past // PAST_UNROLL, lambda _, c: past_steps(PAST_UNROLL, c), first_past)
    cur_ref[0], cur_ref[1] = cur
    count = PAST_UNROLL // 2
    while count:
        @pl.when((n_past & count) != 0)
        def _(count=count):
            cur_ref[0], cur_ref[1] = past_steps(count, (cur_ref[0], cur_ref[1]))
        count //= 2

    def finish_tile(i, carry):
        out_t = jnp.concatenate([acc_ref[i, hh, 0:HEAD_DIM] / acc_ref[i, hh, HEAD_DIM:HEAD_DIM + 1]
                                 for hh in range(2)], axis=0)
        rows = pl.ds(pl.multiple_of(i * t, t), t)
        o_ref[0, rows, :] = (out_t.T * _silu(g_ref[0, rows, :])).astype(BF16)
        return carry

    lax.fori_loop(0, nq, finish_tile, 0)


def _s5_kernel(u_ref, bm_ref, lam_ref, cm_ref, d_ref, gw_ref, gb_ref, o_ref,
               xa_ref, xb_ref, ua_ref, ub_ref, y_ref, st_ref, *, steps, nbatch, nchunks):
    c = pl.program_id(0)

    @pl.when(c == 0)
    def _():
        st_ref[...] = jnp.zeros_like(st_ref)
        xb_ref[...] = jnp.zeros_like(xb_ref)
        ub_ref[...] = jnp.zeros_like(ub_ref)

    lam_re = jnp.broadcast_to(lam_ref[0:1, :], (nbatch, S5_NSTATE))
    lam_im = jnp.broadcast_to(lam_ref[1:2, :], (nbatch, S5_NSTATE))

    halves = WIDTH // LANES
    wide = lambda ref: jnp.concatenate([ref[hf] for hf in range(halves)], axis=1)

    def chunk(x_cur, x_prev, u_cur, u_prev):
        for b in range(nbatch):
            for hf in range(halves):
                lanes = slice(b * WIDTH + hf * LANES, b * WIDTH + (hf + 1) * LANES)
                u_cur[hf, pl.ds(b, steps, stride=nbatch), :] = u_ref[:, lanes]
        x_cur[...] = jnp.dot(wide(u_cur).astype(BF16), bm_ref[...], preferred_element_type=F32)
        y = jnp.dot(x_prev[...].astype(BF16), cm_ref[...], preferred_element_type=F32) + d_ref[...] * wide(u_prev)
        y = 0.5 * y * (1.0 + jnp.tanh(0.7978845608028654 * (y + 0.044715 * (y * y * y))))
        z = jnp.dot(y.astype(BF16), gw_ref[...], preferred_element_type=F32) + gb_ref[...]
        y = y * (1.0 / (1.0 + jnp.exp(-z)))
        for hf in range(halves):
            y_ref[hf] = y[:, hf * LANES:(hf + 1) * LANES]
        for b in range(nbatch):
            for hf in range(halves):
                lanes = slice(b * WIDTH + hf * LANES, b * WIDTH + (hf + 1) * LANES)
                o_ref[:, lanes] = y_ref[hf, pl.ds(b, steps, stride=nbatch), :]

        def body(t, carry):
            xr, xi = carry
            off = pl.multiple_of(t * nbatch, nbatch)
            nr = lam_re * xr - lam_im * xi + x_cur[pl.ds(off, nbatch), 0:S5_NSTATE]
            ni = lam_re * xi + lam_im * xr + x_cur[pl.ds(off, nbatch), S5_NSTATE:2 * S5_NSTATE]
            x_cur[pl.ds(off, nbatch), 0:S5_NSTATE] = nr
            x_cur[pl.ds(off, nbatch), S5_NSTATE:2 * S5_NSTATE] = ni
            return nr, ni

        @pl.when(c < nchunks)
        def _():
            xr, xi = lax.fori_loop(0, steps, body, (st_ref[:, 0:S5_NSTATE], st_ref[:, S5_NSTATE:2 * S5_NSTATE]))
            st_ref[:, 0:S5_NSTATE] = xr
            st_ref[:, S5_NSTATE:2 * S5_NSTATE] = xi

    @pl.when(c % 2 == 0)
    def _():
        chunk(xa_ref, xb_ref, ua_ref, ub_ref)

    @pl.when(c % 2 == 1)
    def _():
        chunk(xb_ref, xa_ref, ub_ref, ua_ref)


def _out_kernel(x_ref, a_ref, b_ref, sg_ref, c_ref, d_ref, w_ref, fg_ref, o_ref, *, final):
    b = (b_ref[...] * _silu(sg_ref[0])).astype(BF16)
    y = (jnp.dot(a_ref[0], w_ref[0:WIDTH, :], preferred_element_type=F32)
         + jnp.dot(b, w_ref[WIDTH:2 * WIDTH, :], preferred_element_type=F32)
         + jnp.dot(c_ref[0], w_ref[2 * WIDTH:3 * WIDTH, :], preferred_element_type=F32)
         + jnp.dot(d_ref[0], w_ref[3 * WIDTH:4 * WIDTH, :], preferred_element_type=F32))
    xn = x_ref[0] + y
    if final:
        xn = _rms(xn, fg_ref[...])
    o_ref[0] = xn


def _tiles(seq):
    tm = 512 if seq % 512 == 0 else MOBA_BLOCK
    tm_out = 1024 if seq % 1024 == 0 else tm
    steps = 128
    return tm, tm_out, steps


def _arrange_w_in(w):
    z = lambda n: jnp.zeros((D_MODEL, n), w.dtype)
    o = 0
    parts = {}
    for name, n in (("fq", 256), ("fk", 256), ("fv", 256), ("fg", 256), ("ff", 4), ("su", 256), ("sg", 256),
                    ("mq", 256), ("mk", 256), ("mv", 256), ("mg", 256), ("cq", 384), ("ckv", 128), ("kr", 32),
                    ("lg", 256)):
        parts[name] = w[:, o:o + n]
        o += n
    cols = [parts["ff"], z(MLA_NOPE - 4), parts["kr"], z(LANES - MLA_NOPE - MLA_ROPE),
            parts["ckv"], parts["cq"], parts["fq"], parts["fk"], parts["mq"], parts["mk"],
            parts["fg"], parts["su"], parts["sg"], parts["mg"], parts["lg"]]
    w_vt = jnp.concatenate([parts["fv"], parts["mv"]], axis=1).T
    return jnp.concatenate(cols, axis=1).astype(BF16), w_vt.astype(BF16)


def _fox_bias_base(hd):
    return (HEAD_DIM if hd % 2 == 0 else 0) + FOX_BIAS_LANES * (hd // 2)


def _fox_routing():
    import numpy as np
    route = np.zeros((3 * LANES, 2 * LANES), np.float32)
    const = np.zeros((1, 2 * LANES), np.float32)
    for hd in range(N_HEADS):
        base = _fox_bias_base(hd)
        for part in range(3):
            route[part * LANES + hd, base + part] = 1.0
            const[0, base + 3 + part] = 1.0
            const[0, LANES + base + part] = 1.0
            route[part * LANES + hd, LANES + base + 3 + part] = -1.0
    return jnp.asarray(route, BF16), jnp.asarray(const, F32)


def _rope_tables(seq):
    pos = jnp.arange(seq).astype(F32)[:, None]
    lane = jnp.arange(LANES)
    half = HEAD_DIM // 2
    inv = jnp.power(ROPE_THETA, -jnp.arange(half, dtype=F32) / half)
    ang = pos * inv[None, :]
    cos, sin = jnp.cos(ang)[:, lane % half], jnp.sin(ang)[:, lane % half]
    lo = (lane % HEAD_DIM) < half
    moba = (cos, jnp.where(lo, -sin, 0.0), jnp.where(lo, 0.0, sin))
    half = MLA_ROPE // 2
    inv = jnp.power(ROPE_THETA, -jnp.arange(half, dtype=F32) / half)
    ang = pos * inv[None, :]
    cos, sin = jnp.cos(ang)[:, lane % half], jnp.sin(ang)[:, lane % half]
    in_lo = (lane >= MLA_NOPE) & (lane < MLA_NOPE + half)
    in_hi = (lane >= MLA_NOPE + half) & (lane < MLA_NOPE + MLA_ROPE)
    mla = (jnp.where(in_lo | in_hi, cos, 1.0), jnp.where(in_lo, -sin, 0.0), jnp.where(in_hi, sin, 0.0))
    return moba, mla


def _arrange_mla(w_uq, w_ukv):
    z = lambda r, n: jnp.zeros((r, n), F32)
    dq = MLA_NOPE + MLA_ROPE
    q_cols, k_cols, v_cols = [], [], []
    for hd in range(N_HEADS):
        q_cols += [w_uq[:, hd * dq:(hd + 1) * dq], z(MLA_Q_RANK, LANES - dq)]
        base = hd * (MLA_NOPE + MLA_V)
        k_cols += [w_ukv[:, base:base + MLA_NOPE], z(MLA_KV_RANK, LANES - MLA_NOPE)]
        v_cols += [w_ukv[:, base + MLA_NOPE:base + MLA_NOPE + MLA_V]]
    return (jnp.concatenate(q_cols, axis=1).astype(BF16), jnp.concatenate(k_cols, axis=1).astype(BF16),
            jnp.concatenate(v_cols, axis=1).T.astype(BF16))


def _s5_matrices(a_re, a_im, log_dt, b_re, b_im, c_re, c_im):
    lam = lax.complex(a_re.astype(F32), a_im.astype(F32))
    dt = jnp.exp(log_dt.astype(F32))[:, None]
    lam_bar = jnp.exp(lam * dt)
    b_bar = ((lam_bar - 1.0) / lam)[..., None] * lax.complex(b_re.astype(F32), b_im.astype(F32))
    eye = jnp.eye(S5_GROUPS, dtype=F32)
    blockdiag_in = lambda t: jnp.einsum('gpc,gh->gchp', t, eye).reshape(WIDTH, S5_NSTATE)
    blockdiag_out = lambda t: jnp.einsum('gcp,gh->gphc', t, eye).reshape(S5_NSTATE, WIDTH)
    bm = jnp.concatenate([blockdiag_in(b_bar.real), blockdiag_in(b_bar.imag)], axis=1).astype(BF16)
    cm = jnp.concatenate([blockdiag_out(c_re.astype(F32)), -blockdiag_out(c_im.astype(F32))], axis=0).astype(BF16)
    lam2 = jnp.stack([lam_bar.real.reshape(S5_NSTATE), lam_bar.imag.reshape(S5_NSTATE)], axis=0)
    return bm, lam2, cm


def _full(shape):
    return pl.BlockSpec(shape, lambda *_: (0,) * len(shape))


def _in_call(x, g, w, fb, tri, route, bconst, moba_t, mla_t, gq, gkv, wuq, wukv, wvt, wuvt, tm):
    nb, seq, _ = x.shape
    tok = lambda width: pl.BlockSpec((1, tm, width), lambda b, s: (b, s, 0))
    tab = pl.BlockSpec((tm, LANES), lambda b, s: (s, 0))
    bf = lambda width: jax.ShapeDtypeStruct((nb, seq, width), BF16)
    f32 = lambda width: jax.ShapeDtypeStruct((nb, seq, width), F32)
    val_t = jax.ShapeDtypeStruct((nb, seq // tm, WIDTH, tm), BF16)
    val_t_spec = pl.BlockSpec((1, 1, WIDTH, tm), lambda b, s: (b, s, 0, 0))
    out_shape = (bf(512), bf(512), val_t, f32(256),
                 jax.ShapeDtypeStruct((seq, nb * WIDTH), F32), f32(256),
                 bf(512), bf(512), val_t, f32(256),
                 bf(512), bf(512), val_t, f32(256),
                 jax.ShapeDtypeStruct((nb, seq // tm, 8, LANES), F32))
    out_specs = (tok(512), tok(512), val_t_spec, tok(256),
                 pl.BlockSpec((tm, WIDTH), lambda b, s: (s, b)), tok(256),
                 tok(512), tok(512), val_t_spec, tok(256),
                 tok(512), tok(512), val_t_spec, tok(256),
                 pl.BlockSpec((1, 1, 8, LANES), lambda b, s: (b, s, 0, 0)))
    in_specs = [tok(D_MODEL), _full((1, D_MODEL)), _full((D_MODEL, N_COLS)), _full((1, LANES)),
                _full((tm, tm)), _full((3 * LANES, 2 * LANES)), _full((1, 2 * LANES)),
                tab, tab, tab, tab, tab, tab,
                _full((1, MLA_Q_RANK)), _full((1, MLA_KV_RANK)),
                _full((MLA_Q_RANK, N_HEADS * LANES)), _full((MLA_KV_RANK, N_HEADS * LANES)),
                _full((2 * WIDTH, D_MODEL)), _full((WIDTH, MLA_KV_RANK))]
    return pl.pallas_call(
        functools.partial(_in_kernel, tm=tm),
        grid=(nb, seq // tm),
        in_specs=in_specs, out_specs=out_specs, out_shape=out_shape,
        scratch_shapes=[pltpu.VMEM((1, LANES), F32), pltpu.VMEM((LANES, WIDTH), F32)],
        compiler_params=pltpu.CompilerParams(dimension_semantics=("arbitrary", "arbitrary"),
                                             vmem_limit_bytes=VMEM_LIMIT),
        name="in_proj",
    )(x, g, w, fb, tri, route, bconst, *moba_t, *mla_t, gq, gkv, wuq, wukv, wvt, wuvt)


def _first_blocks(nb, nq, fstat=None):
    tiles = jnp.arange(nq)
    past = tiles[None, :] < tiles[:, None]
    if fstat is None:
        need = jnp.broadcast_to(past, (nb, 2, nq, nq))
    else:
        qn, kn = fstat[:, :, 0, :N_HEADS] * NORM_SLACK, fstat[:, :, 1, :N_HEADS] * NORM_SLACK
        cmax, cmin = fstat[:, :, 2, :N_HEADS], fstat[:, :, 3, :N_HEADS]
        gap = (qn[:, :, None] * kn[:, None, :] + cmax[:, :, None] - cmin[:, None, :]
               + (qn * kn)[:, :, None])
        need = (gap > -SKIP_GAP) & past[None, :, :, None]
        need = need.reshape(nb, nq, nq, 2, 2).any(-1).transpose(0, 3, 1, 2)
    first = jnp.where(need.any(-1), jnp.argmax(need, axis=-1), nq)
    first = jnp.minimum(first, jnp.maximum(tiles - 1, 0))
    count = jnp.sum(tiles - first, axis=-1, keepdims=True)
    return jnp.concatenate([first, count], axis=-1).astype(jnp.int32).reshape(-1)


def _attn_call(first_blocks, q, k, vt, gate, t, name):
    nb, seq, _ = q.shape
    whole = lambda width: pl.BlockSpec((1, seq, width), lambda b, p, tbl: (b, 0, p))
    return pl.pallas_call(
        functools.partial(_attn_kernel, t=t, nq=seq // t),
        grid_spec=pltpu.PrefetchScalarGridSpec(
            num_scalar_prefetch=1, grid=(nb, 2),
            in_specs=[whole(2 * LANES), whole(2 * LANES),
                      pl.BlockSpec((1, seq // t, LANES, t), lambda b, p, tbl: (b, 0, p, 0)), whole(LANES)],
            out_specs=whole(LANES),
            scratch_shapes=[pltpu.VMEM((2, t, t), F32), pltpu.VMEM((2, t, t), F32),
                            pltpu.VMEM((2, 1, t), F32), pltpu.VMEM((2, 1, t), F32),
                            pltpu.VMEM((seq // t, 2, 1, t), F32),
                            pltpu.VMEM((seq // t, 2, HEAD_DIM + DEN_ROWS, t), F32),
                            pltpu.VMEM((seq // t, 2, LANES, t), BF16),
                            pltpu.SMEM((2,), jnp.int32)]),
        out_shape=jax.ShapeDtypeStruct((nb, seq, WIDTH), BF16),
        compiler_params=pltpu.CompilerParams(dimension_semantics=("arbitrary", "arbitrary"),
                                             vmem_limit_bytes=VMEM_LIMIT),
        name=name,
    )(first_blocks, q, k, vt, gate)


def _s5_call(u, bm, lam2, cm, d, gw, gb, steps, nbatch):
    seq = u.shape[0]
    blk = steps * nbatch
    nchunks = seq // steps
    prev = lambda c: (jnp.maximum(c - 1, 0), 0)
    return pl.pallas_call(
        functools.partial(_s5_kernel, steps=steps, nbatch=nbatch, nchunks=nchunks),
        grid=(nchunks + 1,),
        in_specs=[pl.BlockSpec((steps, nbatch * WIDTH), lambda c: (jnp.minimum(c, nchunks - 1), 0)),
                  _full((WIDTH, 2 * S5_NSTATE)), _full((2, S5_NSTATE)), _full((2 * S5_NSTATE, WIDTH)),
                  _full((1, WIDTH)), _full((WIDTH, WIDTH)), _full((1, WIDTH))],
        out_specs=pl.BlockSpec((steps, nbatch * WIDTH), prev),
        out_shape=jax.ShapeDtypeStruct((seq, nbatch * WIDTH), F32),
        scratch_shapes=[pltpu.VMEM((blk, 2 * S5_NSTATE), F32), pltpu.VMEM((blk, 2 * S5_NSTATE), F32),
                        *[pltpu.VMEM((WIDTH // LANES, blk, LANES), F32)] * 3,
                        pltpu.VMEM((nbatch, 2 * S5_NSTATE), F32)],
        compiler_params=pltpu.CompilerParams(dimension_semantics=("arbitrary",),
                                             vmem_limit_bytes=VMEM_LIMIT),
        name="s5",
    )(u, bm, lam2, cm, d, gw, gb)


def _out_call(x, a, b2d, sg, c, d, w, fg, tm, final):
    nb, seq, _ = x.shape
    tok = lambda width: pl.BlockSpec((1, tm, width), lambda b, s: (b, s, 0))
    return pl.pallas_call(
        functools.partial(_out_kernel, final=final),
        grid=(nb, seq // tm),
        in_specs=[tok(D_MODEL), tok(WIDTH), pl.BlockSpec((tm, WIDTH), lambda b, s: (s, b)), tok(WIDTH),
                  tok(WIDTH), tok(WIDTH), _full((4 * WIDTH, D_MODEL)), _full((1, D_MODEL))],
        out_specs=tok(D_MODEL),
        out_shape=jax.ShapeDtypeStruct(x.shape, F32),
        compiler_params=pltpu.CompilerParams(dimension_semantics=("arbitrary", "arbitrary"),
                                             vmem_limit_bytes=VMEM_LIMIT),
        name="out_proj",
    )(x, a, b2d, sg, c, d, w, fg)


def kernel(x, norm_g, w_in, fox_fb, s5_a_re, s5_a_im, s5_log_dt, s5_b_re, s5_b_im, s5_c_re, s5_c_im, s5_d,
           s5_glu_w, s5_glu_b, mla_q_norm, mla_w_uq, mla_kv_norm, mla_w_ukv, w_out, final_g):
    nb, seq, _ = x.shape
    depth = norm_g.shape[0]
    assert nb == 8, "the S5 recurrence keeps the batch on the 8 sublanes of a vreg"
    tm, tm_out, steps = _tiles(seq)
    assert seq % tm == 0 and seq % steps == 0 and seq // MOBA_BLOCK <= 32

    tri = jnp.tril(jnp.ones((tm, tm), F32)).astype(BF16)
    route, bconst = _fox_routing()
    moba_t, mla_t = _rope_tables(seq)
    fgain = final_g.astype(F32).reshape(1, D_MODEL)
    every_block = _first_blocks(nb, seq // tm)

    for l in range(depth):
        w, wvt = _arrange_w_in(w_in[l])
        fb = jnp.zeros((1, LANES), F32).at[0, :N_HEADS].set(fox_fb[l].astype(F32))
        wuq, wukv, wuvt = _arrange_mla(mla_w_uq[l].astype(F32), mla_w_ukv[l].astype(F32))
        (foxq, foxk, foxv, fg, su, sg, mobq, mobk, mobv, mg, mlaq, mlak, mlav, lg, fstat) = _in_call(
            x, norm_g[l].astype(F32).reshape(1, D_MODEL), w, fb, tri, route, bconst, moba_t, mla_t,
            mla_q_norm[l].astype(F32).reshape(1, MLA_Q_RANK), mla_kv_norm[l].astype(F32).reshape(1, MLA_KV_RANK),
            wuq, wukv, wvt, wuvt, tm)

        a_out = _attn_call(_first_blocks(nb, seq // tm, fstat), foxq, foxk, foxv, fg, tm, "fox_attn")
        c_out = _attn_call(every_block, mobq, mobk, mobv, mg, tm, "moba_attn")
        d_out = _attn_call(every_block, mlaq, mlak, mlav, lg, tm, "mla_attn")

        bm, lam2, cm = _s5_matrices(s5_a_re[l], s5_a_im[l], s5_log_dt[l], s5_b_re[l], s5_b_im[l],
                                    s5_c_re[l], s5_c_im[l])
        b_out = _s5_call(su, bm, lam2, cm, s5_d[l].astype(F32).reshape(1, WIDTH), s5_glu_w[l].astype(BF16),
                         s5_glu_b[l].astype(F32).reshape(1, WIDTH), steps, nb)
        x = _out_call(x, a_out, b_out, sg, c_out, d_out, w_out[l].astype(BF16), fgain, tm_out,
                      final=(l == depth - 1))
    return x
```

```python
import functools

import jax
import jax.numpy as jnp
from jax import lax
from jax.experimental import pallas as pl
from jax.experimental.pallas import tpu as pltpu

F32 = jnp.float32
BF16 = jnp.bfloat16

D_MODEL = 1024
HEAD_DIM = 64
DEN_ROWS = 16
PAST_UNROLL = 16
SKIP_GAP = 152.0
NORM_SLACK = 1.01
N_HEADS = 4
WIDTH = 256
S5_GROUPS = 16
S5_GROUP = 16
S5_STATE = 64
S5_NSTATE = S5_GROUPS * S5_STATE
MOBA_BLOCK = 256
MOBA_TOPK = 3
MAX_BLOCKS = 32
GROUP_HEAD = (1, 3, 0, 2)
LOG2E = 1.4426950408889634
MLA_NOPE = 64
MLA_ROPE = 32
MLA_V = 64
MLA_Q_RANK = 384
MLA_KV_RANK = 128
ROPE_THETA = 10000.0
EPS = 1e-6
NEG = -1e30
LANES = 128
SUBLANES = 8
LOG2_HEAD_DIM = HEAD_DIM.bit_length() - 1
LOG2_MOBA_BLOCK = MOBA_BLOCK.bit_length() - 1
LOG2_MAX_BLOCKS = MAX_BLOCKS.bit_length() - 1
VMEM_LIMIT = 56 * 1024 * 1024

C_FFKR = 0
C_CKV, C_CQ = 128, 256
C_FQ, C_FK = 640, 896
C_MQ, C_MK = 1152, 1408
C_FG, C_SU, C_SG, C_MG, C_LG = 1664, 1920, 2176, 2432, 2688
N_COLS = 2944
PROJ_GROUPS = ((0, 1152), (1152, 512), (1664, 1280))
FOX_BIAS_LANES = 6


def _split3(x):
    x1 = x.astype(BF16)
    r1 = x - x1.astype(F32)
    x2 = r1.astype(BF16)
    r2 = r1 - x2.astype(F32)
    return x1, x2, r2.astype(BF16)


def _rope(x, cos, sin_lo, sin_hi, half):
    return (x * cos + pltpu.roll(x, LANES - half, 1) * sin_lo + pltpu.roll(x, half, 1) * sin_hi)


def _rms(x, g):
    return x * lax.rsqrt(jnp.mean(x * x, axis=-1, keepdims=True) + EPS) * g


def _silu(g):
    return g * (1.0 / (1.0 + jnp.exp(-g)))


def _in_kernel(x_ref, g_ref, w_ref, fb_ref, tri_ref, route_ref, bconst_ref,
               mc_ref, msl_ref, msh_ref, lc_ref, lsl_ref, lsh_ref,
               gq_ref, gkv_ref, wuq_ref, wukv_ref, wvt_ref, wuvt_ref,
               foxq_ref, foxk_ref, foxv_ref, fg_ref, su_ref, sg_ref,
               mobq_ref, mobk_ref, mobv_ref, mg_ref,
               mlaq_ref, mlak_ref, mlav_ref, lg_ref, fstat_ref,
               carry_ref, km_ref, *, tm):
    sblk = pl.program_id(1)
    nblk = tm // MOBA_BLOCK

    @pl.when(sblk == 0)
    def _():
        carry_ref[...] = jnp.zeros_like(carry_ref)
        km_ref[...] = jnp.zeros_like(km_ref)

    h = _rms(x_ref[0], g_ref[...]).astype(BF16)

    z = {}

    def proj(c0, width):
        for g0, gw in PROJ_GROUPS:
            if g0 <= c0 and c0 + width <= g0 + gw:
                if g0 not in z:
                    z[g0] = jnp.dot(h, w_ref[:, g0:g0 + gw], preferred_element_type=F32)
                return z[g0][:, c0 - g0:c0 - g0 + width]
        raise ValueError("column range crosses a projection group")

    lane = lax.broadcasted_iota(jnp.int32, (tm, LANES), 1)
    row = lax.broadcasted_iota(jnp.int32, (tm, LANES), 0)
    low_half = lane < HEAD_DIM

    def proj_t(wt):
        return lax.dot_general(wt, h, (((1,), (1,)), ((), ())), preferred_element_type=F32)

    for g0, gw in PROJ_GROUPS:
        proj(g0, gw)
    fg_ref[0] = proj(C_FG, WIDTH)
    sg_ref[0] = proj(C_SG, WIDTH)
    mg_ref[0] = proj(C_MG, WIDTH)
    lg_ref[0] = proj(C_LG, WIDTH)
    su_ref[...] = proj(C_SU, WIDTH)

    ffkr = proj(C_FFKR, LANES)
    ff = ffkr + fb_ref[...]
    logf = -(jnp.maximum(-ff, 0.0) + jnp.log1p(jnp.exp(-jnp.abs(ff))))
    within3 = jnp.dot(tri_ref[...], jnp.concatenate(_split3(logf), axis=1), preferred_element_type=F32)
    within = within3[:, 0:LANES] + within3[:, LANES:2 * LANES] + within3[:, 2 * LANES:]
    cum = within + carry_ref[...]
    carry_ref[...] = cum[tm - 1:tm, :]
    cum2 = cum * LOG2E
    routed = jnp.dot(jnp.concatenate(_split3(cum2), axis=1), route_ref[...],
                     preferred_element_type=F32) + bconst_ref[...]
    fq = proj(C_FQ, WIDTH) * (HEAD_DIM ** -0.5 * LOG2E)
    fk = proj(C_FK, WIDTH)
    lane_row = lane[0:1, :]
    qmax_row = jnp.zeros((1, LANES), F32)
    kmax_row = jnp.zeros((1, LANES), F32)
    for hd in range(N_HEADS):
        pair = hd // 2
        own = low_half if hd % 2 == 0 else jnp.logical_not(low_half)
        base = _fox_bias_base(hd)
        mine = (lane >= base) & (lane < base + FOX_BIAS_LANES)
        sl = slice(pair * LANES, (pair + 1) * LANES)
        qb = fq[:, sl].astype(BF16)
        kb = fk[:, sl].astype(BF16)
        qa = jnp.where(own, qb, jnp.where(mine, routed[:, 0:LANES], 0.0).astype(BF16))
        ka = jnp.where(own, kb, jnp.where(mine, routed[:, LANES:], 0.0).astype(BF16))
        foxq_ref[0, :, hd * LANES:(hd + 1) * LANES] = qa
        foxk_ref[0, :, hd * LANES:(hd + 1) * LANES] = ka
        for rounded, is_q in ((qb, True), (kb, False)):
            r = jnp.where(own, rounded.astype(F32), 0.0)
            norm = jnp.sqrt(jnp.max(jnp.sum(r * r, axis=1, keepdims=True), axis=0, keepdims=True))
            if is_q:
                qmax_row = jnp.where(lane_row == hd, norm, qmax_row)
            else:
                kmax_row = jnp.where(lane_row == hd, norm, kmax_row)
    srow = lax.broadcasted_iota(jnp.int32, (SUBLANES, LANES), 0)
    fstat_ref[0, 0] = jnp.where(srow == 0, qmax_row, jnp.where(srow == 1, kmax_row, jnp.where(
        srow == 2, jnp.max(cum2, axis=0, keepdims=True), jnp.where(
            srow == 3, jnp.min(cum2, axis=0, keepdims=True), 0.0))))

    mc, msl, msh = mc_ref[...], msl_ref[...], msh_ref[...]
    mq = proj(C_MQ, WIDTH)
    mk = proj(C_MK, WIDTH)
    q_r = [_rope(mq[:, p * LANES:(p + 1) * LANES], mc, msl, msh, HEAD_DIM // 2) for p in range(2)]
    k_r = [_rope(mk[:, p * LANES:(p + 1) * LANES], mc, msl, msh, HEAD_DIM // 2) for p in range(2)]
    km_row = lax.broadcasted_iota(jnp.int32, (LANES, WIDTH), 0)
    km_lane = lax.broadcasted_iota(jnp.int32, (LANES, WIDTH), 1)
    km_grp = lax.shift_right_logical(km_row, LOG2_MAX_BLOCKS)
    km_head = jnp.where(km_grp == 0, GROUP_HEAD[0], jnp.where(km_grp == 1, GROUP_HEAD[1],
                        jnp.where(km_grp == 2, GROUP_HEAD[2], GROUP_HEAD[3])))
    km_own = lax.shift_right_logical(km_lane, LOG2_HEAD_DIM) == km_head
    kmt = km_ref[...]
    for nb in range(nblk):
        blk = sblk * nblk + nb
        km = jnp.concatenate(
            [jnp.mean(k_r[p][nb * MOBA_BLOCK:(nb + 1) * MOBA_BLOCK, :], axis=0, keepdims=True) for p in range(2)],
            axis=1)
        kmt = jnp.where((km_row & (MAX_BLOCKS - 1)) == blk, jnp.where(km_own, km, 0.0), kmt)
    km_ref[...] = kmt
    gate_t = lax.dot_general(kmt, jnp.concatenate(q_r, axis=1), (((1,), (1,)), ((), ())),
                             precision=lax.Precision.HIGHEST, preferred_element_type=F32)
    cand = lax.broadcasted_iota(jnp.int32, (MAX_BLOCKS, tm), 0).astype(F32)
    tok = lax.broadcasted_iota(jnp.int32, (MAX_BLOCKS, tm), 1)
    blk_tok = (sblk * nblk + lax.shift_right_logical(tok, LOG2_MOBA_BLOCK)).astype(F32)
    bias_rows = []
    for grp in range(N_HEADS):
        g = jnp.where(cand < blk_tok, gate_t[grp * MAX_BLOCKS:(grp + 1) * MAX_BLOCKS, :], -jnp.inf)
        chosen = jnp.zeros((MAX_BLOCKS, tm), F32)
        for _ in range(MOBA_TOPK):
            m = jnp.max(g, axis=0, keepdims=True)
            first = jnp.min(jnp.where(g == m, cand, 1e9), axis=0, keepdims=True)
            first = jnp.where(m > -jnp.inf, first, -1.0)
            pick = cand == first
            chosen = jnp.where(pick, 1.0, chosen)
            g = jnp.where(pick, -jnp.inf, g)
        keep = jnp.where(cand == blk_tok, 1.0, chosen)
        bias_rows.append(jnp.where(keep > 0.0, 0.0, NEG))
    sel_bias = jnp.concatenate(bias_rows, axis=0).T
    blk_row = sblk * nblk + lax.shift_right_logical(row, LOG2_MOBA_BLOCK)
    onehot = jnp.where((lane & (MAX_BLOCKS - 1)) == blk_row, 1.0, 0.0)
    lane_grp = lax.shift_right_logical(lane, LOG2_MAX_BLOCKS)
    for hd in range(N_HEADS):
        pair = hd // 2
        own = low_half if hd % 2 == 0 else jnp.logical_not(low_half)
        mine = lane_grp == GROUP_HEAD.index(hd)
        qa = jnp.where(own, q_r[pair] * (HEAD_DIM ** -0.5 * LOG2E), jnp.where(mine, sel_bias, 0.0))
        ka = jnp.where(own, k_r[pair], jnp.where(mine, onehot, 0.0))
        mobq_ref[0, :, hd * LANES:(hd + 1) * LANES] = qa.astype(BF16)
        mobk_ref[0, :, hd * LANES:(hd + 1) * LANES] = ka.astype(BF16)

    lc, lsl, lsh = lc_ref[...], lsl_ref[...], lsh_ref[...]
    cqn = _rms(proj(C_CQ, MLA_Q_RANK), gq_ref[...]).astype(BF16)
    qf = jnp.dot(cqn, wuq_ref[...], preferred_element_type=F32)
    ckvn = _rms(proj(C_CKV, MLA_KV_RANK), gkv_ref[...]).astype(BF16)
    kv = jnp.dot(ckvn, wukv_ref[...], preferred_element_type=F32)
    kr = jnp.where(low_half, 0.0, _rope(ffkr, lc, lsl, lsh, MLA_ROPE // 2))
    scale = (MLA_NOPE + MLA_ROPE) ** -0.5 * LOG2E
    for hd in range(N_HEADS):
        sl = slice(hd * LANES, (hd + 1) * LANES)
        mlaq_ref[0, :, sl] = (_rope(qf[:, sl], lc, lsl, lsh, MLA_ROPE // 2) * scale).astype(BF16)
        mlak_ref[0, :, sl] = (kv[:, sl] + kr).astype(BF16)
    mlav_ref[0, 0] = lax.dot_general(wuvt_ref[...], ckvn, (((1,), (1,)), ((), ())),
                                     preferred_element_type=F32).astype(BF16)
    foxv_ref[0, 0] = proj_t(wvt_ref[0:WIDTH, :]).astype(BF16)
    mobv_ref[0, 0] = proj_t(wvt_ref[WIDTH:2 * WIDTH, :]).astype(BF16)


def _attn_kernel(tbl_ref, q_ref, k_ref, vt_ref, g_ref, o_ref, sa_ref, sb_ref, mxa_ref, mxb_ref, m_ref, acc_ref,
                 qt_ref, cur_ref, *, t, nq):
    bufs = ((sa_ref, mxa_ref), (sb_ref, mxb_ref))
    half = t // 2
    causal = (lax.broadcasted_iota(jnp.int32, (half, half), 0)
              <= lax.broadcasted_iota(jnp.int32, (half, half), 1))
    ones = jnp.ones((DEN_ROWS, t), BF16)
    pair = lambda i, j: (jnp.int32(i), jnp.int32(j))

    def transpose_tile(i):
        i = jnp.minimum(i, nq - 1)
        rows = pl.ds(pl.multiple_of(i * t, t), t)
        for hh in range(2):
            qt_ref[i, hh] = q_ref[0, rows, hh * LANES:(hh + 1) * LANES].astype(F32).T.astype(BF16)

    def scores(hh, ij, buf):
        s_ref, mx_ref = buf
        koff = pl.multiple_of(ij[1] * t, t)
        st = jnp.dot(k_ref[0, pl.ds(koff, t), hh * LANES:(hh + 1) * LANES], qt_ref[ij[0], hh],
                     preferred_element_type=F32)
        s_ref[hh] = st
        mx_ref[hh] = jnp.max(st, axis=0, keepdims=True)

    def values(hh, j):
        return jnp.concatenate([vt_ref[0, j, hh * HEAD_DIM:(hh + 1) * HEAD_DIM, :], ones], axis=0)

    def diag_scores(hh, i, buf):
        koff = pl.multiple_of(i * t, t)
        qt = qt_ref[i, hh]
        buf[0][hh, 0:half, :] = jnp.dot(k_ref[0, pl.ds(koff, half), hh * LANES:(hh + 1) * LANES], qt,
                                        preferred_element_type=F32)
        buf[0][hh, half:, half:] = jnp.dot(k_ref[0, pl.ds(koff + half, half), hh * LANES:(hh + 1) * LANES],
                                           qt[:, half:], preferred_element_type=F32)

    def first_update(hh, i, buf):
        lo = buf[0][hh, 0:half, :]
        lo_lo = jnp.where(causal, lo[:, 0:half], NEG)
        hi_hi = jnp.where(causal, buf[0][hh, half:, half:], NEG)
        mx_lo = jnp.max(lo_lo, axis=0, keepdims=True)
        mx_hi = jnp.maximum(jnp.max(lo[:, half:], axis=0, keepdims=True), jnp.max(hi_hi, axis=0, keepdims=True))
        p_lo = jnp.concatenate([jnp.exp2(lo_lo - mx_lo), jnp.exp2(lo[:, half:] - mx_hi)], axis=1).astype(BF16)
        p_hi = jnp.exp2(hi_hi - mx_hi).astype(BF16)
        va = values(hh, i)
        acc_lo = jnp.dot(va[:, 0:half], p_lo, preferred_element_type=F32)
        acc_hi = jnp.dot(va[:, half:], p_hi, preferred_element_type=F32)
        acc_ref[i, hh, :, 0:half] = acc_lo[:, 0:half]
        acc_ref[i, hh, :, half:] = acc_lo[:, half:] + acc_hi
        m_ref[i, hh] = jnp.concatenate([mx_lo, mx_hi], axis=1)

    def update(hh, ij, buf):
        i, j = ij
        m = m_ref[i, hh]
        m_new = jnp.maximum(m, buf[1][hh])
        alpha = jnp.exp2(m - m_new)
        p = jnp.exp2(buf[0][hh] - m_new).astype(BF16)
        acc_ref[i, hh] = alpha * acc_ref[i, hh] + jnp.dot(values(hh, j), p, preferred_element_type=F32)
        m_ref[i, hh] = m_new

    def stage(nxt, nxt_buf, cur, cur_buf, consume):
        for hh in range(2):
            if nxt is not None:
                scores(hh, nxt, nxt_buf)
            consume(hh, cur, cur_buf)

    tbl = (pl.program_id(0) * 2 + pl.program_id(1)) * (nq + 1)
    first_block = lambda i: tbl_ref[tbl + i]
    n_past = tbl_ref[tbl + nq]
    first_past = pair(min(1, nq - 1), 0)

    def diag_stage(nxt_diag, nxt_past, nxt_buf, cur, cur_buf):
        for hh in range(2):
            if nxt_diag is not None:
                diag_scores(hh, nxt_diag, nxt_buf)
            if nxt_past is not None:
                scores(hh, nxt_past, nxt_buf)
            first_update(hh, cur, cur_buf)

    for i in range(min(3, nq)):
        transpose_tile(jnp.int32(i))
    for hh in range(2):
        diag_scores(hh, jnp.int32(0), bufs[0])

    def two_diagonals(n, carry):
        i = 2 * n
        diag_stage(i + 1, None, bufs[1], i, bufs[0])
        diag_stage(i + 2, None, bufs[0], i + 1, bufs[1])
        transpose_tile(i + 3)
        transpose_tile(i + 4)
        return carry

    looped = 2 * ((nq - 1) // 2)
    lax.fori_loop(0, looped // 2, two_diagonals, 0)
    for i in range(looped, nq):
        last = i == nq - 1
        diag_stage(None if last else jnp.int32(i + 1), first_past if last and nq > 1 else None,
                   bufs[(i + 1) % 2], jnp.int32(i), bufs[i % 2])
    par = nq % 2

    def succ(ij):
        i, j = ij
        last = j == i - 1
        up = jnp.minimum(i + 1, nq - 1)
        return jnp.where(last, up, i), jnp.where(last, first_block(up), j + 1)

    def past_steps(count, cur):
        for k in range(count):
            nxt = succ(cur)
            stage(nxt, bufs[(par + k + 1) % 2], cur, bufs[(par + k) % 2], update)
            cur = nxt
        return cur

    cur = lax.fori_loop(0, n_past // PAST_UNROLL, lambda _, c: past_steps(PAST_UNROLL, c), first_past)
    cur_ref[0], cur_ref[1] = cur
    count = PAST_UNROLL // 2
    while count:
        @pl.when((n_past & count) != 0)
        def _(count=count):
            cur_ref[0], cur_ref[1] = past_steps(count, (cur_ref[0], cur_ref[1]))
        count //= 2

    def finish_tile(i, carry):
        out_t = jnp.concatenate([acc_ref[i, hh, 0:HEAD_DIM] / acc_ref[i, hh, HEAD_DIM:HEAD_DIM + 1]
                                 for hh in range(2)], axis=0)
        rows = pl.ds(pl.multiple_of(i * t, t), t)
        o_ref[0, rows, :] = (out_t.T * _silu(g_ref[0, rows, :])).astype(BF16)
        return carry

    lax.fori_loop(0, nq, finish_tile, 0)


def _s5_kernel(u_ref, bm_ref, lam_ref, cm_ref, d_ref, gw_ref, gb_ref, o_ref,
               xa_ref, xb_ref, ua_ref, ub_ref, y_ref, st_ref, *, steps, nbatch, nchunks):
    c = pl.program_id(0)

    @pl.when(c == 0)
    def _():
        st_ref[...] = jnp.zeros_like(st_ref)
        xb_ref[...] = jnp.zeros_like(xb_ref)
        ub_ref[...] = jnp.zeros_like(ub_ref)

    lam_re = jnp.broadcast_to(lam_ref[0:1, :], (nbatch, S5_NSTATE))
    lam_im = jnp.broadcast_to(lam_ref[1:2, :], (nbatch, S5_NSTATE))

    halves = WIDTH // LANES
    wide = lambda ref: jnp.concatenate([ref[hf] for hf in range(halves)], axis=1)

    def chunk(x_cur, x_prev, u_cur, u_prev):
        for b in range(nbatch):
            for hf in range(halves):
                lanes = slice(b * WIDTH + hf * LANES, b * WIDTH + (hf + 1) * LANES)
                u_cur[hf, pl.ds(b, steps, stride=nbatch), :] = u_ref[:, lanes]
        x_cur[...] = jnp.dot(wide(u_cur).astype(BF16), bm_ref[...], preferred_element_type=F32)
        y = jnp.dot(x_prev[...].astype(BF16), cm_ref[...], preferred_element_type=F32) + d_ref[...] * wide(u_prev)
        y = 0.5 * y * (1.0 + jnp.tanh(0.7978845608028654 * (y + 0.044715 * (y * y * y))))
        z = jnp.dot(y.astype(BF16), gw_ref[...], preferred_element_type=F32) + gb_ref[...]
        y = y * (1.0 / (1.0 + jnp.exp(-z)))
        for hf in range(halves):
            y_ref[hf] = y[:, hf * LANES:(hf + 1) * LANES]
        for b in range(nbatch):
            for hf in range(halves):
                lanes = slice(b * WIDTH + hf * LANES, b * WIDTH + (hf + 1) * LANES)
                o_ref[:, lanes] = y_ref[hf, pl.ds(b, steps, stride=nbatch), :]

        def body(t, carry):
            xr, xi = carry
            off = pl.multiple_of(t * nbatch, nbatch)
            nr = lam_re * xr - lam_im * xi + x_cur[pl.ds(off, nbatch), 0:S5_NSTATE]
            ni = lam_re * xi + lam_im * xr + x_cur[pl.ds(off, nbatch), S5_NSTATE:2 * S5_NSTATE]
            x_cur[pl.ds(off, nbatch), 0:S5_NSTATE] = nr
            x_cur[pl.ds(off, nbatch), S5_NSTATE:2 * S5_NSTATE] = ni
            return nr, ni

        @pl.when(c < nchunks)
        def _():
            xr, xi = lax.fori_loop(0, steps, body, (st_ref[:, 0:S5_NSTATE], st_ref[:, S5_NSTATE:2 * S5_NSTATE]))
            st_ref[:, 0:S5_NSTATE] = xr
            st_ref[:, S5_NSTATE:2 * S5_NSTATE] = xi

    @pl.when(c % 2 == 0)
    def _():
        chunk(xa_ref, xb_ref, ua_ref, ub_ref)

    @pl.when(c % 2 == 1)
    def _():
        chunk(xb_ref, xa_ref, ub_ref, ua_ref)


def _out_kernel(x_ref, a_ref, b_ref, sg_ref, c_ref, d_ref, w_ref, fg_ref, o_ref, *, final):
    b = (b_ref[...] * _silu(sg_ref[0])).astype(BF16)
    y = (jnp.dot(a_ref[0], w_ref[0:WIDTH, :], preferred_element_type=F32)
         + jnp.dot(b, w_ref[WIDTH:2 * WIDTH, :], preferred_element_type=F32)
         + jnp.dot(c_ref[0], w_ref[2 * WIDTH:3 * WIDTH, :], preferred_element_type=F32)
         + jnp.dot(d_ref[0], w_ref[3 * WIDTH:4 * WIDTH, :], preferred_element_type=F32))
    xn = x_ref[0] + y
    if final:
        xn = _rms(xn, fg_ref[...])
    o_ref[0] = xn


def _tiles(seq):
    tm = 512 if seq % 512 == 0 else MOBA_BLOCK
    tm_out = 1024 if seq % 1024 == 0 else tm
    steps = 128
    return tm, tm_out, steps


def _arrange_w_in(w):
    z = lambda n: jnp.zeros((D_MODEL, n), w.dtype)
    o = 0
    parts = {}
    for name, n in (("fq", 256), ("fk", 256), ("fv", 256), ("fg", 256), ("ff", 4), ("su", 256), ("sg", 256),
                    ("mq", 256), ("mk", 256), ("mv", 256), ("mg", 256), ("cq", 384), ("ckv", 128), ("kr", 32),
                    ("lg", 256)):
        parts[name] = w[:, o:o + n]
        o += n
    cols = [parts["ff"], z(MLA_NOPE - 4), parts["kr"], z(LANES - MLA_NOPE - MLA_ROPE),
            parts["ckv"], parts["cq"], parts["fq"], parts["fk"], parts["mq"], parts["mk"],
            parts["fg"], parts["su"], parts["sg"], parts["mg"], parts["lg"]]
    w_vt = jnp.concatenate([parts["fv"], parts["mv"]], axis=1).T
    return jnp.concatenate(cols, axis=1).astype(BF16), w_vt.astype(BF16)


def _fox_bias_base(hd):
    return (HEAD_DIM if hd % 2 == 0 else 0) + FOX_BIAS_LANES * (hd // 2)


def _fox_routing():
    import numpy as np
    route = np.zeros((3 * LANES, 2 * LANES), np.float32)
    const = np.zeros((1, 2 * LANES), np.float32)
    for hd in range(N_HEADS):
        base = _fox_bias_base(hd)
        for part in range(3):
            route[part * LANES + hd, base + part] = 1.0
            const[0, base + 3 + part] = 1.0
            const[0, LANES + base + part] = 1.0
            route[part * LANES + hd, LANES + base + 3 + part] = -1.0
    return jnp.asarray(route, BF16), jnp.asarray(const, F32)


def _rope_tables(seq):
    pos = jnp.arange(seq).astype(F32)[:, None]
    lane = jnp.arange(LANES)
    half = HEAD_DIM // 2
    inv = jnp.power(ROPE_THETA, -jnp.arange(half, dtype=F32) / half)
    ang = pos * inv[None, :]
    cos, sin = jnp.cos(ang)[:, lane % half], jnp.sin(ang)[:, lane % half]
    lo = (lane % HEAD_DIM) < half
    moba = (cos, jnp.where(lo, -sin, 0.0), jnp.where(lo, 0.0, sin))
    half = MLA_ROPE // 2
    inv = jnp.power(ROPE_THETA, -jnp.arange(half, dtype=F32) / half)
    ang = pos * inv[None, :]
    cos, sin = jnp.cos(ang)[:, lane % half], jnp.sin(ang)[:, lane % half]
    in_lo = (lane >= MLA_NOPE) & (lane < MLA_NOPE + half)
    in_hi = (lane >= MLA_NOPE + half) & (lane < MLA_NOPE + MLA_ROPE)
    mla = (jnp.where(in_lo | in_hi, cos, 1.0), jnp.where(in_lo, -sin, 0.0), jnp.where(in_hi, sin, 0.0))
    return moba, mla


def _arrange_mla(w_uq, w_ukv):
    z = lambda r, n: jnp.zeros((r, n), F32)
    dq = MLA_NOPE + MLA_ROPE
    q_cols, k_cols, v_cols = [], [], []
    for hd in range(N_HEADS):
        q_cols += [w_uq[:, hd * dq:(hd + 1) * dq], z(MLA_Q_RANK, LANES - dq)]
        base = hd * (MLA_NOPE + MLA_V)
        k_cols += [w_ukv[:, base:base + MLA_NOPE], z(MLA_KV_RANK, LANES - MLA_NOPE)]
        v_cols += [w_ukv[:, base + MLA_NOPE:base + MLA_NOPE + MLA_V]]
    return (jnp.concatenate(q_cols, axis=1).astype(BF16), jnp.concatenate(k_cols, axis=1).astype(BF16),
            jnp.concatenate(v_cols, axis=1).T.astype(BF16))


def _s5_matrices(a_re, a_im, log_dt, b_re, b_im, c_re, c_im):
    lam = lax.complex(a_re.astype(F32), a_im.astype(F32))
    dt = jnp.exp(log_dt.astype(F32))[:, None]
    lam_bar = jnp.exp(lam * dt)
    b_bar = ((lam_bar - 1.0) / lam)[..., None] * lax.complex(b_re.astype(F32), b_im.astype(F32))
    eye = jnp.eye(S5_GROUPS, dtype=F32)
    blockdiag_in = lambda t: jnp.einsum('gpc,gh->gchp', t, eye).reshape(WIDTH, S5_NSTATE)
    blockdiag_out = lambda t: jnp.einsum('gcp,gh->gphc', t, eye).reshape(S5_NSTATE, WIDTH)
    bm = jnp.concatenate([blockdiag_in(b_bar.real), blockdiag_in(b_bar.imag)], axis=1).astype(BF16)
    cm = jnp.concatenate([blockdiag_out(c_re.astype(F32)), -blockdiag_out(c_im.astype(F32))], axis=0).astype(BF16)
    lam2 = jnp.stack([lam_bar.real.reshape(S5_NSTATE), lam_bar.imag.reshape(S5_NSTATE)], axis=0)
    return bm, lam2, cm


def _full(shape):
    return pl.BlockSpec(shape, lambda *_: (0,) * len(shape))


def _in_call(x, g, w, fb, tri, route, bconst, moba_t, mla_t, gq, gkv, wuq, wukv, wvt, wuvt, tm):
    nb, seq, _ = x.shape
    tok = lambda width: pl.BlockSpec((1, tm, width), lambda b, s: (b, s, 0))
    tab = pl.BlockSpec((tm, LANES), lambda b, s: (s, 0))
    bf = lambda width: jax.ShapeDtypeStruct((nb, seq, width), BF16)
    f32 = lambda width: jax.ShapeDtypeStruct((nb, seq, width), F32)
    val_t = jax.ShapeDtypeStruct((nb, seq // tm, WIDTH, tm), BF16)
    val_t_spec = pl.BlockSpec((1, 1, WIDTH, tm), lambda b, s: (b, s, 0, 0))
    out_shape = (bf(512), bf(512), val_t, f32(256),
                 jax.ShapeDtypeStruct((seq, nb * WIDTH), F32), f32(256),
                 bf(512), bf(512), val_t, f32(256),
                 bf(512), bf(512), val_t, f32(256),
                 jax.ShapeDtypeStruct((nb, seq // tm, SUBLANES, LANES), F32))
    out_specs = (tok(512), tok(512), val_t_spec, tok(256),
                 pl.BlockSpec((tm, WIDTH), lambda b, s: (s, b)), tok(256),
                 tok(512), tok(512), val_t_spec, tok(256),
                 tok(512), tok(512), val_t_spec, tok(256),
                 pl.BlockSpec((1, 1, SUBLANES, LANES), lambda b, s: (b, s, 0, 0)))
    in_specs = [tok(D_MODEL), _full((1, D_MODEL)), _full((D_MODEL, N_COLS)), _full((1, LANES)),
                _full((tm, tm)), _full((3 * LANES, 2 * LANES)), _full((1, 2 * LANES)),
                tab, tab, tab, tab, tab, tab,
                _full((1, MLA_Q_RANK)), _full((1, MLA_KV_RANK)),
                _full((MLA_Q_RANK, N_HEADS * LANES)), _full((MLA_KV_RANK, N_HEADS * LANES)),
                _full((2 * WIDTH, D_MODEL)), _full((WIDTH, MLA_KV_RANK))]
    return pl.pallas_call(
        functools.partial(_in_kernel, tm=tm),
        grid=(nb, seq // tm),
        in_specs=in_specs, out_specs=out_specs, out_shape=out_shape,
        scratch_shapes=[pltpu.VMEM((1, LANES), F32), pltpu.VMEM((LANES, WIDTH), F32)],
        compiler_params=pltpu.CompilerParams(dimension_semantics=("arbitrary", "arbitrary"),
                                             vmem_limit_bytes=VMEM_LIMIT),
        name="in_proj",
    )(x, g, w, fb, tri, route, bconst, *moba_t, *mla_t, gq, gkv, wuq, wukv, wvt, wuvt)


def _first_blocks(nb, nq, fstat=None):
    tiles = jnp.arange(nq)
    past = tiles[None, :] < tiles[:, None]
    if fstat is None:
        need = jnp.broadcast_to(past, (nb, 2, nq, nq))
    else:
        qn, kn = fstat[:, :, 0, :N_HEADS] * NORM_SLACK, fstat[:, :, 1, :N_HEADS] * NORM_SLACK
        cmax, cmin = fstat[:, :, 2, :N_HEADS], fstat[:, :, 3, :N_HEADS]
        gap = (qn[:, :, None] * kn[:, None, :] + cmax[:, :, None] - cmin[:, None, :]
               + (qn * kn)[:, :, None])
        need = (gap > -SKIP_GAP) & past[None, :, :, None]
        need = need.reshape(nb, nq, nq, 2, 2).any(-1).transpose(0, 3, 1, 2)
    first = jnp.where(need.any(-1), jnp.argmax(need, axis=-1), nq)
    first = jnp.minimum(first, jnp.maximum(tiles - 1, 0))
    count = jnp.sum(tiles - first, axis=-1, keepdims=True)
    return jnp.concatenate([first, count], axis=-1).astype(jnp.int32).reshape(-1)


def _attn_call(first_blocks, q, k, vt, gate, t, name):
    nb, seq, _ = q.shape
    whole = lambda width: pl.BlockSpec((1, seq, width), lambda b, p, tbl: (b, 0, p))
    return pl.pallas_call(
        functools.partial(_attn_kernel, t=t, nq=seq // t),
        grid_spec=pltpu.PrefetchScalarGridSpec(
            num_scalar_prefetch=1, grid=(nb, 2),
            in_specs=[whole(2 * LANES), whole(2 * LANES),
                      pl.BlockSpec((1, seq // t, LANES, t), lambda b, p, tbl: (b, 0, p, 0)), whole(LANES)],
            out_specs=whole(LANES),
            scratch_shapes=[pltpu.VMEM((2, t, t), F32), pltpu.VMEM((2, t, t), F32),
                            pltpu.VMEM((2, 1, t), F32), pltpu.VMEM((2, 1, t), F32),
                            pltpu.VMEM((seq // t, 2, 1, t), F32),
                            pltpu.VMEM((seq // t, 2, HEAD_DIM + DEN_ROWS, t), F32),
                            pltpu.VMEM((seq // t, 2, LANES, t), BF16),
                            pltpu.SMEM((2,), jnp.int32)]),
        out_shape=jax.ShapeDtypeStruct((nb, seq, WIDTH), BF16),
        compiler_params=pltpu.CompilerParams(dimension_semantics=("arbitrary", "arbitrary"),
                                             vmem_limit_bytes=VMEM_LIMIT),
        name=name,
    )(first_blocks, q, k, vt, gate)


def _s5_call(u, bm, lam2, cm, d, gw, gb, steps, nbatch):
    seq = u.shape[0]
    blk = steps * nbatch
    nchunks = seq // steps
    prev = lambda c: (jnp.maximum(c - 1, 0), 0)
    return pl.pallas_call(
        functools.partial(_s5_kernel, steps=steps, nbatch=nbatch, nchunks=nchunks),
        grid=(nchunks + 1,),
        in_specs=[pl.BlockSpec((steps, nbatch * WIDTH), lambda c: (jnp.minimum(c, nchunks - 1), 0)),
                  _full((WIDTH, 2 * S5_NSTATE)), _full((2, S5_NSTATE)), _full((2 * S5_NSTATE, WIDTH)),
                  _full((1, WIDTH)), _full((WIDTH, WIDTH)), _full((1, WIDTH))],
        out_specs=pl.BlockSpec((steps, nbatch * WIDTH), prev),
        out_shape=jax.ShapeDtypeStruct((seq, nbatch * WIDTH), F32),
        scratch_shapes=[pltpu.VMEM((blk, 2 * S5_NSTATE), F32), pltpu.VMEM((blk, 2 * S5_NSTATE), F32),
                        *[pltpu.VMEM((WIDTH // LANES, blk, LANES), F32)] * 3,
                        pltpu.VMEM((nbatch, 2 * S5_NSTATE), F32)],
        compiler_params=pltpu.CompilerParams(dimension_semantics=("arbitrary",),
                                             vmem_limit_bytes=VMEM_LIMIT),
        name="s5",
    )(u, bm, lam2, cm, d, gw, gb)


def _out_call(x, a, b2d, sg, c, d, w, fg, tm, final):
    nb, seq, _ = x.shape
    tok = lambda width: pl.BlockSpec((1, tm, width), lambda b, s: (b, s, 0))
    return pl.pallas_call(
        functools.partial(_out_kernel, final=final),
        grid=(nb, seq // tm),
        in_specs=[tok(D_MODEL), tok(WIDTH), pl.BlockSpec((tm, WIDTH), lambda b, s: (s, b)), tok(WIDTH),
                  tok(WIDTH), tok(WIDTH), _full((4 * WIDTH, D_MODEL)), _full((1, D_MODEL))],
        out_specs=tok(D_MODEL),
        out_shape=jax.ShapeDtypeStruct(x.shape, F32),
        compiler_params=pltpu.CompilerParams(dimension_semantics=("arbitrary", "arbitrary"),
                                             vmem_limit_bytes=VMEM_LIMIT),
        name="out_proj",
    )(x, a, b2d, sg, c, d, w, fg)


def kernel(x, norm_g, w_in, fox_fb, s5_a_re, s5_a_im, s5_log_dt, s5_b_re, s5_b_im, s5_c_re, s5_c_im, s5_d,
           s5_glu_w, s5_glu_b, mla_q_norm, mla_w_uq, mla_kv_norm, mla_w_ukv, w_out, final_g):
    nb, seq, _ = x.shape
    depth = norm_g.shape[0]
    assert nb == 8, "the S5 recurrence keeps the batch on the 8 sublanes of a vreg"
    tm, tm_out, steps = _tiles(seq)
    assert seq % tm == 0 and seq % steps == 0 and seq // MOBA_BLOCK <= 32

    tri = jnp.tril(jnp.ones((tm, tm), F32)).astype(BF16)
    route, bconst = _fox_routing()
    moba_t, mla_t = _rope_tables(seq)
    fgain = final_g.astype(F32).reshape(1, D_MODEL)
    every_block = _first_blocks(nb, seq // tm)

    for l in range(depth):
        w, wvt = _arrange_w_in(w_in[l])
        fb = jnp.zeros((1, LANES), F32).at[0, :N_HEADS].set(fox_fb[l].astype(F32))
        wuq, wukv, wuvt = _arrange_mla(mla_w_uq[l].astype(F32), mla_w_ukv[l].astype(F32))
        (foxq, foxk, foxv, fg, su, sg, mobq, mobk, mobv, mg, mlaq, mlak, mlav, lg, fstat) = _in_call(
            x, norm_g[l].astype(F32).reshape(1, D_MODEL), w, fb, tri, route, bconst, moba_t, mla_t,
            mla_q_norm[l].astype(F32).reshape(1, MLA_Q_RANK), mla_kv_norm[l].astype(F32).reshape(1, MLA_KV_RANK),
            wuq, wukv, wvt, wuvt, tm)

        a_out = _attn_call(_first_blocks(nb, seq // tm, fstat), foxq, foxk, foxv, fg, tm, "fox_attn")
        c_out = _attn_call(every_block, mobq, mobk, mobv, mg, tm, "moba_attn")
        d_out = _attn_call(every_block, mlaq, mlak, mlav, lg, tm, "mla_attn")

        bm, lam2, cm = _s5_matrices(s5_a_re[l], s5_a_im[l], s5_log_dt[l], s5_b_re[l], s5_b_im[l],
                                    s5_c_re[l], s5_c_im[l])
        b_out = _s5_call(su, bm, lam2, cm, s5_d[l].astype(F32).reshape(1, WIDTH), s5_glu_w[l].astype(BF16),
                         s5_glu_b[l].astype(F32).reshape(1, WIDTH), steps, nb)
        x = _out_call(x, a_out, b_out, sg, c_out, d_out, w_out[l].astype(BF16), fgain, tm_out,
                      final=(l == depth - 1))
    return x
```

```python
import functools

import jax
import jax.numpy as jnp
from jax import lax
from jax.experimental import pallas as pl
from jax.experimental.pallas import tpu as pltpu

F32 = jnp.float32
BF16 = jnp.bfloat16

D_MODEL = 1024
HEAD_DIM = 64
DEN_ROWS = 16
PAST_UNROLL = 16
SKIP_GAP = 152.0
NORM_SLACK = 1.01
N_HEADS = 4
WIDTH = 256
S5_GROUPS = 16
S5_GROUP = 16
S5_STATE = 64
S5_NSTATE = S5_GROUPS * S5_STATE
MOBA_BLOCK = 256
MOBA_TOPK = 3
MAX_BLOCKS = 32
GROUP_HEAD = (1, 3, 0, 2)
LOG2E = 1.4426950408889634
MLA_NOPE = 64
MLA_ROPE = 32
MLA_V = 64
MLA_Q_RANK = 384
MLA_KV_RANK = 128
ROPE_THETA = 10000.0
EPS = 1e-6
NEG = -1e30
LANES = 128
SUBLANES = 8
LOG2_HEAD_DIM = HEAD_DIM.bit_length() - 1
LOG2_MOBA_BLOCK = MOBA_BLOCK.bit_length() - 1
LOG2_MAX_BLOCKS = MAX_BLOCKS.bit_length() - 1
VMEM_LIMIT = 56 * 1024 * 1024

C_FFKR = 0
C_CKV, C_CQ = 128, 256
C_FQ, C_FK = 640, 896
C_MQ, C_MK = 1152, 1408
C_FG, C_SU, C_SG, C_MG, C_LG = 1664, 1920, 2176, 2432, 2688
N_COLS = 2944
PROJ_GROUPS = ((0, 1152), (1152, 512), (1664, 1280))
FOX_BIAS_LANES = 6


def _split3(x):
    x1 = x.astype(BF16)
    r1 = x - x1.astype(F32)
    x2 = r1.astype(BF16)
    r2 = r1 - x2.astype(F32)
    return x1, x2, r2.astype(BF16)


def _rope(x, cos, sin_lo, sin_hi, half):
    return (x * cos + pltpu.roll(x, LANES - half, 1) * sin_lo + pltpu.roll(x, half, 1) * sin_hi)


def _rms(x, g):
    return x * lax.rsqrt(jnp.mean(x * x, axis=-1, keepdims=True) + EPS) * g


def _silu(g):
    return g * (1.0 / (1.0 + jnp.exp(-g)))


def _in_kernel(x_ref, g_ref, w_ref, fb_ref, tri_ref, route_ref, bconst_ref,
               mc_ref, msl_ref, msh_ref, lc_ref, lsl_ref, lsh_ref,
               gq_ref, gkv_ref, wuq_ref, wukv_ref, wvt_ref, wuvt_ref,
               foxq_ref, foxk_ref, foxv_ref, fg_ref, su_ref, sg_ref,
               mobq_ref, mobk_ref, mobv_ref, mg_ref,
               mlaq_ref, mlak_ref, mlav_ref, lg_ref, fstat_ref,
               carry_ref, km_ref, *, tm):
    sblk = pl.program_id(1)
    nblk = tm // MOBA_BLOCK

    @pl.when(sblk == 0)
    def _():
        carry_ref[...] = jnp.zeros_like(carry_ref)
        km_ref[...] = jnp.zeros_like(km_ref)

    h = _rms(x_ref[0], g_ref[...]).astype(BF16)

    z = {}

    def proj(c0, width):
        for g0, gw in PROJ_GROUPS:
            if g0 <= c0 and c0 + width <= g0 + gw:
                if g0 not in z:
                    z[g0] = jnp.dot(h, w_ref[:, g0:g0 + gw], preferred_element_type=F32)
                return z[g0][:, c0 - g0:c0 - g0 + width]
        raise ValueError("column range crosses a projection group")

    lane = lax.broadcasted_iota(jnp.int32, (tm, LANES), 1)
    row = lax.broadcasted_iota(jnp.int32, (tm, LANES), 0)
    low_half = lane < HEAD_DIM

    def proj_t(wt):
        return lax.dot_general(wt, h, (((1,), (1,)), ((), ())), preferred_element_type=F32)

    for g0, gw in PROJ_GROUPS:
        proj(g0, gw)
    fg_ref[0] = proj(C_FG, WIDTH)
    sg_ref[0] = proj(C_SG, WIDTH)
    mg_ref[0] = proj(C_MG, WIDTH)
    lg_ref[0] = proj(C_LG, WIDTH)
    su_ref[...] = proj(C_SU, WIDTH)

    ffkr = proj(C_FFKR, LANES)
    ff = ffkr + fb_ref[...]
    logf = -(jnp.maximum(-ff, 0.0) + jnp.log1p(jnp.exp(-jnp.abs(ff))))
    within3 = jnp.dot(tri_ref[...], jnp.concatenate(_split3(logf), axis=1), preferred_element_type=F32)
    within = within3[:, 0:LANES] + within3[:, LANES:2 * LANES] + within3[:, 2 * LANES:]
    cum = within + carry_ref[...]
    carry_ref[...] = cum[tm - 1:tm, :]
    cum2 = cum * LOG2E
    routed = jnp.dot(jnp.concatenate(_split3(cum2), axis=1), route_ref[...],
                     preferred_element_type=F32) + bconst_ref[...]
    fq = proj(C_FQ, WIDTH) * (HEAD_DIM ** -0.5 * LOG2E)
    fk = proj(C_FK, WIDTH)
    lane_row = lane[0:1, :]
    qmax_row = jnp.zeros((1, LANES), F32)
    kmax_row = jnp.zeros((1, LANES), F32)
    for hd in range(N_HEADS):
        pair = hd // 2
        own = low_half if hd % 2 == 0 else jnp.logical_not(low_half)
        base = _fox_bias_base(hd)
        mine = (lane >= base) & (lane < base + FOX_BIAS_LANES)
        sl = slice(pair * LANES, (pair + 1) * LANES)
        qb = fq[:, sl].astype(BF16)
        kb = fk[:, sl].astype(BF16)
        qa = jnp.where(own, qb, jnp.where(mine, routed[:, 0:LANES], 0.0).astype(BF16))
        ka = jnp.where(own, kb, jnp.where(mine, routed[:, LANES:], 0.0).astype(BF16))
        foxq_ref[0, :, hd * LANES:(hd + 1) * LANES] = qa
        foxk_ref[0, :, hd * LANES:(hd + 1) * LANES] = ka
        for rounded, is_q in ((qb, True), (kb, False)):
            r = jnp.where(own, rounded.astype(F32), 0.0)
            norm = jnp.sqrt(jnp.max(jnp.sum(r * r, axis=1, keepdims=True), axis=0, keepdims=True))
            if is_q:
                qmax_row = jnp.where(lane_row == hd, norm, qmax_row)
            else:
                kmax_row = jnp.where(lane_row == hd, norm, kmax_row)
    srow = lax.broadcasted_iota(jnp.int32, (SUBLANES, LANES), 0)
    fstat_ref[0, 0] = jnp.where(srow == 0, qmax_row, jnp.where(srow == 1, kmax_row, jnp.where(
        srow == 2, jnp.max(cum2, axis=0, keepdims=True), jnp.where(
            srow == 3, jnp.min(cum2, axis=0, keepdims=True), 0.0))))

    mc, msl, msh = mc_ref[...], msl_ref[...], msh_ref[...]
    mq = proj(C_MQ, WIDTH)
    mk = proj(C_MK, WIDTH)
    q_r = [_rope(mq[:, p * LANES:(p + 1) * LANES], mc, msl, msh, HEAD_DIM // 2) for p in range(2)]
    k_r = [_rope(mk[:, p * LANES:(p + 1) * LANES], mc, msl, msh, HEAD_DIM // 2) for p in range(2)]
    km_row = lax.broadcasted_iota(jnp.int32, (LANES, WIDTH), 0)
    km_lane = lax.broadcasted_iota(jnp.int32, (LANES, WIDTH), 1)
    km_grp = lax.shift_right_logical(km_row, LOG2_MAX_BLOCKS)
    km_head = jnp.where(km_grp == 0, GROUP_HEAD[0], jnp.where(km_grp == 1, GROUP_HEAD[1],
                        jnp.where(km_grp == 2, GROUP_HEAD[2], GROUP_HEAD[3])))
    km_own = lax.shift_right_logical(km_lane, LOG2_HEAD_DIM) == km_head
    kmt = km_ref[...]
    for nb in range(nblk):
        blk = sblk * nblk + nb
        km = jnp.concatenate(
            [jnp.mean(k_r[p][nb * MOBA_BLOCK:(nb + 1) * MOBA_BLOCK, :], axis=0, keepdims=True) for p in range(2)],
            axis=1)
        kmt = jnp.where((km_row & (MAX_BLOCKS - 1)) == blk, jnp.where(km_own, km, 0.0), kmt)
    km_ref[...] = kmt
    gate_t = lax.dot_general(kmt, jnp.concatenate(q_r, axis=1), (((1,), (1,)), ((), ())),
                             precision=lax.Precision.HIGHEST, preferred_element_type=F32)
    cand = lax.broadcasted_iota(jnp.int32, (MAX_BLOCKS, tm), 0).astype(F32)
    tok = lax.broadcasted_iota(jnp.int32, (MAX_BLOCKS, tm), 1)
    blk_tok = (sblk * nblk + lax.shift_right_logical(tok, LOG2_MOBA_BLOCK)).astype(F32)
    bias_rows = []
    for grp in range(N_HEADS):
        g = jnp.where(cand < blk_tok, gate_t[grp * MAX_BLOCKS:(grp + 1) * MAX_BLOCKS, :], -jnp.inf)
        chosen = jnp.zeros((MAX_BLOCKS, tm), F32)
        for _ in range(MOBA_TOPK):
            m = jnp.max(g, axis=0, keepdims=True)
            first = jnp.min(jnp.where(g == m, cand, 1e9), axis=0, keepdims=True)
            first = jnp.where(m > -jnp.inf, first, -1.0)
            pick = cand == first
            chosen = jnp.where(pick, 1.0, chosen)
            g = jnp.where(pick, -jnp.inf, g)
        keep = jnp.where(cand == blk_tok, 1.0, chosen)
        bias_rows.append(jnp.where(keep > 0.0, 0.0, NEG))
    sel_bias = jnp.concatenate(bias_rows, axis=0).T
    blk_row = sblk * nblk + lax.shift_right_logical(row, LOG2_MOBA_BLOCK)
    onehot = jnp.where((lane & (MAX_BLOCKS - 1)) == blk_row, 1.0, 0.0)
    lane_grp = lax.shift_right_logical(lane, LOG2_MAX_BLOCKS)
    for hd in range(N_HEADS):
        pair = hd // 2
        own = low_half if hd % 2 == 0 else jnp.logical_not(low_half)
        mine = lane_grp == GROUP_HEAD.index(hd)
        qa = jnp.where(own, q_r[pair] * (HEAD_DIM ** -0.5 * LOG2E), jnp.where(mine, sel_bias, 0.0))
        ka = jnp.where(own, k_r[pair], jnp.where(mine, onehot, 0.0))
        mobq_ref[0, :, hd * LANES:(hd + 1) * LANES] = qa.astype(BF16)
        mobk_ref[0, :, hd * LANES:(hd + 1) * LANES] = ka.astype(BF16)

    lc, lsl, lsh = lc_ref[...], lsl_ref[...], lsh_ref[...]
    cqn = _rms(proj(C_CQ, MLA_Q_RANK), gq_ref[...]).astype(BF16)
    qf = jnp.dot(cqn, wuq_ref[...], preferred_element_type=F32)
    ckvn = _rms(proj(C_CKV, MLA_KV_RANK), gkv_ref[...]).astype(BF16)
    kv = jnp.dot(ckvn, wukv_ref[...], preferred_element_type=F32)
    kr = jnp.where(low_half, 0.0, _rope(ffkr, lc, lsl, lsh, MLA_ROPE // 2))
    scale = (MLA_NOPE + MLA_ROPE) ** -0.5 * LOG2E
    for hd in range(N_HEADS):
        sl = slice(hd * LANES, (hd + 1) * LANES)
        mlaq_ref[0, :, sl] = (_rope(qf[:, sl], lc, lsl, lsh, MLA_ROPE // 2) * scale).astype(BF16)
        mlak_ref[0, :, sl] = (kv[:, sl] + kr).astype(BF16)
    mlav_ref[0, 0] = lax.dot_general(wuvt_ref[...], ckvn, (((1,), (1,)), ((), ())),
                                     preferred_element_type=F32).astype(BF16)
    foxv_ref[0, 0] = proj_t(wvt_ref[0:WIDTH, :]).astype(BF16)
    mobv_ref[0, 0] = proj_t(wvt_ref[WIDTH:2 * WIDTH, :]).astype(BF16)


def _attn_kernel(tbl_ref, q_ref, k_ref, vt_ref, g_ref, o_ref, sa_ref, sb_ref, mxa_ref, mxb_ref, m_ref, acc_ref,
                 qt_ref, cur_ref, *, t, nq):
    bufs = ((sa_ref, mxa_ref), (sb_ref, mxb_ref))
    half = t // 2
    causal = (lax.broadcasted_iota(jnp.int32, (half, half), 0)
              <= lax.broadcasted_iota(jnp.int32, (half, half), 1))
    ones = jnp.ones((DEN_ROWS, t), BF16)
    pair = lambda i, j: (jnp.int32(i), jnp.int32(j))

    def transpose_tile(i):
        i = jnp.minimum(i, nq - 1)
        rows = pl.ds(pl.multiple_of(i * t, t), t)
        for hh in range(2):
            qt_ref[i, hh] = q_ref[0, rows, hh * LANES:(hh + 1) * LANES].astype(F32).T.astype(BF16)

    def scores(hh, ij, buf):
        s_ref, mx_ref = buf
        koff = pl.multiple_of(ij[1] * t, t)
        st = jnp.dot(k_ref[0, pl.ds(koff, t), hh * LANES:(hh + 1) * LANES], qt_ref[ij[0], hh],
                     preferred_element_type=F32)
        s_ref[hh] = st
        mx_ref[hh] = jnp.max(st, axis=0, keepdims=True)

    def values(hh, j):
        return jnp.concatenate([vt_ref[0, j, hh * HEAD_DIM:(hh + 1) * HEAD_DIM, :], ones], axis=0)

    def diag_scores(hh, i, buf):
        koff = pl.multiple_of(i * t, t)
        qt = qt_ref[i, hh]
        buf[0][hh, 0:half, :] = jnp.dot(k_ref[0, pl.ds(koff, half), hh * LANES:(hh + 1) * LANES], qt,
                                        preferred_element_type=F32)
        buf[0][hh, half:, half:] = jnp.dot(k_ref[0, pl.ds(koff + half, half), hh * LANES:(hh + 1) * LANES],
                                           qt[:, half:], preferred_element_type=F32)

    def first_update(hh, i, buf):
        lo = buf[0][hh, 0:half, :]
        lo_lo = jnp.where(causal, lo[:, 0:half], NEG)
        hi_hi = jnp.where(causal, buf[0][hh, half:, half:], NEG)
        mx_lo = jnp.max(lo_lo, axis=0, keepdims=True)
        mx_hi = jnp.maximum(jnp.max(lo[:, half:], axis=0, keepdims=True), jnp.max(hi_hi, axis=0, keepdims=True))
        p_lo = jnp.concatenate([jnp.exp2(lo_lo - mx_lo), jnp.exp2(lo[:, half:] - mx_hi)], axis=1).astype(BF16)
        p_hi = jnp.exp2(hi_hi - mx_hi).astype(BF16)
        va = values(hh, i)
        acc_lo = jnp.dot(va[:, 0:half], p_lo, preferred_element_type=F32)
        acc_hi = jnp.dot(va[:, half:], p_hi, preferred_element_type=F32)
        acc_ref[i, hh, :, 0:half] = acc_lo[:, 0:half]
        acc_ref[i, hh, :, half:] = acc_lo[:, half:] + acc_hi
        m_ref[i, hh] = jnp.concatenate([mx_lo, mx_hi], axis=1)

    def update(hh, ij, buf):
        i, j = ij
        m = m_ref[i, hh]
        m_new = jnp.maximum(m, buf[1][hh])
        alpha = jnp.exp2(m - m_new)
        p = jnp.exp2(buf[0][hh] - m_new).astype(BF16)
        acc_ref[i, hh] = alpha * acc_ref[i, hh] + jnp.dot(values(hh, j), p, preferred_element_type=F32)
        m_ref[i, hh] = m_new

    def stage(nxt, nxt_buf, cur, cur_buf, consume):
        for hh in range(2):
            if nxt is not None:
                scores(hh, nxt, nxt_buf)
            consume(hh, cur, cur_buf)

    tbl = (pl.program_id(0) * 2 + pl.program_id(1)) * (nq + 1)
    first_block = lambda i: tbl_ref[tbl + i]
    n_past = tbl_ref[tbl + nq]
    first_past = pair(min(1, nq - 1), 0)

    def diag_stage(nxt_diag, nxt_past, nxt_buf, cur, cur_buf):
        for hh in range(2):
            if nxt_diag is not None:
                diag_scores(hh, nxt_diag, nxt_buf)
            if nxt_past is not None:
                scores(hh, nxt_past, nxt_buf)
            first_update(hh, cur, cur_buf)

    for i in range(min(3, nq)):
        transpose_tile(jnp.int32(i))
    for hh in range(2):
        diag_scores(hh, jnp.int32(0), bufs[0])

    def two_diagonals(n, carry):
        i = 2 * n
        diag_stage(i + 1, None, bufs[1], i, bufs[0])
        diag_stage(i + 2, None, bufs[0], i + 1, bufs[1])
        transpose_tile(i + 3)
        transpose_tile(i + 4)
        return carry

    looped = 2 * ((nq - 1) // 2)
    lax.fori_loop(0, looped // 2, two_diagonals, 0)
    for i in range(looped, nq):
        last = i == nq - 1
        diag_stage(None if last else jnp.int32(i + 1), first_past if last and nq > 1 else None,
                   bufs[(i + 1) % 2], jnp.int32(i), bufs[i % 2])
    par = nq % 2

    def succ(ij):
        i, j = ij
        last = j == i - 1
        up = jnp.minimum(i + 1, nq - 1)
        return jnp.where(last, up, i), jnp.where(last, first_block(up), j + 1)

    def past_steps(count, cur):
        for k in range(count):
            nxt = succ(cur)
            stage(nxt, bufs[(par + k + 1) % 2], cur, bufs[(par + k) % 2], update)
            cur = nxt
        return cur

    cur = lax.fori_loop(0, n_past // PAST_UNROLL, lambda _, c: past_steps(PAST_UNROLL, c), first_past)
    cur_ref[0], cur_ref[1] = cur
    count = PAST_UNROLL // 2
    while count:
        @pl.when((n_past & count) != 0)
        def _(count=count):
            cur_ref[0], cur_ref[1] = past_steps(count, (cur_ref[0], cur_ref[1]))
        count //= 2

    def finish_tile(i, carry):
        out_t = jnp.concatenate([acc_ref[i, hh, 0:HEAD_DIM] / acc_ref[i, hh, HEAD_DIM:HEAD_DIM + 1]
                                 for hh in range(2)], axis=0)
        rows = pl.ds(pl.multiple_of(i * t, t), t)
        o_ref[0, rows, :] = (out_t.T * _silu(g_ref[0, rows, :])).astype(BF16)
        return carry

    lax.fori_loop(0, nq, finish_tile, 0)


def _s5_kernel(u_ref, bm_ref, lam_ref, cm_ref, d_ref, gw_ref, gb_ref, o_ref,
               xa_ref, xb_ref, ua_ref, ub_ref, y_ref, st_ref, *, steps, nbatch, nchunks):
    c = pl.program_id(0)

    @pl.when(c == 0)
    def _():
        st_ref[...] = jnp.zeros_like(st_ref)
        xb_ref[...] = jnp.zeros_like(xb_ref)
        ub_ref[...] = jnp.zeros_like(ub_ref)

    lam_re = jnp.broadcast_to(lam_ref[0:1, :], (nbatch, S5_NSTATE))
    lam_im = jnp.broadcast_to(lam_ref[1:2, :], (nbatch, S5_NSTATE))

    halves = WIDTH // LANES
    wide = lambda ref: jnp.concatenate([ref[hf] for hf in range(halves)], axis=1)

    def chunk(x_cur, x_prev, u_cur, u_prev):
        for b in range(nbatch):
            for hf in range(halves):
                lanes = slice(b * WIDTH + hf * LANES, b * WIDTH + (hf + 1) * LANES)
                u_cur[hf, pl.ds(b, steps, stride=nbatch), :] = u_ref[:, lanes]
        x_cur[...] = jnp.dot(wide(u_cur).astype(BF16), bm_ref[...], preferred_element_type=F32)
        y = jnp.dot(x_prev[...].astype(BF16), cm_ref[...], preferred_element_type=F32) + d_ref[...] * wide(u_prev)
        y = 0.5 * y * (1.0 + jnp.tanh(0.7978845608028654 * (y + 0.044715 * (y * y * y))))
        z = jnp.dot(y.astype(BF16), gw_ref[...], preferred_element_type=F32) + gb_ref[...]
        y = y * (1.0 / (1.0 + jnp.exp(-z)))
        for hf in range(halves):
            y_ref[hf] = y[:, hf * LANES:(hf + 1) * LANES]
        for b in range(nbatch):
            for hf in range(halves):
                lanes = slice(b * WIDTH + hf * LANES, b * WIDTH + (hf + 1) * LANES)
                o_ref[:, lanes] = y_ref[hf, pl.ds(b, steps, stride=nbatch), :]

        def body(t, carry):
            xr, xi = carry
            off = pl.multiple_of(t * nbatch, nbatch)
            nr = lam_re * xr - lam_im * xi + x_cur[pl.ds(off, nbatch), 0:S5_NSTATE]
            ni = lam_re * xi + lam_im * xr + x_cur[pl.ds(off, nbatch), S5_NSTATE:2 * S5_NSTATE]
            x_cur[pl.ds(off, nbatch), 0:S5_NSTATE] = nr
            x_cur[pl.ds(off, nbatch), S5_NSTATE:2 * S5_NSTATE] = ni
            return nr, ni

        @pl.when(c < nchunks)
        def _():
            xr, xi = lax.fori_loop(0, steps, body, (st_ref[:, 0:S5_NSTATE], st_ref[:, S5_NSTATE:2 * S5_NSTATE]),
                                   unroll=8)
            st_ref[:, 0:S5_NSTATE] = xr
            st_ref[:, S5_NSTATE:2 * S5_NSTATE] = xi

    @pl.when(c % 2 == 0)
    def _():
        chunk(xa_ref, xb_ref, ua_ref, ub_ref)

    @pl.when(c % 2 == 1)
    def _():
        chunk(xb_ref, xa_ref, ub_ref, ua_ref)


def _out_kernel(x_ref, a_ref, b_ref, sg_ref, c_ref, d_ref, w_ref, fg_ref, o_ref, *, final):
    b = (b_ref[...] * _silu(sg_ref[0])).astype(BF16)
    y = (jnp.dot(a_ref[0], w_ref[0:WIDTH, :], preferred_element_type=F32)
         + jnp.dot(b, w_ref[WIDTH:2 * WIDTH, :], preferred_element_type=F32)
         + jnp.dot(c_ref[0], w_ref[2 * WIDTH:3 * WIDTH, :], preferred_element_type=F32)
         + jnp.dot(d_ref[0], w_ref[3 * WIDTH:4 * WIDTH, :], preferred_element_type=F32))
    xn = x_ref[0] + y
    if final:
        xn = _rms(xn, fg_ref[...])
    o_ref[0] = xn


def _tiles(seq):
    tm = 512 if seq % 512 == 0 else MOBA_BLOCK
    tm_out = 1024 if seq % 1024 == 0 else tm
    steps = 128
    return tm, tm_out, steps


def _arrange_w_in(w):
    z = lambda n: jnp.zeros((D_MODEL, n), w.dtype)
    o = 0
    parts = {}
    for name, n in (("fq", 256), ("fk", 256), ("fv", 256), ("fg", 256), ("ff", 4), ("su", 256), ("sg", 256),
                    ("mq", 256), ("mk", 256), ("mv", 256), ("mg", 256), ("cq", 384), ("ckv", 128), ("kr", 32),
                    ("lg", 256)):
        parts[name] = w[:, o:o + n]
        o += n
    cols = [parts["ff"], z(MLA_NOPE - 4), parts["kr"], z(LANES - MLA_NOPE - MLA_ROPE),
            parts["ckv"], parts["cq"], parts["fq"], parts["fk"], parts["mq"], parts["mk"],
            parts["fg"], parts["su"], parts["sg"], parts["mg"], parts["lg"]]
    w_vt = jnp.concatenate([parts["fv"], parts["mv"]], axis=1).T
    return jnp.concatenate(cols, axis=1).astype(BF16), w_vt.astype(BF16)


def _fox_bias_base(hd):
    return (HEAD_DIM if hd % 2 == 0 else 0) + FOX_BIAS_LANES * (hd // 2)


def _fox_routing():
    import numpy as np
    route = np.zeros((3 * LANES, 2 * LANES), np.float32)
    const = np.zeros((1, 2 * LANES), np.float32)
    for hd in range(N_HEADS):
        base = _fox_bias_base(hd)
        for part in range(3):
            route[part * LANES + hd, base + part] = 1.0
            const[0, base + 3 + part] = 1.0
            const[0, LANES + base + part] = 1.0
            route[part * LANES + hd, LANES + base + 3 + part] = -1.0
    return jnp.asarray(route, BF16), jnp.asarray(const, F32)


def _rope_tables(seq):
    pos = jnp.arange(seq).astype(F32)[:, None]
    lane = jnp.arange(LANES)
    half = HEAD_DIM // 2
    inv = jnp.power(ROPE_THETA, -jnp.arange(half, dtype=F32) / half)
    ang = pos * inv[None, :]
    cos, sin = jnp.cos(ang)[:, lane % half], jnp.sin(ang)[:, lane % half]
    lo = (lane % HEAD_DIM) < half
    moba = (cos, jnp.where(lo, -sin, 0.0), jnp.where(lo, 0.0, sin))
    half = MLA_ROPE // 2
    inv = jnp.power(ROPE_THETA, -jnp.arange(half, dtype=F32) / half)
    ang = pos * inv[None, :]
    cos, sin = jnp.cos(ang)[:, lane % half], jnp.sin(ang)[:, lane % half]
    in_lo = (lane >= MLA_NOPE) & (lane < MLA_NOPE + half)
    in_hi = (lane >= MLA_NOPE + half) & (lane < MLA_NOPE + MLA_ROPE)
    mla = (jnp.where(in_lo | in_hi, cos, 1.0), jnp.where(in_lo, -sin, 0.0), jnp.where(in_hi, sin, 0.0))
    return moba, mla


def _arrange_mla(w_uq, w_ukv):
    z = lambda r, n: jnp.zeros((r, n), F32)
    dq = MLA_NOPE + MLA_ROPE
    q_cols, k_cols, v_cols = [], [], []
    for hd in range(N_HEADS):
        q_cols += [w_uq[:, hd * dq:(hd + 1) * dq], z(MLA_Q_RANK, LANES - dq)]
        base = hd * (MLA_NOPE + MLA_V)
        k_cols += [w_ukv[:, base:base + MLA_NOPE], z(MLA_KV_RANK, LANES - MLA_NOPE)]
        v_cols += [w_ukv[:, base + MLA_NOPE:base + MLA_NOPE + MLA_V]]
    return (jnp.concatenate(q_cols, axis=1).astype(BF16), jnp.concatenate(k_cols, axis=1).astype(BF16),
            jnp.concatenate(v_cols, axis=1).T.astype(BF16))


def _s5_matrices(a_re, a_im, log_dt, b_re, b_im, c_re, c_im):
    lam = lax.complex(a_re.astype(F32), a_im.astype(F32))
    dt = jnp.exp(log_dt.astype(F32))[:, None]
    lam_bar = jnp.exp(lam * dt)
    b_bar = ((lam_bar - 1.0) / lam)[..., None] * lax.complex(b_re.astype(F32), b_im.astype(F32))
    eye = jnp.eye(S5_GROUPS, dtype=F32)
    blockdiag_in = lambda t: jnp.einsum('gpc,gh->gchp', t, eye).reshape(WIDTH, S5_NSTATE)
    blockdiag_out = lambda t: jnp.einsum('gcp,gh->gphc', t, eye).reshape(S5_NSTATE, WIDTH)
    bm = jnp.concatenate([blockdiag_in(b_bar.real), blockdiag_in(b_bar.imag)], axis=1).astype(BF16)
    cm = jnp.concatenate([blockdiag_out(c_re.astype(F32)), -blockdiag_out(c_im.astype(F32))], axis=0).astype(BF16)
    lam2 = jnp.stack([lam_bar.real.reshape(S5_NSTATE), lam_bar.imag.reshape(S5_NSTATE)], axis=0)
    return bm, lam2, cm


def _full(shape):
    return pl.BlockSpec(shape, lambda *_: (0,) * len(shape))


def _in_call(x, g, w, fb, tri, route, bconst, moba_t, mla_t, gq, gkv, wuq, wukv, wvt, wuvt, tm):
    nb, seq, _ = x.shape
    tok = lambda width: pl.BlockSpec((1, tm, width), lambda b, s: (b, s, 0))
    tab = pl.BlockSpec((tm, LANES), lambda b, s: (s, 0))
    bf = lambda width: jax.ShapeDtypeStruct((nb, seq, width), BF16)
    f32 = lambda width: jax.ShapeDtypeStruct((nb, seq, width), F32)
    val_t = jax.ShapeDtypeStruct((nb, seq // tm, WIDTH, tm), BF16)
    val_t_spec = pl.BlockSpec((1, 1, WIDTH, tm), lambda b, s: (b, s, 0, 0))
    out_shape = (bf(512), bf(512), val_t, f32(256),
                 jax.ShapeDtypeStruct((seq, nb * WIDTH), F32), f32(256),
                 bf(512), bf(512), val_t, f32(256),
                 bf(512), bf(512), val_t, f32(256),
                 jax.ShapeDtypeStruct((nb, seq // tm, SUBLANES, LANES), F32))
    out_specs = (tok(512), tok(512), val_t_spec, tok(256),
                 pl.BlockSpec((tm, WIDTH), lambda b, s: (s, b)), tok(256),
                 tok(512), tok(512), val_t_spec, tok(256),
                 tok(512), tok(512), val_t_spec, tok(256),
                 pl.BlockSpec((1, 1, SUBLANES, LANES), lambda b, s: (b, s, 0, 0)))
    in_specs = [tok(D_MODEL), _full((1, D_MODEL)), _full((D_MODEL, N_COLS)), _full((1, LANES)),
                _full((tm, tm)), _full((3 * LANES, 2 * LANES)), _full((1, 2 * LANES)),
                tab, tab, tab, tab, tab, tab,
                _full((1, MLA_Q_RANK)), _full((1, MLA_KV_RANK)),
                _full((MLA_Q_RANK, N_HEADS * LANES)), _full((MLA_KV_RANK, N_HEADS * LANES)),
                _full((2 * WIDTH, D_MODEL)), _full((WIDTH, MLA_KV_RANK))]
    return pl.pallas_call(
        functools.partial(_in_kernel, tm=tm),
        grid=(nb, seq // tm),
        in_specs=in_specs, out_specs=out_specs, out_shape=out_shape,
        scratch_shapes=[pltpu.VMEM((1, LANES), F32), pltpu.VMEM((LANES, WIDTH), F32)],
        compiler_params=pltpu.CompilerParams(dimension_semantics=("arbitrary", "arbitrary"),
                                             vmem_limit_bytes=VMEM_LIMIT),
        name="in_proj",
    )(x, g, w, fb, tri, route, bconst, *moba_t, *mla_t, gq, gkv, wuq, wukv, wvt, wuvt)


def _first_blocks(nb, nq, fstat=None):
    tiles = jnp.arange(nq)
    past = tiles[None, :] < tiles[:, None]
    if fstat is None:
        need = jnp.broadcast_to(past, (nb, 2, nq, nq))
    else:
        qn, kn = fstat[:, :, 0, :N_HEADS] * NORM_SLACK, fstat[:, :, 1, :N_HEADS] * NORM_SLACK
        cmax, cmin = fstat[:, :, 2, :N_HEADS], fstat[:, :, 3, :N_HEADS]
        gap = (qn[:, :, None] * kn[:, None, :] + cmax[:, :, None] - cmin[:, None, :]
               + (qn * kn)[:, :, None])
        need = (gap > -SKIP_GAP) & past[None, :, :, None]
        need = need.reshape(nb, nq, nq, 2, 2).any(-1).transpose(0, 3, 1, 2)
    first = jnp.where(need.any(-1), jnp.argmax(need, axis=-1), nq)
    first = jnp.minimum(first, jnp.maximum(tiles - 1, 0))
    count = jnp.sum(tiles - first, axis=-1, keepdims=True)
    return jnp.concatenate([first, count], axis=-1).astype(jnp.int32).reshape(-1)


def _attn_call(first_blocks, q, k, vt, gate, t, name):
    nb, seq, _ = q.shape
    whole = lambda width: pl.BlockSpec((1, seq, width), lambda b, p, tbl: (b, 0, p))
    return pl.pallas_call(
        functools.partial(_attn_kernel, t=t, nq=seq // t),
        grid_spec=pltpu.PrefetchScalarGridSpec(
            num_scalar_prefetch=1, grid=(nb, 2),
            in_specs=[whole(2 * LANES), whole(2 * LANES),
                      pl.BlockSpec((1, seq // t, LANES, t), lambda b, p, tbl: (b, 0, p, 0)), whole(LANES)],
            out_specs=whole(LANES),
            scratch_shapes=[pltpu.VMEM((2, t, t), F32), pltpu.VMEM((2, t, t), F32),
                            pltpu.VMEM((2, 1, t), F32), pltpu.VMEM((2, 1, t), F32),
                            pltpu.VMEM((seq // t, 2, 1, t), F32),
                            pltpu.VMEM((seq // t, 2, HEAD_DIM + DEN_ROWS, t), F32),
                            pltpu.VMEM((seq // t, 2, LANES, t), BF16),
                            pltpu.SMEM((2,), jnp.int32)]),
        out_shape=jax.ShapeDtypeStruct((nb, seq, WIDTH), BF16),
        compiler_params=pltpu.CompilerParams(dimension_semantics=("arbitrary", "arbitrary"),
                                             vmem_limit_bytes=VMEM_LIMIT),
        name=name,
    )(first_blocks, q, k, vt, gate)


def _s5_call(u, bm, lam2, cm, d, gw, gb, steps, nbatch):
    seq = u.shape[0]
    blk = steps * nbatch
    nchunks = seq // steps
    prev = lambda c: (jnp.maximum(c - 1, 0), 0)
    return pl.pallas_call(
        functools.partial(_s5_kernel, steps=steps, nbatch=nbatch, nchunks=nchunks),
        grid=(nchunks + 1,),
        in_specs=[pl.BlockSpec((steps, nbatch * WIDTH), lambda c: (jnp.minimum(c, nchunks - 1), 0)),
                  _full((WIDTH, 2 * S5_NSTATE)), _full((2, S5_NSTATE)), _full((2 * S5_NSTATE, WIDTH)),
                  _full((1, WIDTH)), _full((WIDTH, WIDTH)), _full((1, WIDTH))],
        out_specs=pl.BlockSpec((steps, nbatch * WIDTH), prev),
        out_shape=jax.ShapeDtypeStruct((seq, nbatch * WIDTH), F32),
        scratch_shapes=[pltpu.VMEM((blk, 2 * S5_NSTATE), F32), pltpu.VMEM((blk, 2 * S5_NSTATE), F32),
                        *[pltpu.VMEM((WIDTH // LANES, blk, LANES), F32)] * 3,
                        pltpu.VMEM((nbatch, 2 * S5_NSTATE), F32)],
        compiler_params=pltpu.CompilerParams(dimension_semantics=("arbitrary",),
                                             vmem_limit_bytes=VMEM_LIMIT),
        name="s5",
    )(u, bm, lam2, cm, d, gw, gb)


def _out_call(x, a, b2d, sg, c, d, w, fg, tm, final):
    nb, seq, _ = x.shape
    tok = lambda width: pl.BlockSpec((1, tm, width), lambda b, s: (b, s, 0))
    return pl.pallas_call(
        functools.partial(_out_kernel, final=final),
        grid=(nb, seq // tm),
        in_specs=[tok(D_MODEL), tok(WIDTH), pl.BlockSpec((tm, WIDTH), lambda b, s: (s, b)), tok(WIDTH),
                  tok(WIDTH), tok(WIDTH), _full((4 * WIDTH, D_MODEL)), _full((1, D_MODEL))],
        out_specs=tok(D_MODEL),
        out_shape=jax.ShapeDtypeStruct(x.shape, F32),
        compiler_params=pltpu.CompilerParams(dimension_semantics=("arbitrary", "arbitrary"),
                                             vmem_limit_bytes=VMEM_LIMIT),
        name="out_proj",
    )(x, a, b2d, sg, c, d, w, fg)


def kernel(x, norm_g, w_in, fox_fb, s5_a_re, s5_a_im, s5_log_dt, s5_b_re, s5_b_im, s5_c_re, s5_c_im, s5_d,
           s5_glu_w, s5_glu_b, mla_q_norm, mla_w_uq, mla_kv_norm, mla_w_ukv, w_out, final_g):
    nb, seq, _ = x.shape
    depth = norm_g.shape[0]
    assert nb == 8, "the S5 recurrence keeps the batch on the 8 sublanes of a vreg"
    tm, tm_out, steps = _tiles(seq)
    assert seq % tm == 0 and seq % steps == 0 and seq // MOBA_BLOCK <= 32

    tri = jnp.tril(jnp.ones((tm, tm), F32)).astype(BF16)
    route, bconst = _fox_routing()
    moba_t, mla_t = _rope_tables(seq)
    fgain = final_g.astype(F32).reshape(1, D_MODEL)
    every_block = _first_blocks(nb, seq // tm)

    for l in range(depth):
        w, wvt = _arrange_w_in(w_in[l])
        fb = jnp.zeros((1, LANES), F32).at[0, :N_HEADS].set(fox_fb[l].astype(F32))
        wuq, wukv, wuvt = _arrange_mla(mla_w_uq[l].astype(F32), mla_w_ukv[l].astype(F32))
        (foxq, foxk, foxv, fg, su, sg, mobq, mobk, mobv, mg, mlaq, mlak, mlav, lg, fstat) = _in_call(
            x, norm_g[l].astype(F32).reshape(1, D_MODEL), w, fb, tri, route, bconst, moba_t, mla_t,
            mla_q_norm[l].astype(F32).reshape(1, MLA_Q_RANK), mla_kv_norm[l].astype(F32).reshape(1, MLA_KV_RANK),
            wuq, wukv, wvt, wuvt, tm)

        a_out = _attn_call(_first_blocks(nb, seq // tm, fstat), foxq, foxk, foxv, fg, tm, "fox_attn")
        c_out = _attn_call(every_block, mobq, mobk, mobv, mg, tm, "moba_attn")
        d_out = _attn_call(every_block, mlaq, mlak, mlav, lg, tm, "mla_attn")

        bm, lam2, cm = _s5_matrices(s5_a_re[l], s5_a_im[l], s5_log_dt[l], s5_b_re[l], s5_b_im[l],
                                    s5_c_re[l], s5_c_im[l])
        b_out = _s5_call(su, bm, lam2, cm, s5_d[l].astype(F32).reshape(1, WIDTH), s5_glu_w[l].astype(BF16),
                         s5_glu_b[l].astype(F32).reshape(1, WIDTH), steps, nb)
        x = _out_call(x, a_out, b_out, sg, c_out, d_out, w_out[l].astype(BF16), fgain, tm_out,
                      final=(l == depth - 1))
    return x
```

```python
import functools

import jax
import jax.numpy as jnp
from jax import lax
from jax.experimental import pallas as pl
from jax.experimental.pallas import tpu as pltpu

F32 = jnp.float32
BF16 = jnp.bfloat16

D_MODEL = 1024
HEAD_DIM = 64
DEN_ROWS = 16
PAST_UNROLL = 16
DIAG_UNROLL = 4
SKIP_GAP = 152.0
NORM_SLACK = 1.01
N_HEADS = 4
WIDTH = 256
S5_GROUPS = 16
S5_GROUP = 16
S5_STATE = 64
S5_NSTATE = S5_GROUPS * S5_STATE
MOBA_BLOCK = 256
MOBA_TOPK = 3
MAX_BLOCKS = 32
GROUP_HEAD = (1, 3, 0, 2)
LOG2E = 1.4426950408889634
MLA_NOPE = 64
MLA_ROPE = 32
MLA_V = 64
MLA_Q_RANK = 384
MLA_KV_RANK = 128
ROPE_THETA = 10000.0
EPS = 1e-6
NEG = -1e30
LANES = 128
SUBLANES = 8
LOG2_HEAD_DIM = HEAD_DIM.bit_length() - 1
LOG2_MOBA_BLOCK = MOBA_BLOCK.bit_length() - 1
LOG2_MAX_BLOCKS = MAX_BLOCKS.bit_length() - 1
VMEM_LIMIT = 56 * 1024 * 1024

C_FFKR = 0
C_CKV, C_CQ = 128, 256
C_FQ, C_FK = 640, 896
C_MQ, C_MK = 1152, 1408
C_FG, C_SU, C_SG, C_MG, C_LG = 1664, 1920, 2176, 2432, 2688
N_COLS = 2944
PROJ_GROUPS = ((0, 1152), (1152, 512), (1664, 1280))
FOX_BIAS_LANES = 6


def _split3(x):
    x1 = x.astype(BF16)
    r1 = x - x1.astype(F32)
    x2 = r1.astype(BF16)
    r2 = r1 - x2.astype(F32)
    return x1, x2, r2.astype(BF16)


def _rope(x, cos, sin_lo, sin_hi, half):
    return (x * cos + pltpu.roll(x, LANES - half, 1) * sin_lo + pltpu.roll(x, half, 1) * sin_hi)


def _rms(x, g):
    return x * lax.rsqrt(jnp.mean(x * x, axis=-1, keepdims=True) + EPS) * g


def _silu(g):
    return g * (1.0 / (1.0 + jnp.exp(-g)))


def _in_kernel(x_ref, g_ref, w_ref, fb_ref, tri_ref, route_ref, bconst_ref,
               mc_ref, msl_ref, msh_ref, lc_ref, lsl_ref, lsh_ref,
               gq_ref, gkv_ref, wuq_ref, wukv_ref, wvt_ref, wuvt_ref,
               foxq_ref, foxk_ref, foxv_ref, fg_ref, su_ref, sg_ref,
               mobq_ref, mobk_ref, mobv_ref, mg_ref,
               mlaq_ref, mlak_ref, mlav_ref, lg_ref, fstat_ref,
               carry_ref, km_ref, *, tm):
    sblk = pl.program_id(1)
    nblk = tm // MOBA_BLOCK

    @pl.when(sblk == 0)
    def _():
        carry_ref[...] = jnp.zeros_like(carry_ref)
        km_ref[...] = jnp.zeros_like(km_ref)

    h = _rms(x_ref[0], g_ref[...]).astype(BF16)

    z = {}

    def proj(c0, width):
        for g0, gw in PROJ_GROUPS:
            if g0 <= c0 and c0 + width <= g0 + gw:
                if g0 not in z:
                    z[g0] = jnp.dot(h, w_ref[:, g0:g0 + gw], preferred_element_type=F32)
                return z[g0][:, c0 - g0:c0 - g0 + width]
        raise ValueError("column range crosses a projection group")

    lane = lax.broadcasted_iota(jnp.int32, (tm, LANES), 1)
    row = lax.broadcasted_iota(jnp.int32, (tm, LANES), 0)
    low_half = lane < HEAD_DIM

    def proj_t(wt):
        return lax.dot_general(wt, h, (((1,), (1,)), ((), ())), preferred_element_type=F32)

    for g0, gw in PROJ_GROUPS:
        proj(g0, gw)
    fg_ref[0] = proj(C_FG, WIDTH)
    sg_ref[0] = proj(C_SG, WIDTH)
    mg_ref[0] = proj(C_MG, WIDTH)
    lg_ref[0] = proj(C_LG, WIDTH)
    su_ref[...] = proj(C_SU, WIDTH)

    ffkr = proj(C_FFKR, LANES)
    ff = ffkr + fb_ref[...]
    logf = -(jnp.maximum(-ff, 0.0) + jnp.log1p(jnp.exp(-jnp.abs(ff))))
    within3 = jnp.dot(tri_ref[...], jnp.concatenate(_split3(logf), axis=1), preferred_element_type=F32)
    within = within3[:, 0:LANES] + within3[:, LANES:2 * LANES] + within3[:, 2 * LANES:]
    cum = within + carry_ref[...]
    carry_ref[...] = cum[tm - 1:tm, :]
    cum2 = cum * LOG2E
    routed = jnp.dot(jnp.concatenate(_split3(cum2), axis=1), route_ref[...],
                     preferred_element_type=F32) + bconst_ref[...]
    fq = proj(C_FQ, WIDTH) * (HEAD_DIM ** -0.5 * LOG2E)
    fk = proj(C_FK, WIDTH)
    lane_row = lane[0:1, :]
    qmax_row = jnp.zeros((1, LANES), F32)
    kmax_row = jnp.zeros((1, LANES), F32)
    for hd in range(N_HEADS):
        pair = hd // 2
        own = low_half if hd % 2 == 0 else jnp.logical_not(low_half)
        base = _fox_bias_base(hd)
        mine = (lane >= base) & (lane < base + FOX_BIAS_LANES)
        sl = slice(pair * LANES, (pair + 1) * LANES)
        qb = fq[:, sl].astype(BF16)
        kb = fk[:, sl].astype(BF16)
        qa = jnp.where(own, qb, jnp.where(mine, routed[:, 0:LANES], 0.0).astype(BF16))
        ka = jnp.where(own, kb, jnp.where(mine, routed[:, LANES:], 0.0).astype(BF16))
        foxq_ref[0, :, hd * LANES:(hd + 1) * LANES] = qa
        foxk_ref[0, :, hd * LANES:(hd + 1) * LANES] = ka
        for rounded, is_q in ((qb, True), (kb, False)):
            r = jnp.where(own, rounded.astype(F32), 0.0)
            norm = jnp.sqrt(jnp.max(jnp.sum(r * r, axis=1, keepdims=True), axis=0, keepdims=True))
            if is_q:
                qmax_row = jnp.where(lane_row == hd, norm, qmax_row)
            else:
                kmax_row = jnp.where(lane_row == hd, norm, kmax_row)
    srow = lax.broadcasted_iota(jnp.int32, (SUBLANES, LANES), 0)
    fstat_ref[0, 0] = jnp.where(srow == 0, qmax_row, jnp.where(srow == 1, kmax_row, jnp.where(
        srow == 2, jnp.max(cum2, axis=0, keepdims=True), jnp.where(
            srow == 3, jnp.min(cum2, axis=0, keepdims=True), 0.0))))

    mc, msl, msh = mc_ref[...], msl_ref[...], msh_ref[...]
    mq = proj(C_MQ, WIDTH)
    mk = proj(C_MK, WIDTH)
    q_r = [_rope(mq[:, p * LANES:(p + 1) * LANES], mc, msl, msh, HEAD_DIM // 2) for p in range(2)]
    k_r = [_rope(mk[:, p * LANES:(p + 1) * LANES], mc, msl, msh, HEAD_DIM // 2) for p in range(2)]
    km_row = lax.broadcasted_iota(jnp.int32, (LANES, WIDTH), 0)
    km_lane = lax.broadcasted_iota(jnp.int32, (LANES, WIDTH), 1)
    km_grp = lax.shift_right_logical(km_row, LOG2_MAX_BLOCKS)
    km_head = jnp.where(km_grp == 0, GROUP_HEAD[0], jnp.where(km_grp == 1, GROUP_HEAD[1],
                        jnp.where(km_grp == 2, GROUP_HEAD[2], GROUP_HEAD[3])))
    km_own = lax.shift_right_logical(km_lane, LOG2_HEAD_DIM) == km_head
    kmt = km_ref[...]
    for nb in range(nblk):
        blk = sblk * nblk + nb
        km = jnp.concatenate(
            [jnp.mean(k_r[p][nb * MOBA_BLOCK:(nb + 1) * MOBA_BLOCK, :], axis=0, keepdims=True) for p in range(2)],
            axis=1)
        kmt = jnp.where((km_row & (MAX_BLOCKS - 1)) == blk, jnp.where(km_own, km, 0.0), kmt)
    km_ref[...] = kmt
    gate_t = lax.dot_general(kmt, jnp.concatenate(q_r, axis=1), (((1,), (1,)), ((), ())),
                             precision=lax.Precision.HIGHEST, preferred_element_type=F32)
    cand = lax.broadcasted_iota(jnp.int32, (MAX_BLOCKS, tm), 0).astype(F32)
    tok = lax.broadcasted_iota(jnp.int32, (MAX_BLOCKS, tm), 1)
    blk_tok = (sblk * nblk + lax.shift_right_logical(tok, LOG2_MOBA_BLOCK)).astype(F32)
    bias_rows = []
    for grp in range(N_HEADS):
        g = jnp.where(cand < blk_tok, gate_t[grp * MAX_BLOCKS:(grp + 1) * MAX_BLOCKS, :], -jnp.inf)
        chosen = jnp.zeros((MAX_BLOCKS, tm), F32)
        for _ in range(MOBA_TOPK):
            m = jnp.max(g, axis=0, keepdims=True)
            first = jnp.min(jnp.where(g == m, cand, 1e9), axis=0, keepdims=True)
            first = jnp.where(m > -jnp.inf, first, -1.0)
            pick = cand == first
            chosen = jnp.where(pick, 1.0, chosen)
            g = jnp.where(pick, -jnp.inf, g)
        keep = jnp.where(cand == blk_tok, 1.0, chosen)
        bias_rows.append(jnp.where(keep > 0.0, 0.0, NEG))
    sel_bias = jnp.concatenate(bias_rows, axis=0).T
    blk_row = sblk * nblk + lax.shift_right_logical(row, LOG2_MOBA_BLOCK)
    onehot = jnp.where((lane & (MAX_BLOCKS - 1)) == blk_row, 1.0, 0.0)
    lane_grp = lax.shift_right_logical(lane, LOG2_MAX_BLOCKS)
    for hd in range(N_HEADS):
        pair = hd // 2
        own = low_half if hd % 2 == 0 else jnp.logical_not(low_half)
        mine = lane_grp == GROUP_HEAD.index(hd)
        qa = jnp.where(own, q_r[pair] * (HEAD_DIM ** -0.5 * LOG2E), jnp.where(mine, sel_bias, 0.0))
        ka = jnp.where(own, k_r[pair], jnp.where(mine, onehot, 0.0))
        mobq_ref[0, :, hd * LANES:(hd + 1) * LANES] = qa.astype(BF16)
        mobk_ref[0, :, hd * LANES:(hd + 1) * LANES] = ka.astype(BF16)

    lc, lsl, lsh = lc_ref[...], lsl_ref[...], lsh_ref[...]
    cqn = _rms(proj(C_CQ, MLA_Q_RANK), gq_ref[...]).astype(BF16)
    qf = jnp.dot(cqn, wuq_ref[...], preferred_element_type=F32)
    ckvn = _rms(proj(C_CKV, MLA_KV_RANK), gkv_ref[...]).astype(BF16)
    kv = jnp.dot(ckvn, wukv_ref[...], preferred_element_type=F32)
    kr = jnp.where(low_half, 0.0, _rope(ffkr, lc, lsl, lsh, MLA_ROPE // 2))
    scale = (MLA_NOPE + MLA_ROPE) ** -0.5 * LOG2E
    for hd in range(N_HEADS):
        sl = slice(hd * LANES, (hd + 1) * LANES)
        mlaq_ref[0, :, sl] = (_rope(qf[:, sl], lc, lsl, lsh, MLA_ROPE // 2) * scale).astype(BF16)
        mlak_ref[0, :, sl] = (kv[:, sl] + kr).astype(BF16)
    mlav_ref[0, 0] = lax.dot_general(wuvt_ref[...], ckvn, (((1,), (1,)), ((), ())),
                                     preferred_element_type=F32).astype(BF16)
    foxv_ref[0, 0] = proj_t(wvt_ref[0:WIDTH, :]).astype(BF16)
    mobv_ref[0, 0] = proj_t(wvt_ref[WIDTH:2 * WIDTH, :]).astype(BF16)


def _attn_kernel(tbl_ref, q_ref, k_ref, vt_ref, g_ref, o_ref, sa_ref, sb_ref, mxa_ref, mxb_ref, m_ref, acc_ref,
                 qt_ref, cur_ref, *, t, nq):
    bufs = ((sa_ref, mxa_ref), (sb_ref, mxb_ref))
    half = t // 2
    causal = (lax.broadcasted_iota(jnp.int32, (half, half), 0)
              <= lax.broadcasted_iota(jnp.int32, (half, half), 1))
    ones = jnp.ones((DEN_ROWS, t), BF16)
    pair = lambda i, j: (jnp.int32(i), jnp.int32(j))

    def transpose_tile(i):
        i = jnp.minimum(i, nq - 1)
        rows = pl.ds(pl.multiple_of(i * t, t), t)
        for hh in range(2):
            qt_ref[i, hh] = q_ref[0, rows, hh * LANES:(hh + 1) * LANES].astype(F32).T.astype(BF16)

    def scores(hh, ij, buf):
        s_ref, mx_ref = buf
        koff = pl.multiple_of(ij[1] * t, t)
        st = jnp.dot(k_ref[0, pl.ds(koff, t), hh * LANES:(hh + 1) * LANES], qt_ref[ij[0], hh],
                     preferred_element_type=F32)
        s_ref[hh] = st
        mx_ref[hh] = jnp.max(st, axis=0, keepdims=True)

    def values(hh, j):
        return jnp.concatenate([vt_ref[0, j, hh * HEAD_DIM:(hh + 1) * HEAD_DIM, :], ones], axis=0)

    def diag_scores(hh, i, buf):
        koff = pl.multiple_of(i * t, t)
        qt = qt_ref[i, hh]
        buf[0][hh, 0:half, :] = jnp.dot(k_ref[0, pl.ds(koff, half), hh * LANES:(hh + 1) * LANES], qt,
                                        preferred_element_type=F32)
        buf[0][hh, half:, half:] = jnp.dot(k_ref[0, pl.ds(koff + half, half), hh * LANES:(hh + 1) * LANES],
                                           qt[:, half:], preferred_element_type=F32)

    def first_update(hh, i, buf):
        lo = buf[0][hh, 0:half, :]
        lo_lo = jnp.where(causal, lo[:, 0:half], NEG)
        hi_hi = jnp.where(causal, buf[0][hh, half:, half:], NEG)
        mx_lo = jnp.max(lo_lo, axis=0, keepdims=True)
        mx_hi = jnp.maximum(jnp.max(lo[:, half:], axis=0, keepdims=True), jnp.max(hi_hi, axis=0, keepdims=True))
        p_lo = jnp.concatenate([jnp.exp2(lo_lo - mx_lo), jnp.exp2(lo[:, half:] - mx_hi)], axis=1).astype(BF16)
        p_hi = jnp.exp2(hi_hi - mx_hi).astype(BF16)
        va = values(hh, i)
        acc_lo = jnp.dot(va[:, 0:half], p_lo, preferred_element_type=F32)
        acc_hi = jnp.dot(va[:, half:], p_hi, preferred_element_type=F32)
        acc_ref[i, hh, :, 0:half] = acc_lo[:, 0:half]
        acc_ref[i, hh, :, half:] = acc_lo[:, half:] + acc_hi
        m_ref[i, hh] = jnp.concatenate([mx_lo, mx_hi], axis=1)

    def update(hh, ij, buf):
        i, j = ij
        m = m_ref[i, hh]
        m_new = jnp.maximum(m, buf[1][hh])
        alpha = jnp.exp2(m - m_new)
        p = jnp.exp2(buf[0][hh] - m_new).astype(BF16)
        acc_ref[i, hh] = alpha * acc_ref[i, hh] + jnp.dot(values(hh, j), p, preferred_element_type=F32)
        m_ref[i, hh] = m_new

    def stage(nxt, nxt_buf, cur, cur_buf, consume):
        for hh in range(2):
            if nxt is not None:
                scores(hh, nxt, nxt_buf)
            consume(hh, cur, cur_buf)

    tbl = (pl.program_id(0) * 2 + pl.program_id(1)) * (nq + 1)
    first_block = lambda i: tbl_ref[tbl + i]
    n_past = tbl_ref[tbl + nq]
    first_past = pair(min(1, nq - 1), 0)

    def diag_stage(nxt_diag, nxt_past, nxt_buf, cur, cur_buf):
        for hh in range(2):
            if nxt_diag is not None:
                diag_scores(hh, nxt_diag, nxt_buf)
            if nxt_past is not None:
                scores(hh, nxt_past, nxt_buf)
            first_update(hh, cur, cur_buf)

    for i in range(min(DIAG_UNROLL + 1, nq)):
        transpose_tile(jnp.int32(i))
    for hh in range(2):
        diag_scores(hh, jnp.int32(0), bufs[0])

    def diagonals(n, carry):
        i0 = DIAG_UNROLL * n
        for k in range(DIAG_UNROLL):
            diag_stage(i0 + k + 1, None, bufs[(k + 1) % 2], i0 + k, bufs[k % 2])
        for k in range(DIAG_UNROLL):
            transpose_tile(i0 + DIAG_UNROLL + 1 + k)
        return carry

    looped = DIAG_UNROLL * ((nq - 1) // DIAG_UNROLL)
    lax.fori_loop(0, looped // DIAG_UNROLL, diagonals, 0)
    for i in range(looped, nq):
        last = i == nq - 1
        diag_stage(None if last else jnp.int32(i + 1), first_past if last and nq > 1 else None,
                   bufs[(i + 1) % 2], jnp.int32(i), bufs[i % 2])
    par = nq % 2

    def succ(ij):
        i, j = ij
        last = j == i - 1
        up = jnp.minimum(i + 1, nq - 1)
        return jnp.where(last, up, i), jnp.where(last, first_block(up), j + 1)

    def past_steps(count, cur):
        for k in range(count):
            nxt = succ(cur)
            stage(nxt, bufs[(par + k + 1) % 2], cur, bufs[(par + k) % 2], update)
            cur = nxt
        return cur

    cur = lax.fori_loop(0, n_past // PAST_UNROLL, lambda _, c: past_steps(PAST_UNROLL, c), first_past)
    cur_ref[0], cur_ref[1] = cur
    count = PAST_UNROLL // 2
    while count:
        @pl.when((n_past & count) != 0)
        def _(count=count):
            cur_ref[0], cur_ref[1] = past_steps(count, (cur_ref[0], cur_ref[1]))
        count //= 2

    def finish_tile(i, carry):
        out_t = jnp.concatenate([acc_ref[i, hh, 0:HEAD_DIM] / acc_ref[i, hh, HEAD_DIM:HEAD_DIM + 1]
                                 for hh in range(2)], axis=0)
        rows = pl.ds(pl.multiple_of(i * t, t), t)
        o_ref[0, rows, :] = (out_t.T * _silu(g_ref[0, rows, :])).astype(BF16)
        return carry

    lax.fori_loop(0, nq, finish_tile, 0)


def _s5_kernel(u_ref, bm_ref, lam_ref, cm_ref, d_ref, gw_ref, gb_ref, o_ref,
               xa_ref, xb_ref, ua_ref, ub_ref, y_ref, st_ref, *, steps, nbatch, nchunks):
    c = pl.program_id(0)

    @pl.when(c == 0)
    def _():
        st_ref[...] = jnp.zeros_like(st_ref)
        xb_ref[...] = jnp.zeros_like(xb_ref)
        ub_ref[...] = jnp.zeros_like(ub_ref)

    lam_re = jnp.broadcast_to(lam_ref[0:1, :], (nbatch, S5_NSTATE))
    lam_im = jnp.broadcast_to(lam_ref[1:2, :], (nbatch, S5_NSTATE))

    halves = WIDTH // LANES
    wide = lambda ref: jnp.concatenate([ref[hf] for hf in range(halves)], axis=1)

    def chunk(x_cur, x_prev, u_cur, u_prev):
        for b in range(nbatch):
            for hf in range(halves):
                lanes = slice(b * WIDTH + hf * LANES, b * WIDTH + (hf + 1) * LANES)
                u_cur[hf, pl.ds(b, steps, stride=nbatch), :] = u_ref[:, lanes]
        x_cur[...] = jnp.dot(wide(u_cur).astype(BF16), bm_ref[...], preferred_element_type=F32)
        y = jnp.dot(x_prev[...].astype(BF16), cm_ref[...], preferred_element_type=F32) + d_ref[...] * wide(u_prev)
        y = 0.5 * y * (1.0 + jnp.tanh(0.7978845608028654 * (y + 0.044715 * (y * y * y))))
        z = jnp.dot(y.astype(BF16), gw_ref[...], preferred_element_type=F32) + gb_ref[...]
        y = y * (1.0 / (1.0 + jnp.exp(-z)))
        for hf in range(halves):
            y_ref[hf] = y[:, hf * LANES:(hf + 1) * LANES]
        for b in range(nbatch):
            for hf in range(halves):
                lanes = slice(b * WIDTH + hf * LANES, b * WIDTH + (hf + 1) * LANES)
                o_ref[:, lanes] = y_ref[hf, pl.ds(b, steps, stride=nbatch), :]

        def body(t, carry):
            xr, xi = carry
            off = pl.multiple_of(t * nbatch, nbatch)
            nr = lam_re * xr - lam_im * xi + x_cur[pl.ds(off, nbatch), 0:S5_NSTATE]
            ni = lam_re * xi + lam_im * xr + x_cur[pl.ds(off, nbatch), S5_NSTATE:2 * S5_NSTATE]
            x_cur[pl.ds(off, nbatch), 0:S5_NSTATE] = nr
            x_cur[pl.ds(off, nbatch), S5_NSTATE:2 * S5_NSTATE] = ni
            return nr, ni

        @pl.when(c < nchunks)
        def _():
            xr, xi = lax.fori_loop(0, steps, body, (st_ref[:, 0:S5_NSTATE], st_ref[:, S5_NSTATE:2 * S5_NSTATE]),
                                   unroll=8)
            st_ref[:, 0:S5_NSTATE] = xr
            st_ref[:, S5_NSTATE:2 * S5_NSTATE] = xi

    @pl.when(c % 2 == 0)
    def _():
        chunk(xa_ref, xb_ref, ua_ref, ub_ref)

    @pl.when(c % 2 == 1)
    def _():
        chunk(xb_ref, xa_ref, ub_ref, ua_ref)


def _out_kernel(x_ref, a_ref, b_ref, sg_ref, c_ref, d_ref, w_ref, fg_ref, o_ref, *, final):
    b = (b_ref[...] * _silu(sg_ref[0])).astype(BF16)
    y = (jnp.dot(a_ref[0], w_ref[0:WIDTH, :], preferred_element_type=F32)
         + jnp.dot(b, w_ref[WIDTH:2 * WIDTH, :], preferred_element_type=F32)
         + jnp.dot(c_ref[0], w_ref[2 * WIDTH:3 * WIDTH, :], preferred_element_type=F32)
         + jnp.dot(d_ref[0], w_ref[3 * WIDTH:4 * WIDTH, :], preferred_element_type=F32))
    xn = x_ref[0] + y
    if final:
        xn = _rms(xn, fg_ref[...])
    o_ref[0] = xn


def _tiles(seq):
    tm = 512 if seq % 512 == 0 else MOBA_BLOCK
    tm_out = 1024 if seq % 1024 == 0 else tm
    steps = 128
    return tm, tm_out, steps


def _arrange_w_in(w):
    z = lambda n: jnp.zeros((D_MODEL, n), w.dtype)
    o = 0
    parts = {}
    for name, n in (("fq", 256), ("fk", 256), ("fv", 256), ("fg", 256), ("ff", 4), ("su", 256), ("sg", 256),
                    ("mq", 256), ("mk", 256), ("mv", 256), ("mg", 256), ("cq", 384), ("ckv", 128), ("kr", 32),
                    ("lg", 256)):
        parts[name] = w[:, o:o + n]
        o += n
    cols = [parts["ff"], z(MLA_NOPE - 4), parts["kr"], z(LANES - MLA_NOPE - MLA_ROPE),
            parts["ckv"], parts["cq"], parts["fq"], parts["fk"], parts["mq"], parts["mk"],
            parts["fg"], parts["su"], parts["sg"], parts["mg"], parts["lg"]]
    w_vt = jnp.concatenate([parts["fv"], parts["mv"]], axis=1).T
    return jnp.concatenate(cols, axis=1).astype(BF16), w_vt.astype(BF16)


def _fox_bias_base(hd):
    return (HEAD_DIM if hd % 2 == 0 else 0) + FOX_BIAS_LANES * (hd // 2)


def _fox_routing():
    import numpy as np
    route = np.zeros((3 * LANES, 2 * LANES), np.float32)
    const = np.zeros((1, 2 * LANES), np.float32)
    for hd in range(N_HEADS):
        base = _fox_bias_base(hd)
        for part in range(3):
            route[part * LANES + hd, base + part] = 1.0
            const[0, base + 3 + part] = 1.0
            const[0, LANES + base + part] = 1.0
            route[part * LANES + hd, LANES + base + 3 + part] = -1.0
    return jnp.asarray(route, BF16), jnp.asarray(const, F32)


def _rope_tables(seq):
    pos = jnp.arange(seq).astype(F32)[:, None]
    lane = jnp.arange(LANES)
    half = HEAD_DIM // 2
    inv = jnp.power(ROPE_THETA, -jnp.arange(half, dtype=F32) / half)
    ang = pos * inv[None, :]
    cos, sin = jnp.cos(ang)[:, lane % half], jnp.sin(ang)[:, lane % half]
    lo = (lane % HEAD_DIM) < half
    moba = (cos, jnp.where(lo, -sin, 0.0), jnp.where(lo, 0.0, sin))
    half = MLA_ROPE // 2
    inv = jnp.power(ROPE_THETA, -jnp.arange(half, dtype=F32) / half)
    ang = pos * inv[None, :]
    cos, sin = jnp.cos(ang)[:, lane % half], jnp.sin(ang)[:, lane % half]
    in_lo = (lane >= MLA_NOPE) & (lane < MLA_NOPE + half)
    in_hi = (lane >= MLA_NOPE + half) & (lane < MLA_NOPE + MLA_ROPE)
    mla = (jnp.where(in_lo | in_hi, cos, 1.0), jnp.where(in_lo, -sin, 0.0), jnp.where(in_hi, sin, 0.0))
    return moba, mla


def _arrange_mla(w_uq, w_ukv):
    z = lambda r, n: jnp.zeros((r, n), F32)
    dq = MLA_NOPE + MLA_ROPE
    q_cols, k_cols, v_cols = [], [], []
    for hd in range(N_HEADS):
        q_cols += [w_uq[:, hd * dq:(hd + 1) * dq], z(MLA_Q_RANK, LANES - dq)]
        base = hd * (MLA_NOPE + MLA_V)
        k_cols += [w_ukv[:, base:base + MLA_NOPE], z(MLA_KV_RANK, LANES - MLA_NOPE)]
        v_cols += [w_ukv[:, base + MLA_NOPE:base + MLA_NOPE + MLA_V]]
    return (jnp.concatenate(q_cols, axis=1).astype(BF16), jnp.concatenate(k_cols, axis=1).astype(BF16),
            jnp.concatenate(v_cols, axis=1).T.astype(BF16))


def _s5_matrices(a_re, a_im, log_dt, b_re, b_im, c_re, c_im):
    lam = lax.complex(a_re.astype(F32), a_im.astype(F32))
    dt = jnp.exp(log_dt.astype(F32))[:, None]
    lam_bar = jnp.exp(lam * dt)
    b_bar = ((lam_bar - 1.0) / lam)[..., None] * lax.complex(b_re.astype(F32), b_im.astype(F32))
    eye = jnp.eye(S5_GROUPS, dtype=F32)
    blockdiag_in = lambda t: jnp.einsum('gpc,gh->gchp', t, eye).reshape(WIDTH, S5_NSTATE)
    blockdiag_out = lambda t: jnp.einsum('gcp,gh->gphc', t, eye).reshape(S5_NSTATE, WIDTH)
    bm = jnp.concatenate([blockdiag_in(b_bar.real), blockdiag_in(b_bar.imag)], axis=1).astype(BF16)
    cm = jnp.concatenate([blockdiag_out(c_re.astype(F32)), -blockdiag_out(c_im.astype(F32))], axis=0).astype(BF16)
    lam2 = jnp.stack([lam_bar.real.reshape(S5_NSTATE), lam_bar.imag.reshape(S5_NSTATE)], axis=0)
    return bm, lam2, cm


def _full(shape):
    return pl.BlockSpec(shape, lambda *_: (0,) * len(shape))


def _in_call(x, g, w, fb, tri, route, bconst, moba_t, mla_t, gq, gkv, wuq, wukv, wvt, wuvt, tm):
    nb, seq, _ = x.shape
    tok = lambda width: pl.BlockSpec((1, tm, width), lambda b, s: (b, s, 0))
    tab = pl.BlockSpec((tm, LANES), lambda b, s: (s, 0))
    bf = lambda width: jax.ShapeDtypeStruct((nb, seq, width), BF16)
    f32 = lambda width: jax.ShapeDtypeStruct((nb, seq, width), F32)
    val_t = jax.ShapeDtypeStruct((nb, seq // tm, WIDTH, tm), BF16)
    val_t_spec = pl.BlockSpec((1, 1, WIDTH, tm), lambda b, s: (b, s, 0, 0))
    out_shape = (bf(512), bf(512), val_t, f32(256),
                 jax.ShapeDtypeStruct((seq, nb * WIDTH), F32), f32(256),
                 bf(512), bf(512), val_t, f32(256),
                 bf(512), bf(512), val_t, f32(256),
                 jax.ShapeDtypeStruct((nb, seq // tm, SUBLANES, LANES), F32))
    out_specs = (tok(512), tok(512), val_t_spec, tok(256),
                 pl.BlockSpec((tm, WIDTH), lambda b, s: (s, b)), tok(256),
                 tok(512), tok(512), val_t_spec, tok(256),
                 tok(512), tok(512), val_t_spec, tok(256),
                 pl.BlockSpec((1, 1, SUBLANES, LANES), lambda b, s: (b, s, 0, 0)))
    in_specs = [tok(D_MODEL), _full((1, D_MODEL)), _full((D_MODEL, N_COLS)), _full((1, LANES)),
                _full((tm, tm)), _full((3 * LANES, 2 * LANES)), _full((1, 2 * LANES)),
                tab, tab, tab, tab, tab, tab,
                _full((1, MLA_Q_RANK)), _full((1, MLA_KV_RANK)),
                _full((MLA_Q_RANK, N_HEADS * LANES)), _full((MLA_KV_RANK, N_HEADS * LANES)),
                _full((2 * WIDTH, D_MODEL)), _full((WIDTH, MLA_KV_RANK))]
    return pl.pallas_call(
        functools.partial(_in_kernel, tm=tm),
        grid=(nb, seq // tm),
        in_specs=in_specs, out_specs=out_specs, out_shape=out_shape,
        scratch_shapes=[pltpu.VMEM((1, LANES), F32), pltpu.VMEM((LANES, WIDTH), F32)],
        compiler_params=pltpu.CompilerParams(dimension_semantics=("arbitrary", "arbitrary"),
                                             vmem_limit_bytes=VMEM_LIMIT),
        name="in_proj",
    )(x, g, w, fb, tri, route, bconst, *moba_t, *mla_t, gq, gkv, wuq, wukv, wvt, wuvt)


def _first_blocks(nb, nq, fstat=None):
    tiles = jnp.arange(nq)
    past = tiles[None, :] < tiles[:, None]
    if fstat is None:
        need = jnp.broadcast_to(past, (nb, 2, nq, nq))
    else:
        qn, kn = fstat[:, :, 0, :N_HEADS] * NORM_SLACK, fstat[:, :, 1, :N_HEADS] * NORM_SLACK
        cmax, cmin = fstat[:, :, 2, :N_HEADS], fstat[:, :, 3, :N_HEADS]
        gap = (qn[:, :, None] * kn[:, None, :] + cmax[:, :, None] - cmin[:, None, :]
               + (qn * kn)[:, :, None])
        need = (gap > -SKIP_GAP) & past[None, :, :, None]
        need = need.reshape(nb, nq, nq, 2, 2).any(-1).transpose(0, 3, 1, 2)
    first = jnp.where(need.any(-1), jnp.argmax(need, axis=-1), nq)
    first = jnp.minimum(first, jnp.maximum(tiles - 1, 0))
    count = jnp.sum(tiles - first, axis=-1, keepdims=True)
    return jnp.concatenate([first, count], axis=-1).astype(jnp.int32).reshape(-1)


def _attn_call(first_blocks, q, k, vt, gate, t, name):
    nb, seq, _ = q.shape
    whole = lambda width: pl.BlockSpec((1, seq, width), lambda b, p, tbl: (b, 0, p))
    return pl.pallas_call(
        functools.partial(_attn_kernel, t=t, nq=seq // t),
        grid_spec=pltpu.PrefetchScalarGridSpec(
            num_scalar_prefetch=1, grid=(nb, 2),
            in_specs=[whole(2 * LANES), whole(2 * LANES),
                      pl.BlockSpec((1, seq // t, LANES, t), lambda b, p, tbl: (b, 0, p, 0)), whole(LANES)],
            out_specs=whole(LANES),
            scratch_shapes=[pltpu.VMEM((2, t, t), F32), pltpu.VMEM((2, t, t), F32),
                            pltpu.VMEM((2, 1, t), F32), pltpu.VMEM((2, 1, t), F32),
                            pltpu.VMEM((seq // t, 2, 1, t), F32),
                            pltpu.VMEM((seq // t, 2, HEAD_DIM + DEN_ROWS, t), F32),
                            pltpu.VMEM((seq // t, 2, LANES, t), BF16),
                            pltpu.SMEM((2,), jnp.int32)]),
        out_shape=jax.ShapeDtypeStruct((nb, seq, WIDTH), BF16),
        compiler_params=pltpu.CompilerParams(dimension_semantics=("arbitrary", "arbitrary"),
                                             vmem_limit_bytes=VMEM_LIMIT),
        name=name,
    )(first_blocks, q, k, vt, gate)


def _s5_call(u, bm, lam2, cm, d, gw, gb, steps, nbatch):
    seq = u.shape[0]
    blk = steps * nbatch
    nchunks = seq // steps
    prev = lambda c: (jnp.maximum(c - 1, 0), 0)
    return pl.pallas_call(
        functools.partial(_s5_kernel, steps=steps, nbatch=nbatch, nchunks=nchunks),
        grid=(nchunks + 1,),
        in_specs=[pl.BlockSpec((steps, nbatch * WIDTH), lambda c: (jnp.minimum(c, nchunks - 1), 0)),
                  _full((WIDTH, 2 * S5_NSTATE)), _full((2, S5_NSTATE)), _full((2 * S5_NSTATE, WIDTH)),
                  _full((1, WIDTH)), _full((WIDTH, WIDTH)), _full((1, WIDTH))],
        out_specs=pl.BlockSpec((steps, nbatch * WIDTH), prev),
        out_shape=jax.ShapeDtypeStruct((seq, nbatch * WIDTH), F32),
        scratch_shapes=[pltpu.VMEM((blk, 2 * S5_NSTATE), F32), pltpu.VMEM((blk, 2 * S5_NSTATE), F32),
                        *[pltpu.VMEM((WIDTH // LANES, blk, LANES), F32)] * 3,
                        pltpu.VMEM((nbatch, 2 * S5_NSTATE), F32)],
        compiler_params=pltpu.CompilerParams(dimension_semantics=("arbitrary",),
                                             vmem_limit_bytes=VMEM_LIMIT),
        name="s5",
    )(u, bm, lam2, cm, d, gw, gb)


def _out_call(x, a, b2d, sg, c, d, w, fg, tm, final):
    nb, seq, _ = x.shape
    tok = lambda width: pl.BlockSpec((1, tm, width), lambda b, s: (b, s, 0))
    return pl.pallas_call(
        functools.partial(_out_kernel, final=final),
        grid=(nb, seq // tm),
        in_specs=[tok(D_MODEL), tok(WIDTH), pl.BlockSpec((tm, WIDTH), lambda b, s: (s, b)), tok(WIDTH),
                  tok(WIDTH), tok(WIDTH), _full((4 * WIDTH, D_MODEL)), _full((1, D_MODEL))],
        out_specs=tok(D_MODEL),
        out_shape=jax.ShapeDtypeStruct(x.shape, F32),
        compiler_params=pltpu.CompilerParams(dimension_semantics=("arbitrary", "arbitrary"),
                                             vmem_limit_bytes=VMEM_LIMIT),
        name="out_proj",
    )(x, a, b2d, sg, c, d, w, fg)


def kernel(x, norm_g, w_in, fox_fb, s5_a_re, s5_a_im, s5_log_dt, s5_b_re, s5_b_im, s5_c_re, s5_c_im, s5_d,
           s5_glu_w, s5_glu_b, mla_q_norm, mla_w_uq, mla_kv_norm, mla_w_ukv, w_out, final_g):
    nb, seq, _ = x.shape
    depth = norm_g.shape[0]
    assert nb == 8, "the S5 recurrence keeps the batch on the 8 sublanes of a vreg"
    tm, tm_out, steps = _tiles(seq)
    assert seq % tm == 0 and seq % steps == 0 and seq // MOBA_BLOCK <= 32

    tri = jnp.tril(jnp.ones((tm, tm), F32)).astype(BF16)
    route, bconst = _fox_routing()
    moba_t, mla_t = _rope_tables(seq)
    fgain = final_g.astype(F32).reshape(1, D_MODEL)
    every_block = _first_blocks(nb, seq // tm)

    for l in range(depth):
        w, wvt = _arrange_w_in(w_in[l])
        fb = jnp.zeros((1, LANES), F32).at[0, :N_HEADS].set(fox_fb[l].astype(F32))
        wuq, wukv, wuvt = _arrange_mla(mla_w_uq[l].astype(F32), mla_w_ukv[l].astype(F32))
        (foxq, foxk, foxv, fg, su, sg, mobq, mobk, mobv, mg, mlaq, mlak, mlav, lg, fstat) = _in_call(
            x, norm_g[l].astype(F32).reshape(1, D_MODEL), w, fb, tri, route, bconst, moba_t, mla_t,
            mla_q_norm[l].astype(F32).reshape(1, MLA_Q_RANK), mla_kv_norm[l].astype(F32).reshape(1, MLA_KV_RANK),
            wuq, wukv, wvt, wuvt, tm)

        a_out = _attn_call(_first_blocks(nb, seq // tm, fstat), foxq, foxk, foxv, fg, tm, "fox_attn")
        c_out = _attn_call(every_block, mobq, mobk, mobv, mg, tm, "moba_attn")
        d_out = _attn_call(every_block, mlaq, mlak, mlav, lg, tm, "mla_attn")

        bm, lam2, cm = _s5_matrices(s5_a_re[l], s5_a_im[l], s5_log_dt[l], s5_b_re[l], s5_b_im[l],
                                    s5_c_re[l], s5_c_im[l])
        b_out = _s5_call(su, bm, lam2, cm, s5_d[l].astype(F32).reshape(1, WIDTH), s5_glu_w[l].astype(BF16),
                         s5_glu_b[l].astype(F32).reshape(1, WIDTH), steps, nb)
        x = _out_call(x, a_out, b_out, sg, c_out, d_out, w_out[l].astype(BF16), fgain, tm_out,
                      final=(l == depth - 1))
    return x
```

```python
import functools

import jax
import jax.numpy as jnp
from jax import lax
from jax.experimental import pallas as pl
from jax.experimental.pallas import tpu as pltpu

F32 = jnp.float32
BF16 = jnp.bfloat16

D_MODEL = 1024
HEAD_DIM = 64
DEN_ROWS = 16
PAST_UNROLL = 16
DIAG_UNROLL = 4
SKIP_GAP = 152.0
NORM_SLACK = 1.01
N_HEADS = 4
WIDTH = 256
S5_GROUPS = 16
S5_GROUP = 16
S5_STATE = 64
S5_NSTATE = S5_GROUPS * S5_STATE
MOBA_BLOCK = 256
MOBA_TOPK = 3
MAX_BLOCKS = 32
GROUP_HEAD = (1, 3, 0, 2)
LOG2E = 1.4426950408889634
MLA_NOPE = 64
MLA_ROPE = 32
MLA_V = 64
MLA_Q_RANK = 384
MLA_KV_RANK = 128
ROPE_THETA = 10000.0
EPS = 1e-6
NEG = -1e30
LANES = 128
SUBLANES = 8
LOG2_HEAD_DIM = HEAD_DIM.bit_length() - 1
LOG2_MOBA_BLOCK = MOBA_BLOCK.bit_length() - 1
LOG2_MAX_BLOCKS = MAX_BLOCKS.bit_length() - 1
VMEM_LIMIT = 56 * 1024 * 1024

C_FFKR = 0
C_CKV, C_CQ = 128, 256
C_FQ, C_FK = 640, 896
C_MQ, C_MK = 1152, 1408
C_FG, C_SU, C_SG, C_MG, C_LG = 1664, 1920, 2176, 2432, 2688
N_COLS = 2944
PROJ_GROUPS = ((0, 1152), (1152, 512), (1664, 1280))
FOX_BIAS_LANES = 6


def _split3(x):
    x1 = x.astype(BF16)
    r1 = x - x1.astype(F32)
    x2 = r1.astype(BF16)
    r2 = r1 - x2.astype(F32)
    return x1, x2, r2.astype(BF16)


def _rope(x, cos, sin_lo, sin_hi, half):
    return (x * cos + pltpu.roll(x, LANES - half, 1) * sin_lo + pltpu.roll(x, half, 1) * sin_hi)


def _rms(x, g):
    return x * lax.rsqrt(jnp.mean(x * x, axis=-1, keepdims=True) + EPS) * g


def _silu(g):
    return g * (1.0 / (1.0 + jnp.exp(-g)))


def _in_kernel(x_ref, g_ref, w_ref, fb_ref, tri_ref, route_ref, bconst_ref,
               mc_ref, msl_ref, msh_ref, lc_ref, lsl_ref, lsh_ref,
               gq_ref, gkv_ref, wuq_ref, wukv_ref, wvt_ref, wuvt_ref,
               foxq_ref, foxk_ref, foxv_ref, fg_ref, su_ref, sg_ref,
               mobq_ref, mobk_ref, mobv_ref, mg_ref,
               mlaq_ref, mlak_ref, mlav_ref, lg_ref, fstat_ref,
               carry_ref, km_ref, *, tm):
    sblk = pl.program_id(1)
    nblk = tm // MOBA_BLOCK

    @pl.when(sblk == 0)
    def _():
        carry_ref[...] = jnp.zeros_like(carry_ref)
        km_ref[...] = jnp.zeros_like(km_ref)

    h = _rms(x_ref[0], g_ref[...]).astype(BF16)

    z = {}

    def proj(c0, width):
        for g0, gw in PROJ_GROUPS:
            if g0 <= c0 and c0 + width <= g0 + gw:
                if g0 not in z:
                    z[g0] = jnp.dot(h, w_ref[:, g0:g0 + gw], preferred_element_type=F32)
                return z[g0][:, c0 - g0:c0 - g0 + width]
        raise ValueError("column range crosses a projection group")

    lane = lax.broadcasted_iota(jnp.int32, (tm, LANES), 1)
    row = lax.broadcasted_iota(jnp.int32, (tm, LANES), 0)
    low_half = lane < HEAD_DIM

    def proj_t(wt):
        return lax.dot_general(wt, h, (((1,), (1,)), ((), ())), preferred_element_type=F32)

    for g0, gw in PROJ_GROUPS:
        proj(g0, gw)
    fg_ref[0] = proj(C_FG, WIDTH)
    sg_ref[0] = proj(C_SG, WIDTH)
    mg_ref[0] = proj(C_MG, WIDTH)
    lg_ref[0] = proj(C_LG, WIDTH)
    su_ref[...] = proj(C_SU, WIDTH)

    ffkr = proj(C_FFKR, LANES)
    ff = ffkr + fb_ref[...]
    logf = -(jnp.maximum(-ff, 0.0) + jnp.log1p(jnp.exp(-jnp.abs(ff))))
    within3 = jnp.dot(tri_ref[...], jnp.concatenate(_split3(logf), axis=1), preferred_element_type=F32)
    within = within3[:, 0:LANES] + within3[:, LANES:2 * LANES] + within3[:, 2 * LANES:]
    cum = within + carry_ref[...]
    carry_ref[...] = cum[tm - 1:tm, :]
    cum2 = cum * LOG2E
    routed = jnp.dot(jnp.concatenate(_split3(cum2), axis=1), route_ref[...],
                     preferred_element_type=F32) + bconst_ref[...]
    fq = proj(C_FQ, WIDTH) * (HEAD_DIM ** -0.5 * LOG2E)
    fk = proj(C_FK, WIDTH)
    lane_row = lane[0:1, :]
    qmax_row = jnp.zeros((1, LANES), F32)
    kmax_row = jnp.zeros((1, LANES), F32)
    for hd in range(N_HEADS):
        pair = hd // 2
        own = low_half if hd % 2 == 0 else jnp.logical_not(low_half)
        base = _fox_bias_base(hd)
        mine = (lane >= base) & (lane < base + FOX_BIAS_LANES)
        sl = slice(pair * LANES, (pair + 1) * LANES)
        qb = fq[:, sl].astype(BF16)
        kb = fk[:, sl].astype(BF16)
        qa = jnp.where(own, qb, jnp.where(mine, routed[:, 0:LANES], 0.0).astype(BF16))
        ka = jnp.where(own, kb, jnp.where(mine, routed[:, LANES:], 0.0).astype(BF16))
        foxq_ref[0, :, hd * LANES:(hd + 1) * LANES] = qa
        foxk_ref[0, :, hd * LANES:(hd + 1) * LANES] = ka
        for rounded, is_q in ((qb, True), (kb, False)):
            r = jnp.where(own, rounded.astype(F32), 0.0)
            norm = jnp.sqrt(jnp.max(jnp.sum(r * r, axis=1, keepdims=True), axis=0, keepdims=True))
            if is_q:
                qmax_row = jnp.where(lane_row == hd, norm, qmax_row)
            else:
                kmax_row = jnp.where(lane_row == hd, norm, kmax_row)
    srow = lax.broadcasted_iota(jnp.int32, (SUBLANES, LANES), 0)
    fstat_ref[0, 0] = jnp.where(srow == 0, qmax_row, jnp.where(srow == 1, kmax_row, jnp.where(
        srow == 2, jnp.max(cum2, axis=0, keepdims=True), jnp.where(
            srow == 3, jnp.min(cum2, axis=0, keepdims=True), 0.0))))

    mc, msl, msh = mc_ref[...], msl_ref[...], msh_ref[...]
    mq = proj(C_MQ, WIDTH)
    mk = proj(C_MK, WIDTH)
    q_r = [_rope(mq[:, p * LANES:(p + 1) * LANES], mc, msl, msh, HEAD_DIM // 2) for p in range(2)]
    k_r = [_rope(mk[:, p * LANES:(p + 1) * LANES], mc, msl, msh, HEAD_DIM // 2) for p in range(2)]
    km_row = lax.broadcasted_iota(jnp.int32, (LANES, WIDTH), 0)
    km_lane = lax.broadcasted_iota(jnp.int32, (LANES, WIDTH), 1)
    km_grp = lax.shift_right_logical(km_row, LOG2_MAX_BLOCKS)
    km_head = jnp.where(km_grp == 0, GROUP_HEAD[0], jnp.where(km_grp == 1, GROUP_HEAD[1],
                        jnp.where(km_grp == 2, GROUP_HEAD[2], GROUP_HEAD[3])))
    km_own = lax.shift_right_logical(km_lane, LOG2_HEAD_DIM) == km_head
    kmt = km_ref[...]
    for nb in range(nblk):
        blk = sblk * nblk + nb
        km = jnp.concatenate(
            [jnp.mean(k_r[p][nb * MOBA_BLOCK:(nb + 1) * MOBA_BLOCK, :], axis=0, keepdims=True) for p in range(2)],
            axis=1)
        kmt = jnp.where((km_row & (MAX_BLOCKS - 1)) == blk, jnp.where(km_own, km, 0.0), kmt)
    km_ref[...] = kmt
    gate_t = lax.dot_general(kmt, jnp.concatenate(q_r, axis=1), (((1,), (1,)), ((), ())),
                             precision=lax.Precision.HIGHEST, preferred_element_type=F32)
    cand = lax.broadcasted_iota(jnp.int32, (MAX_BLOCKS, tm), 0).astype(F32)
    tok = lax.broadcasted_iota(jnp.int32, (MAX_BLOCKS, tm), 1)
    blk_tok = (sblk * nblk + lax.shift_right_logical(tok, LOG2_MOBA_BLOCK)).astype(F32)
    bias_rows = []
    for grp in range(N_HEADS):
        g = jnp.where(cand < blk_tok, gate_t[grp * MAX_BLOCKS:(grp + 1) * MAX_BLOCKS, :], -jnp.inf)
        chosen = jnp.zeros((MAX_BLOCKS, tm), F32)
        for _ in range(MOBA_TOPK):
            m = jnp.max(g, axis=0, keepdims=True)
            first = jnp.min(jnp.where(g == m, cand, 1e9), axis=0, keepdims=True)
            first = jnp.where(m > -jnp.inf, first, -1.0)
            pick = cand == first
            chosen = jnp.where(pick, 1.0, chosen)
            g = jnp.where(pick, -jnp.inf, g)
        keep = jnp.where(cand == blk_tok, 1.0, chosen)
        bias_rows.append(jnp.where(keep > 0.0, 0.0, NEG))
    sel_bias = jnp.concatenate(bias_rows, axis=0).T
    blk_row = sblk * nblk + lax.shift_right_logical(row, LOG2_MOBA_BLOCK)
    onehot = jnp.where((lane & (MAX_BLOCKS - 1)) == blk_row, 1.0, 0.0)
    lane_grp = lax.shift_right_logical(lane, LOG2_MAX_BLOCKS)
    for hd in range(N_HEADS):
        pair = hd // 2
        own = low_half if hd % 2 == 0 else jnp.logical_not(low_half)
        mine = lane_grp == GROUP_HEAD.index(hd)
        qa = jnp.where(own, q_r[pair] * (HEAD_DIM ** -0.5 * LOG2E), jnp.where(mine, sel_bias, 0.0))
        ka = jnp.where(own, k_r[pair], jnp.where(mine, onehot, 0.0))
        mobq_ref[0, :, hd * LANES:(hd + 1) * LANES] = qa.astype(BF16)
        mobk_ref[0, :, hd * LANES:(hd + 1) * LANES] = ka.astype(BF16)

    lc, lsl, lsh = lc_ref[...], lsl_ref[...], lsh_ref[...]
    cqn = _rms(proj(C_CQ, MLA_Q_RANK), gq_ref[...]).astype(BF16)
    qf = jnp.dot(cqn, wuq_ref[...], preferred_element_type=F32)
    ckvn = _rms(proj(C_CKV, MLA_KV_RANK), gkv_ref[...]).astype(BF16)
    kv = jnp.dot(ckvn, wukv_ref[...], preferred_element_type=F32)
    kr = jnp.where(low_half, 0.0, _rope(ffkr, lc, lsl, lsh, MLA_ROPE // 2))
    scale = (MLA_NOPE + MLA_ROPE) ** -0.5 * LOG2E
    for hd in range(N_HEADS):
        sl = slice(hd * LANES, (hd + 1) * LANES)
        mlaq_ref[0, :, sl] = (_rope(qf[:, sl], lc, lsl, lsh, MLA_ROPE // 2) * scale).astype(BF16)
        mlak_ref[0, :, sl] = (kv[:, sl] + kr).astype(BF16)
    mlav_ref[0, 0] = lax.dot_general(wuvt_ref[...], ckvn, (((1,), (1,)), ((), ())),
                                     preferred_element_type=F32).astype(BF16)
    foxv_ref[0, 0] = proj_t(wvt_ref[0:WIDTH, :]).astype(BF16)
    mobv_ref[0, 0] = proj_t(wvt_ref[WIDTH:2 * WIDTH, :]).astype(BF16)


def _attn_kernel(tbl_ref, q_ref, k_ref, vt_ref, g_ref, o_ref, sa_ref, sb_ref, mxa_ref, mxb_ref, m_ref, acc_ref,
                 qt_ref, cur_ref, *, t, nq):
    bufs = ((sa_ref, mxa_ref), (sb_ref, mxb_ref))
    half = t // 2
    causal = (lax.broadcasted_iota(jnp.int32, (half, half), 0)
              <= lax.broadcasted_iota(jnp.int32, (half, half), 1))
    ones = jnp.ones((DEN_ROWS, t), BF16)
    pair = lambda i, j: (jnp.int32(i), jnp.int32(j))

    def transpose_tile(i):
        i = jnp.minimum(i, nq - 1)
        rows = pl.ds(pl.multiple_of(i * t, t), t)
        for hh in range(2):
            qt_ref[i, hh] = q_ref[0, rows, hh * LANES:(hh + 1) * LANES].astype(F32).T.astype(BF16)

    def scores(hh, ij, buf):
        s_ref, mx_ref = buf
        koff = pl.multiple_of(ij[1] * t, t)
        st = jnp.dot(k_ref[0, pl.ds(koff, t), hh * LANES:(hh + 1) * LANES], qt_ref[ij[0], hh],
                     preferred_element_type=F32)
        s_ref[hh] = st
        mx_ref[hh] = jnp.max(st, axis=0, keepdims=True)

    def values(hh, j):
        return jnp.concatenate([vt_ref[0, j, hh * HEAD_DIM:(hh + 1) * HEAD_DIM, :], ones], axis=0)

    def diag_scores(hh, i, buf):
        koff = pl.multiple_of(i * t, t)
        qt = qt_ref[i, hh]
        buf[0][hh, 0:half, :] = jnp.dot(k_ref[0, pl.ds(koff, half), hh * LANES:(hh + 1) * LANES], qt,
                                        preferred_element_type=F32)
        buf[0][hh, half:, half:] = jnp.dot(k_ref[0, pl.ds(koff + half, half), hh * LANES:(hh + 1) * LANES],
                                           qt[:, half:], preferred_element_type=F32)

    def first_update(hh, i, buf):
        lo = buf[0][hh, 0:half, :]
        lo_lo = jnp.where(causal, lo[:, 0:half], NEG)
        hi_hi = jnp.where(causal, buf[0][hh, half:, half:], NEG)
        mx_lo = jnp.max(lo_lo, axis=0, keepdims=True)
        mx_hi = jnp.maximum(jnp.max(lo[:, half:], axis=0, keepdims=True), jnp.max(hi_hi, axis=0, keepdims=True))
        p_lo = jnp.concatenate([jnp.exp2(lo_lo - mx_lo), jnp.exp2(lo[:, half:] - mx_hi)], axis=1).astype(BF16)
        p_hi = jnp.exp2(hi_hi - mx_hi).astype(BF16)
        va = values(hh, i)
        acc_lo = jnp.dot(va[:, 0:half], p_lo, preferred_element_type=F32)
        acc_hi = jnp.dot(va[:, half:], p_hi, preferred_element_type=F32)
        acc_ref[i, hh, :, 0:half] = acc_lo[:, 0:half]
        acc_ref[i, hh, :, half:] = acc_lo[:, half:] + acc_hi
        m_ref[i, hh] = jnp.concatenate([mx_lo, mx_hi], axis=1)

    def update(hh, ij, buf):
        i, j = ij
        m = m_ref[i, hh]
        m_new = jnp.maximum(m, buf[1][hh])
        alpha = jnp.exp2(m - m_new)
        p = jnp.exp2(buf[0][hh] - m_new).astype(BF16)
        acc_ref[i, hh] = alpha * acc_ref[i, hh] + jnp.dot(values(hh, j), p, preferred_element_type=F32)
        m_ref[i, hh] = m_new

    def stage(nxt, nxt_buf, cur, cur_buf, consume):
        for hh in range(2):
            if nxt is not None:
                scores(hh, nxt, nxt_buf)
            consume(hh, cur, cur_buf)

    tbl = (pl.program_id(0) * 2 + pl.program_id(1)) * (nq + 1)
    first_block = lambda i: tbl_ref[tbl + i]
    n_past = tbl_ref[tbl + nq]
    first_past = pair(min(1, nq - 1), 0)

    def diag_stage(nxt_diag, nxt_past, nxt_buf, cur, cur_buf):
        for hh in range(2):
            if nxt_diag is not None:
                diag_scores(hh, nxt_diag, nxt_buf)
            if nxt_past is not None:
                scores(hh, nxt_past, nxt_buf)
            first_update(hh, cur, cur_buf)

    for i in range(min(DIAG_UNROLL + 1, nq)):
        transpose_tile(jnp.int32(i))
    for hh in range(2):
        diag_scores(hh, jnp.int32(0), bufs[0])

    def diagonals(n, carry):
        i0 = DIAG_UNROLL * n
        for k in range(DIAG_UNROLL):
            diag_stage(i0 + k + 1, None, bufs[(k + 1) % 2], i0 + k, bufs[k % 2])
        for k in range(DIAG_UNROLL):
            transpose_tile(i0 + DIAG_UNROLL + 1 + k)
        return carry

    looped = DIAG_UNROLL * ((nq - 1) // DIAG_UNROLL)
    lax.fori_loop(0, looped // DIAG_UNROLL, diagonals, 0)
    for i in range(looped, nq):
        last = i == nq - 1
        diag_stage(None if last else jnp.int32(i + 1), first_past if last and nq > 1 else None,
                   bufs[(i + 1) % 2], jnp.int32(i), bufs[i % 2])
    par = nq % 2

    def succ(ij):
        i, j = ij
        last = j == i - 1
        up = jnp.minimum(i + 1, nq - 1)
        return jnp.where(last, up, i), jnp.where(last, first_block(up), j + 1)

    def past_steps(count, cur):
        for k in range(count):
            nxt = succ(cur)
            stage(nxt, bufs[(par + k + 1) % 2], cur, bufs[(par + k) % 2], update)
            cur = nxt
        return cur

    cur = lax.fori_loop(0, n_past // PAST_UNROLL, lambda _, c: past_steps(PAST_UNROLL, c), first_past)
    cur_ref[0], cur_ref[1] = cur
    count = PAST_UNROLL // 2
    while count:
        @pl.when((n_past & count) != 0)
        def _(count=count):
            cur_ref[0], cur_ref[1] = past_steps(count, (cur_ref[0], cur_ref[1]))
        count //= 2

    def finish_tile(i, carry):
        out_t = jnp.concatenate([acc_ref[i, hh, 0:HEAD_DIM] / acc_ref[i, hh, HEAD_DIM:HEAD_DIM + 1]
                                 for hh in range(2)], axis=0)
        rows = pl.ds(pl.multiple_of(i * t, t), t)
        o_ref[0, rows, :] = (out_t.T * _silu(g_ref[0, rows, :])).astype(BF16)
        return carry

    lax.fori_loop(0, nq, finish_tile, 0)


def _s5_kernel(u_ref, bm_ref, lam_ref, cm_ref, d_ref, gw_ref, gb_ref, o_ref,
               xa_ref, xb_ref, ua_ref, ub_ref, y_ref, st_ref, *, steps, nbatch, nchunks):
    c = pl.program_id(0)

    @pl.when(c == 0)
    def _():
        st_ref[...] = jnp.zeros_like(st_ref)
        xb_ref[...] = jnp.zeros_like(xb_ref)
        ub_ref[...] = jnp.zeros_like(ub_ref)

    lam_re = jnp.broadcast_to(lam_ref[0:1, :], (nbatch, S5_NSTATE))
    lam_im = jnp.broadcast_to(lam_ref[1:2, :], (nbatch, S5_NSTATE))

    halves = WIDTH // LANES
    wide = lambda ref: jnp.concatenate([ref[hf] for hf in range(halves)], axis=1)

    def chunk(x_cur, x_prev, u_cur, u_prev):
        for b in range(nbatch):
            for hf in range(halves):
                lanes = slice(b * WIDTH + hf * LANES, b * WIDTH + (hf + 1) * LANES)
                u_cur[hf, pl.ds(b, steps, stride=nbatch), :] = u_ref[:, lanes]
        x_cur[...] = jnp.dot(wide(u_cur).astype(BF16), bm_ref[...], preferred_element_type=F32)
        y = jnp.dot(x_prev[...].astype(BF16), cm_ref[...], preferred_element_type=F32) + d_ref[...] * wide(u_prev)
        y = 0.5 * y * (1.0 + jnp.tanh(0.7978845608028654 * (y + 0.044715 * (y * y * y))))
        z = jnp.dot(y.astype(BF16), gw_ref[...], preferred_element_type=F32) + gb_ref[...]
        y = y * (1.0 / (1.0 + jnp.exp(-z)))
        for hf in range(halves):
            y_ref[hf] = y[:, hf * LANES:(hf + 1) * LANES]
        for b in range(nbatch):
            for hf in range(halves):
                lanes = slice(b * WIDTH + hf * LANES, b * WIDTH + (hf + 1) * LANES)
                o_ref[:, lanes] = y_ref[hf, pl.ds(b, steps, stride=nbatch), :]

        def body(t, carry):
            xr, xi = carry
            off = pl.multiple_of(t * nbatch, nbatch)
            nr = lam_re * xr - lam_im * xi + x_cur[pl.ds(off, nbatch), 0:S5_NSTATE]
            ni = lam_re * xi + lam_im * xr + x_cur[pl.ds(off, nbatch), S5_NSTATE:2 * S5_NSTATE]
            x_cur[pl.ds(off, nbatch), 0:S5_NSTATE] = nr
            x_cur[pl.ds(off, nbatch), S5_NSTATE:2 * S5_NSTATE] = ni
            return nr, ni

        @pl.when(c < nchunks)
        def _():
            xr, xi = lax.fori_loop(0, steps, body, (st_ref[:, 0:S5_NSTATE], st_ref[:, S5_NSTATE:2 * S5_NSTATE]),
                                   unroll=8)
            st_ref[:, 0:S5_NSTATE] = xr
            st_ref[:, S5_NSTATE:2 * S5_NSTATE] = xi

    @pl.when(c % 2 == 0)
    def _():
        chunk(xa_ref, xb_ref, ua_ref, ub_ref)

    @pl.when(c % 2 == 1)
    def _():
        chunk(xb_ref, xa_ref, ub_ref, ua_ref)


def _out_kernel(x_ref, a_ref, b_ref, sg_ref, c_ref, d_ref, w_ref, fg_ref, o_ref, *, final):
    b = (b_ref[...] * _silu(sg_ref[0])).astype(BF16)
    y = (jnp.dot(a_ref[0], w_ref[0:WIDTH, :], preferred_element_type=F32)
         + jnp.dot(b, w_ref[WIDTH:2 * WIDTH, :], preferred_element_type=F32)
         + jnp.dot(c_ref[0], w_ref[2 * WIDTH:3 * WIDTH, :], preferred_element_type=F32)
         + jnp.dot(d_ref[0], w_ref[3 * WIDTH:4 * WIDTH, :], preferred_element_type=F32))
    xn = x_ref[0] + y
    if final:
        xn = _rms(xn, fg_ref[...])
    o_ref[0] = xn


def _tiles(seq):
    tm = 512 if seq % 512 == 0 else MOBA_BLOCK
    tm_out = 1024 if seq % 1024 == 0 else tm
    steps = 128
    return tm, tm_out, steps


def _arrange_w_in(w):
    z = lambda n: jnp.zeros((D_MODEL, n), w.dtype)
    o = 0
    parts = {}
    for name, n in (("fq", 256), ("fk", 256), ("fv", 256), ("fg", 256), ("ff", 4), ("su", 256), ("sg", 256),
                    ("mq", 256), ("mk", 256), ("mv", 256), ("mg", 256), ("cq", 384), ("ckv", 128), ("kr", 32),
                    ("lg", 256)):
        parts[name] = w[:, o:o + n]
        o += n
    cols = [parts["ff"], z(MLA_NOPE - 4), parts["kr"], z(LANES - MLA_NOPE - MLA_ROPE),
            parts["ckv"], parts["cq"], parts["fq"], parts["fk"], parts["mq"], parts["mk"],
            parts["fg"], parts["su"], parts["sg"], parts["mg"], parts["lg"]]
    w_vt = jnp.concatenate([parts["fv"], parts["mv"]], axis=1).T
    return jnp.concatenate(cols, axis=1).astype(BF16), w_vt.astype(BF16)


def _fox_bias_base(hd):
    return (HEAD_DIM if hd % 2 == 0 else 0) + FOX_BIAS_LANES * (hd // 2)


def _fox_routing():
    import numpy as np
    route = np.zeros((3 * LANES, 2 * LANES), np.float32)
    const = np.zeros((1, 2 * LANES), np.float32)
    for hd in range(N_HEADS):
        base = _fox_bias_base(hd)
        for part in range(3):
            route[part * LANES + hd, base + part] = 1.0
            const[0, base + 3 + part] = 1.0
            const[0, LANES + base + part] = 1.0
            route[part * LANES + hd, LANES + base + 3 + part] = -1.0
    return jnp.asarray(route, BF16), jnp.asarray(const, F32)


def _rope_tables(seq):
    pos = jnp.arange(seq).astype(F32)[:, None]
    lane = jnp.arange(LANES)
    half = HEAD_DIM // 2
    inv = jnp.power(ROPE_THETA, -jnp.arange(half, dtype=F32) / half)
    ang = pos * inv[None, :]
    cos, sin = jnp.cos(ang)[:, lane % half], jnp.sin(ang)[:, lane % half]
    lo = (lane % HEAD_DIM) < half
    moba = (cos, jnp.where(lo, -sin, 0.0), jnp.where(lo, 0.0, sin))
    half = MLA_ROPE // 2
    inv = jnp.power(ROPE_THETA, -jnp.arange(half, dtype=F32) / half)
    ang = pos * inv[None, :]
    cos, sin = jnp.cos(ang)[:, lane % half], jnp.sin(ang)[:, lane % half]
    in_lo = (lane >= MLA_NOPE) & (lane < MLA_NOPE + half)
    in_hi = (lane >= MLA_NOPE + half) & (lane < MLA_NOPE + MLA_ROPE)
    mla = (jnp.where(in_lo | in_hi, cos, 1.0), jnp.where(in_lo, -sin, 0.0), jnp.where(in_hi, sin, 0.0))
    return moba, mla


def _arrange_mla(w_uq, w_ukv):
    z = lambda r, n: jnp.zeros((r, n), F32)
    dq = MLA_NOPE + MLA_ROPE
    q_cols, k_cols, v_cols = [], [], []
    for hd in range(N_HEADS):
        q_cols += [w_uq[:, hd * dq:(hd + 1) * dq], z(MLA_Q_RANK, LANES - dq)]
        base = hd * (MLA_NOPE + MLA_V)
        k_cols += [w_ukv[:, base:base + MLA_NOPE], z(MLA_KV_RANK, LANES - MLA_NOPE)]
        v_cols += [w_ukv[:, base + MLA_NOPE:base + MLA_NOPE + MLA_V]]
    return (jnp.concatenate(q_cols, axis=1).astype(BF16), jnp.concatenate(k_cols, axis=1).astype(BF16),
            jnp.concatenate(v_cols, axis=1).T.astype(BF16))


def _s5_matrices(a_re, a_im, log_dt, b_re, b_im, c_re, c_im):
    lam = lax.complex(a_re.astype(F32), a_im.astype(F32))
    dt = jnp.exp(log_dt.astype(F32))[:, None]
    lam_bar = jnp.exp(lam * dt)
    b_bar = ((lam_bar - 1.0) / lam)[..., None] * lax.complex(b_re.astype(F32), b_im.astype(F32))
    eye = jnp.eye(S5_GROUPS, dtype=F32)
    blockdiag_in = lambda t: jnp.einsum('gpc,gh->gchp', t, eye).reshape(WIDTH, S5_NSTATE)
    blockdiag_out = lambda t: jnp.einsum('gcp,gh->gphc', t, eye).reshape(S5_NSTATE, WIDTH)
    bm = jnp.concatenate([blockdiag_in(b_bar.real), blockdiag_in(b_bar.imag)], axis=1).astype(BF16)
    cm = jnp.concatenate([blockdiag_out(c_re.astype(F32)), -blockdiag_out(c_im.astype(F32))], axis=0).astype(BF16)
    lam2 = jnp.stack([lam_bar.real.reshape(S5_NSTATE), lam_bar.imag.reshape(S5_NSTATE)], axis=0)
    return bm, lam2, cm


def _full(shape):
    return pl.BlockSpec(shape, lambda *_: (0,) * len(shape))


def _in_call(x, g, w, fb, tri, route, bconst, moba_t, mla_t, gq, gkv, wuq, wukv, wvt, wuvt, tm):
    nb, seq, _ = x.shape
    tok = lambda width: pl.BlockSpec((1, tm, width), lambda b, s: (b, s, 0))
    tab = pl.BlockSpec((tm, LANES), lambda b, s: (s, 0))
    bf = lambda width: jax.ShapeDtypeStruct((nb, seq, width), BF16)
    f32 = lambda width: jax.ShapeDtypeStruct((nb, seq, width), F32)
    val_t = jax.ShapeDtypeStruct((nb, seq // tm, WIDTH, tm), BF16)
    val_t_spec = pl.BlockSpec((1, 1, WIDTH, tm), lambda b, s: (b, s, 0, 0))
    out_shape = (bf(512), bf(512), val_t, f32(256),
                 jax.ShapeDtypeStruct((seq, nb * WIDTH), F32), f32(256),
                 bf(512), bf(512), val_t, f32(256),
                 bf(512), bf(512), val_t, f32(256),
                 jax.ShapeDtypeStruct((nb, seq // tm, SUBLANES, LANES), F32))
    out_specs = (tok(512), tok(512), val_t_spec, tok(256),
                 pl.BlockSpec((tm, WIDTH), lambda b, s: (s, b)), tok(256),
                 tok(512), tok(512), val_t_spec, tok(256),
                 tok(512), tok(512), val_t_spec, tok(256),
                 pl.BlockSpec((1, 1, SUBLANES, LANES), lambda b, s: (b, s, 0, 0)))
    in_specs = [tok(D_MODEL), _full((1, D_MODEL)), _full((D_MODEL, N_COLS)), _full((1, LANES)),
                _full((tm, tm)), _full((3 * LANES, 2 * LANES)), _full((1, 2 * LANES)),
                tab, tab, tab, tab, tab, tab,
                _full((1, MLA_Q_RANK)), _full((1, MLA_KV_RANK)),
                _full((MLA_Q_RANK, N_HEADS * LANES)), _full((MLA_KV_RANK, N_HEADS * LANES)),
                _full((2 * WIDTH, D_MODEL)), _full((WIDTH, MLA_KV_RANK))]
    return pl.pallas_call(
        functools.partial(_in_kernel, tm=tm),
        grid=(nb, seq // tm),
        in_specs=in_specs, out_specs=out_specs, out_shape=out_shape,
        scratch_shapes=[pltpu.VMEM((1, LANES), F32), pltpu.VMEM((LANES, WIDTH), F32)],
        compiler_params=pltpu.CompilerParams(dimension_semantics=("arbitrary", "arbitrary"),
                                             vmem_limit_bytes=VMEM_LIMIT),
        name="in_proj",
    )(x, g, w, fb, tri, route, bconst, *moba_t, *mla_t, gq, gkv, wuq, wukv, wvt, wuvt)


def _first_blocks(nb, nq, fstat=None):
    tiles = jnp.arange(nq)
    past = tiles[None, :] < tiles[:, None]
    if fstat is None:
        need = jnp.broadcast_to(past, (nb, 2, nq, nq))
    else:
        qn, kn = fstat[:, :, 0, :N_HEADS] * NORM_SLACK, fstat[:, :, 1, :N_HEADS] * NORM_SLACK
        cmax, cmin = fstat[:, :, 2, :N_HEADS], fstat[:, :, 3, :N_HEADS]
        gap = (qn[:, :, None] * kn[:, None, :] + cmax[:, :, None] - cmin[:, None, :]
               + (qn * kn)[:, :, None])
        need = (gap > -SKIP_GAP) & past[None, :, :, None]
        need = need.reshape(nb, nq, nq, 2, 2).any(-1).transpose(0, 3, 1, 2)
    first = jnp.where(need.any(-1), jnp.argmax(need, axis=-1), nq)
    first = jnp.minimum(first, jnp.maximum(tiles - 1, 0))
    count = jnp.sum(tiles - first, axis=-1, keepdims=True)
    return jnp.concatenate([first, count], axis=-1).astype(jnp.int32).reshape(-1)


def _attn_call(first_blocks, q, k, vt, gate, t, name):
    nb, seq, _ = q.shape
    whole = lambda width: pl.BlockSpec((1, seq, width), lambda b, p, tbl: (b, 0, p))
    return pl.pallas_call(
        functools.partial(_attn_kernel, t=t, nq=seq // t),
        grid_spec=pltpu.PrefetchScalarGridSpec(
            num_scalar_prefetch=1, grid=(nb, 2),
            in_specs=[whole(2 * LANES), whole(2 * LANES),
                      pl.BlockSpec((1, seq // t, LANES, t), lambda b, p, tbl: (b, 0, p, 0)), whole(LANES)],
            out_specs=whole(LANES),
            scratch_shapes=[pltpu.VMEM((2, t, t), F32), pltpu.VMEM((2, t, t), F32),
                            pltpu.VMEM((2, 1, t), F32), pltpu.VMEM((2, 1, t), F32),
                            pltpu.VMEM((seq // t, 2, 1, t), F32),
                            pltpu.VMEM((seq // t, 2, HEAD_DIM + DEN_ROWS, t), F32),
                            pltpu.VMEM((seq // t, 2, LANES, t), BF16),
                            pltpu.SMEM((2,), jnp.int32)]),
        out_shape=jax.ShapeDtypeStruct((nb, seq, WIDTH), BF16),
        compiler_params=pltpu.CompilerParams(dimension_semantics=("arbitrary", "arbitrary"),
                                             vmem_limit_bytes=VMEM_LIMIT),
        name=name,
    )(first_blocks, q, k, vt, gate)


def _s5_call(u, bm, lam2, cm, d, gw, gb, steps, nbatch):
    seq = u.shape[0]
    blk = steps * nbatch
    nchunks = seq // steps
    prev = lambda c: (jnp.maximum(c - 1, 0), 0)
    return pl.pallas_call(
        functools.partial(_s5_kernel, steps=steps, nbatch=nbatch, nchunks=nchunks),
        grid=(nchunks + 1,),
        in_specs=[pl.BlockSpec((steps, nbatch * WIDTH), lambda c: (jnp.minimum(c, nchunks - 1), 0)),
                  _full((WIDTH, 2 * S5_NSTATE)), _full((2, S5_NSTATE)), _full((2 * S5_NSTATE, WIDTH)),
                  _full((1, WIDTH)), _full((WIDTH, WIDTH)), _full((1, WIDTH))],
        out_specs=pl.BlockSpec((steps, nbatch * WIDTH), prev),
        out_shape=jax.ShapeDtypeStruct((seq, nbatch * WIDTH), F32),
        scratch_shapes=[pltpu.VMEM((blk, 2 * S5_NSTATE), F32), pltpu.VMEM((blk, 2 * S5_NSTATE), F32),
                        *[pltpu.VMEM((WIDTH // LANES, blk, LANES), F32)] * 3,
                        pltpu.VMEM((nbatch, 2 * S5_NSTATE), F32)],
        compiler_params=pltpu.CompilerParams(dimension_semantics=("arbitrary",),
                                             vmem_limit_bytes=VMEM_LIMIT),
        name="s5",
    )(u, bm, lam2, cm, d, gw, gb)


def _out_call(x, a, b2d, sg, c, d, w, fg, tm, final):
    nb, seq, _ = x.shape
    tok = lambda width: pl.BlockSpec((1, tm, width), lambda b, s: (b, s, 0), pipeline_mode=pl.Buffered(3))
    b_spec = pl.BlockSpec((tm, WIDTH), lambda b, s: (s, b), pipeline_mode=pl.Buffered(3))
    any_spec = pl.BlockSpec(memory_space=pl.ANY)
    vmem_spec = pl.BlockSpec(memory_space=pltpu.VMEM)

    def outer(x_hbm, a_hbm, b_hbm, sg_hbm, c_hbm, d_hbm, w_ref, fg_ref, o_hbm):
        def step(x_ref, a_ref, b_ref, sg_ref, c_ref, d_ref, o_ref):
            _out_kernel(x_ref, a_ref, b_ref, sg_ref, c_ref, d_ref, w_ref, fg_ref, o_ref, final=final)

        pltpu.emit_pipeline(
            step, grid=(nb, seq // tm),
            in_specs=[tok(D_MODEL), tok(WIDTH), b_spec, tok(WIDTH), tok(WIDTH), tok(WIDTH)],
            out_specs=[pl.BlockSpec((1, tm, D_MODEL), lambda b, s: (b, s, 0))],
        )(x_hbm, a_hbm, b_hbm, sg_hbm, c_hbm, d_hbm, o_hbm)

    return pl.pallas_call(
        outer,
        in_specs=[any_spec] * 6 + [vmem_spec, vmem_spec],
        out_specs=any_spec,
        out_shape=jax.ShapeDtypeStruct(x.shape, F32),
        compiler_params=pltpu.CompilerParams(vmem_limit_bytes=VMEM_LIMIT),
        name="out_proj",
    )(x, a, b2d, sg, c, d, w, fg)


def kernel(x, norm_g, w_in, fox_fb, s5_a_re, s5_a_im, s5_log_dt, s5_b_re, s5_b_im, s5_c_re, s5_c_im, s5_d,
           s5_glu_w, s5_glu_b, mla_q_norm, mla_w_uq, mla_kv_norm, mla_w_ukv, w_out, final_g):
    nb, seq, _ = x.shape
    depth = norm_g.shape[0]
    assert nb == 8, "the S5 recurrence keeps the batch on the 8 sublanes of a vreg"
    tm, tm_out, steps = _tiles(seq)
    assert seq % tm == 0 and seq % steps == 0 and seq // MOBA_BLOCK <= 32

    tri = jnp.tril(jnp.ones((tm, tm), F32)).astype(BF16)
    route, bconst = _fox_routing()
    moba_t, mla_t = _rope_tables(seq)
    fgain = final_g.astype(F32).reshape(1, D_MODEL)
    every_block = _first_blocks(nb, seq // tm)

    for l in range(depth):
        w, wvt = _arrange_w_in(w_in[l])
        fb = jnp.zeros((1, LANES), F32).at[0, :N_HEADS].set(fox_fb[l].astype(F32))
        wuq, wukv, wuvt = _arrange_mla(mla_w_uq[l].astype(F32), mla_w_ukv[l].astype(F32))
        (foxq, foxk, foxv, fg, su, sg, mobq, mobk, mobv, mg, mlaq, mlak, mlav, lg, fstat) = _in_call(
            x, norm_g[l].astype(F32).reshape(1, D_MODEL), w, fb, tri, route, bconst, moba_t, mla_t,
            mla_q_norm[l].astype(F32).reshape(1, MLA_Q_RANK), mla_kv_norm[l].astype(F32).reshape(1, MLA_KV_RANK),
            wuq, wukv, wvt, wuvt, tm)

        a_out = _attn_call(_first_blocks(nb, seq // tm, fstat), foxq, foxk, foxv, fg, tm, "fox_attn")
        c_out = _attn_call(every_block, mobq, mobk, mobv, mg, tm, "moba_attn")
        d_out = _attn_call(every_block, mlaq, mlak, mlav, lg, tm, "mla_attn")

        bm, lam2, cm = _s5_matrices(s5_a_re[l], s5_a_im[l], s5_log_dt[l], s5_b_re[l], s5_b_im[l],
                                    s5_c_re[l], s5_c_im[l])
        b_out = _s5_call(su, bm, lam2, cm, s5_d[l].astype(F32).reshape(1, WIDTH), s5_glu_w[l].astype(BF16),
                         s5_glu_b[l].astype(F32).reshape(1, WIDTH), steps, nb)
        x = _out_call(x, a_out, b_out, sg, c_out, d_out, w_out[l].astype(BF16), fgain, tm_out,
                      final=(l == depth - 1))
    return x
```
